```python
import jax, jax.numpy as jnp
from jax import lax
import numpy as np

D_MODEL = 2048
BATCH = 2
SEQ = 4096
DEPTH = 2
DEC_BATCH = 128
DEC_SEQ = 4
PAST_LEN = 8192
PAGE_SIZE = 128

MIX_DIM = D_MODEL
POOL_DIM = MIX_DIM // 2
POOL_WINDOWS = (2, 4, 8, 16)
N_POOL_GROUPS = len(POOL_WINDOWS)
POOL_GROUP_DIM = POOL_DIM // N_POOL_GROUPS
POOL_HIST = max(POOL_WINDOWS) - 1
HEAD_DIM = 64
N_HEADS = (MIX_DIM - POOL_DIM) // HEAD_DIM
N_KV_HEADS = 4
GROUP = N_HEADS // N_KV_HEADS
Q_DIM = N_HEADS * HEAD_DIM
KV_DIM = N_KV_HEADS * HEAD_DIM
IN_DIM = POOL_DIM + Q_DIM + 2 * KV_DIM
WINDOW = 128
BLOCK = WINDOW
FFN_DIM = 5632
RMS_EPS = 1e-5
ATTN_SCALE = HEAD_DIM ** -0.5

kernel_name = 'hymba_pool_swa_sink_macaron_step'


def _rmsnorm(x, g):
    xf = x.astype(jnp.float32)
    y = xf * lax.rsqrt(jnp.mean(xf * xf, axis=-1, keepdims=True) + RMS_EPS)
    return (y * g.astype(jnp.float32)).astype(x.dtype)


def _swiglu(x, wg, wu, wd):
    return (jax.nn.silu(x @ wg) * (x @ wu)) @ wd


def _pool_mix(u, hist, n_hist, pool_w, pool_scale):
    B, T, _ = u.shape
    h = jnp.concatenate([hist, u], axis=1)
    hf = h.astype(jnp.float32)
    cs = jnp.concatenate([jnp.zeros((B, 1, POOL_DIM), jnp.float32), jnp.cumsum(hf, axis=1)], axis=1)
    pos = jnp.arange(T)
    start = POOL_HIST + 1
    groups = []
    for g, w in enumerate(POOL_WINDOWS):
        sl = slice(g * POOL_GROUP_DIM, (g + 1) * POOL_GROUP_DIM)
        s = cs[:, start:start + T, sl] - cs[:, start - w:start - w + T, sl]
        cnt = jnp.minimum(w, n_hist + pos + 1).astype(jnp.float32)
        groups.append(s / cnt[None, :, None] - hf[:, POOL_HIST:, sl])
    d = jnp.stack(groups, axis=2).astype(u.dtype)
    y = jnp.einsum('btgc,gcd->btgd', d, pool_w).reshape(B, T, POOL_DIM) * pool_scale
    return y, h[:, -POOL_HIST:]


def _sink_softmax(scores, valid, sink):
    scores = jnp.where(valid, scores, -jnp.inf)
    m = jnp.maximum(jnp.max(scores, axis=-1, keepdims=True), sink)
    p = jnp.exp(scores - m)
    return p / (jnp.sum(p, axis=-1, keepdims=True) + jnp.exp(sink - m))


def _swa_prompt(q, k, v, sinks):
    B, S, _ = q.shape
    nb = S // BLOCK
    qb = q.reshape(B, nb, BLOCK, N_KV_HEADS, GROUP, HEAD_DIM)
    kb = k.reshape(B, nb, BLOCK, N_KV_HEADS, HEAD_DIM)
    vb = v.reshape(B, nb, BLOCK, N_KV_HEADS, HEAD_DIM)

    def with_prev(t):
        prev = jnp.concatenate([jnp.zeros_like(t[:, :1]), t[:, :-1]], axis=1)
        return jnp.concatenate([prev, t], axis=2)

    kk, vv = with_prev(kb), with_prev(vb)
    scores = jnp.einsum('bnqkgd,bnskd->bnkgqs', qb, kk,
                        preferred_element_type=jnp.float32) * ATTN_SCALE
    i = jnp.arange(BLOCK)[:, None]
    j = jnp.arange(2 * BLOCK)[None, :]
    rel = i + BLOCK - j
    key_pos = (jnp.arange(nb)[:, None, None] - 1) * BLOCK + j[None]
    valid = (rel >= 0)[None] & (rel < WINDOW)[None] & (key_pos >= 0)
    valid = valid[None, :, None, None]
    sink = sinks.astype(jnp.float32).reshape(1, 1, N_KV_HEADS, GROUP, 1, 1)
    p = _sink_softmax(scores, valid, sink)
    out = jnp.einsum('bnkgqs,bnskd->bnqkgd', p.astype(v.dtype), vv)
    return out.reshape(B, S, Q_DIM)


def _swa_sample(q, k_all, v_all, sinks, n_buf):
    B, T, _ = q.shape
    qg = q.reshape(B, T, N_KV_HEADS, GROUP, HEAD_DIM)
    scores = jnp.einsum('btkgd,bskd->bkgts', qg, k_all,
                        preferred_element_type=jnp.float32) * ATTN_SCALE
    rel = n_buf + jnp.arange(T)[:, None] - jnp.arange(n_buf + T)[None, :]
    valid = (rel >= 0) & (rel < WINDOW)
    sink = sinks.astype(jnp.float32).reshape(1, N_KV_HEADS, GROUP, 1, 1)
    p = _sink_softmax(scores, valid, sink)
    out = jnp.einsum('bkgts,bskd->btkgd', p.astype(v_all.dtype), v_all)
    return out.reshape(B, T, Q_DIM)


def _layer(x, pool_hist, n_hist, k_hist, v_hist,
           n1, wg1, wu1, wd1, nm, w_in, pool_w, pool_scale, sinks, w_out, n2, wg2, wu2, wd2):
    B, T, _ = x.shape
    x = x + 0.5 * _swiglu(_rmsnorm(x, n1), wg1, wu1, wd1)
    proj = _rmsnorm(x, nm) @ w_in
    u = proj[..., :POOL_DIM]
    q = proj[..., POOL_DIM:POOL_DIM + Q_DIM]
    k = proj[..., POOL_DIM + Q_DIM:POOL_DIM + Q_DIM + KV_DIM].reshape(B, T, N_KV_HEADS, HEAD_DIM)
    v = proj[..., POOL_DIM + Q_DIM + KV_DIM:].reshape(B, T, N_KV_HEADS, HEAD_DIM)
    pool_out, new_pool = _pool_mix(u, pool_hist, n_hist, pool_w, pool_scale)
    if k_hist is None:
        attn = _swa_prompt(q, k, v, sinks)
        keep = min(WINDOW, T)
        new_k, new_v = k[:, T - keep:], v[:, T - keep:]
    else:
        n_buf = k_hist.shape[1]
        k_all = jnp.concatenate([k_hist, k], axis=1)
        v_all = jnp.concatenate([v_hist, v], axis=1)
        attn = _swa_sample(q, k_all, v_all, sinks, n_buf)
        new_k, new_v = k_all[:, T:], v_all[:, T:]
    x = x + jnp.concatenate([pool_out, attn], axis=-1) @ w_out
    x = x + 0.5 * _swiglu(_rmsnorm(x, n2), wg2, wu2, wd2)
    return x, new_k, new_v, new_pool


def setup_inputs(seed: int = 0) -> dict:
    key = jax.random.key(seed)
    ks = jax.random.split(key, 24)
    f32 = jnp.float32
    n_buf = min(WINDOW, PAST_LEN)

    def nrm(k, shape, scale):
        return jax.random.normal(k, shape, f32) * scale

    def gain(k, shape):
        return 1.0 + 0.02 * jax.random.normal(k, shape, f32)

    return {
        'x_prompt': nrm(ks[0], (BATCH, SEQ, D_MODEL), 1.0),
        'x_sample': nrm(ks[1], (DEC_BATCH, DEC_SEQ, D_MODEL), 1.0),
        'cache_k': nrm(ks[2], (DEPTH, DEC_BATCH, n_buf, N_KV_HEADS, HEAD_DIM), 1.0),
        'cache_v': nrm(ks[3], (DEPTH, DEC_BATCH, n_buf, N_KV_HEADS, HEAD_DIM), 1.0),
        'state_pool': nrm(ks[4], (DEPTH, DEC_BATCH, POOL_HIST, POOL_DIM), 1.0),
        'norm_ffn1': gain(ks[5], (DEPTH, D_MODEL)),
        'ffn1_gate': nrm(ks[6], (DEPTH, D_MODEL, FFN_DIM), D_MODEL ** -0.5),
        'ffn1_up': nrm(ks[7], (DEPTH, D_MODEL, FFN_DIM), D_MODEL ** -0.5),
        'ffn1_down': nrm(ks[8], (DEPTH, FFN_DIM, D_MODEL), FFN_DIM ** -0.5),
        'norm_mix': gain(ks[9], (DEPTH, D_MODEL)),
        'w_in': nrm(ks[10], (DEPTH, D_MODEL, IN_DIM), D_MODEL ** -0.5),
        'pool_w': nrm(ks[11], (DEPTH, N_POOL_GROUPS, POOL_GROUP_DIM, POOL_GROUP_DIM), POOL_GROUP_DIM ** -0.5),
        'pool_scale': gain(ks[12], (DEPTH, POOL_DIM)),
        'attn_sinks': nrm(ks[13], (DEPTH, N_HEADS), 0.5),
        'w_out': nrm(ks[14], (DEPTH, MIX_DIM, D_MODEL), MIX_DIM ** -0.5),
        'norm_ffn2': gain(ks[15], (DEPTH, D_MODEL)),
        'ffn2_gate': nrm(ks[16], (DEPTH, D_MODEL, FFN_DIM), D_MODEL ** -0.5),
        'ffn2_up': nrm(ks[17], (DEPTH, D_MODEL, FFN_DIM), D_MODEL ** -0.5),
        'ffn2_down': nrm(ks[18], (DEPTH, FFN_DIM, D_MODEL), FFN_DIM ** -0.5),
        'final_norm': gain(ks[19], (D_MODEL,)),
    }


def reference(x_prompt, x_sample, cache_k, cache_v, state_pool,
              norm_ffn1, ffn1_gate, ffn1_up, ffn1_down, norm_mix, w_in, pool_w, pool_scale,
              attn_sinks, w_out, norm_ffn2, ffn2_gate, ffn2_up, ffn2_down, final_norm):
    yp, ys = x_prompt, x_sample
    n_hist_sample = min(POOL_HIST, PAST_LEN)
    kp_l, vp_l, pp_l, ks_l, vs_l, ps_l = [], [], [], [], [], []
    for l in range(DEPTH):
        w = (norm_ffn1[l], ffn1_gate[l], ffn1_up[l], ffn1_down[l], norm_mix[l], w_in[l],
             pool_w[l], pool_scale[l], attn_sinks[l], w_out[l], norm_ffn2[l],
             ffn2_gate[l], ffn2_up[l], ffn2_down[l])
        zero_hist = jnp.zeros((yp.shape[0], POOL_HIST, POOL_DIM), yp.dtype)
        yp, kp, vp, pp = _layer(yp, zero_hist, 0, None, None, *w)
        ys, ks_, vs_, ps_ = _layer(ys, state_pool[l], n_hist_sample, cache_k[l], cache_v[l], *w)
        kp_l.append(kp); vp_l.append(vp); pp_l.append(pp)
        ks_l.append(ks_); vs_l.append(vs_); ps_l.append(ps_)
    y_prompt = _rmsnorm(yp, final_norm)
    y_sample = _rmsnorm(ys, final_norm)
    return (y_prompt, y_sample,
            jnp.stack(kp_l), jnp.stack(vp_l), jnp.stack(pp_l),
            jnp.stack(ks_l), jnp.stack(vs_l), jnp.stack(ps_l))
```

```python
import functools

import jax
import jax.numpy as jnp
from jax import lax
from jax.experimental import pallas as pl
from jax.experimental.pallas import tpu as pltpu

F32 = jnp.float32
BF16 = jnp.bfloat16

D_MODEL = 2048
BATCH = 2
SEQ = 4096
DEPTH = 2
DEC_BATCH = 128
DEC_SEQ = 4
PAST_LEN = 8192

POOL_DIM = 1024
POOL_WINDOWS = (2, 4, 8, 16)
POOL_GROUP_DIM = POOL_DIM // len(POOL_WINDOWS)
POOL_HIST = max(POOL_WINDOWS) - 1
HEAD_DIM = 64
N_HEADS = 16
N_KV_HEADS = 4
GROUP = N_HEADS // N_KV_HEADS
Q_DIM = N_HEADS * HEAD_DIM
KV_DIM = N_KV_HEADS * HEAD_DIM
IN_DIM = POOL_DIM + Q_DIM + 2 * KV_DIM
WINDOW = 128
FFN_DIM = 5632
RMS_EPS = 1e-5
ATTN_SCALE = HEAD_DIM ** -0.5
N_BUF = min(WINDOW, PAST_LEN)
N_HIST_SAMPLE = min(POOL_HIST, PAST_LEN)

PROMPT_ROWS = BATCH * SEQ
SAMPLE_ROWS = DEC_BATCH * DEC_SEQ
M_ROWS = PROMPT_ROWS + SAMPLE_ROWS

VMEM_LIMIT_BYTES = 56 * 1024 * 1024

FFN_TM = 544
FFN_TF = 512
PROJ_TM = 544
MIX_TQ = 512
HALO_U = 16
POOL_SB = 32
ATTN_SB = 16
ATTN_ROWS = ATTN_SB * DEC_SEQ


def _rms(x, g):
    return x * lax.rsqrt(jnp.mean(x * x, axis=-1, keepdims=True) + RMS_EPS) * g


def _params(*sem):
    return pltpu.CompilerParams(dimension_semantics=sem,
                                vmem_limit_bytes=VMEM_LIMIT_BYTES)


def _ffn_body(x_ref, g_ref, wg_ref, wu_ref, wd_ref, gf_ref, o_ref, xn_ref, *,
              final_norm):
    f = pl.program_id(1)

    @pl.when(f == 0)
    def _():
        x = x_ref[...]
        xn_ref[...] = _rms(x, g_ref[...]).astype(BF16)
        o_ref[...] = x

    a = xn_ref[...]
    gate = jnp.dot(a, wg_ref[...], preferred_element_type=F32)
    up = jnp.dot(a, wu_ref[...], preferred_element_type=F32)
    h = (gate * jax.nn.sigmoid(gate) * up).astype(BF16)
    o_ref[...] += 0.5 * jnp.dot(h, wd_ref[...], preferred_element_type=F32)

    if final_norm:
        @pl.when(f == pl.num_programs(1) - 1)
        def _():
            o_ref[...] = _rms(o_ref[...], gf_ref[...])


def _ffn(x, g, wg, wu, wd, gf, *, final_norm):
    grid = (M_ROWS // FFN_TM, FFN_DIM // FFN_TF)
    return pl.pallas_call(
        functools.partial(_ffn_body, final_norm=final_norm),
        out_shape=jax.ShapeDtypeStruct((M_ROWS, D_MODEL), F32),
        grid=grid,
        in_specs=[
            pl.BlockSpec((FFN_TM, D_MODEL), lambda i, f: (i, 0)),
            pl.BlockSpec((1, D_MODEL), lambda i, f: (0, 0)),
            pl.BlockSpec((D_MODEL, FFN_TF), lambda i, f: (0, f)),
            pl.BlockSpec((D_MODEL, FFN_TF), lambda i, f: (0, f)),
            pl.BlockSpec((FFN_TF, D_MODEL), lambda i, f: (f, 0)),
            pl.BlockSpec((1, D_MODEL), lambda i, f: (0, 0)),
        ],
        out_specs=pl.BlockSpec((FFN_TM, D_MODEL), lambda i, f: (i, 0)),
        scratch_shapes=[pltpu.VMEM((FFN_TM, D_MODEL), BF16)],
        compiler_params=_params("parallel", "arbitrary"),
        name="ffn_final" if final_norm else "ffn",
    )(x, g, wg, wu, wd, gf)


def _inproj_body(x_ref, g_ref, w_ref, o_ref):
    xn = _rms(x_ref[...], g_ref[...]).astype(BF16)
    o_ref[...] = jnp.dot(xn, w_ref[...], preferred_element_type=F32)


def _inproj(x, g, w):
    return pl.pallas_call(
        _inproj_body,
        out_shape=jax.ShapeDtypeStruct((M_ROWS, IN_DIM), F32),
        grid=(M_ROWS // PROJ_TM,),
        in_specs=[
            pl.BlockSpec((PROJ_TM, D_MODEL), lambda i: (i, 0)),
            pl.BlockSpec((1, D_MODEL), lambda i: (0, 0)),
            pl.BlockSpec((D_MODEL, IN_DIM), lambda i: (0, 0)),
        ],
        out_specs=pl.BlockSpec((PROJ_TM, IN_DIM), lambda i: (i, 0)),
        compiler_params=_params("parallel"),
        name="inproj",
    )(x, g, w)


def _kv_head_of_lane(shape):
    return lax.broadcasted_iota(jnp.int32, shape, 1) // HEAD_DIM


def _pool_linear(d, gi, pw_ref, ps_ref):
    lanes = slice(gi * POOL_GROUP_DIM, (gi + 1) * POOL_GROUP_DIM)
    y = jnp.dot(d.astype(BF16), pw_ref[gi], preferred_element_type=F32)
    return y * ps_ref[:, lanes]


def _prompt_mix_body(sink_ref, u_ref, q_ref, kv_ref, kvh_ref, uh_ref, x_ref,
                     pw_ref, ps_ref, wo_ref, o_ref, mix_ref):
    tiles_per_seq = SEQ // MIX_TQ
    tile = pl.program_id(0) % tiles_per_seq
    first = tile == 0

    lo = jnp.where(first, WINDOW, 0)
    keep_u = jnp.full((HALO_U, POOL_DIM), lo, jnp.int32) == 0
    keep_kv = jnp.full((WINDOW, 2 * KV_DIM), lo, jnp.int32) == 0
    uh = jnp.where(keep_u, uh_ref[...], 0.0)
    pos = tile * MIX_TQ + lax.broadcasted_iota(jnp.int32, (MIX_TQ, 1), 0)
    for gi, w in enumerate(POOL_WINDOWS):
        lanes = slice(gi * POOL_GROUP_DIM, (gi + 1) * POOL_GROUP_DIM)
        xg = jnp.concatenate([uh[:, lanes], u_ref[:, lanes]], axis=0)
        s = xg
        sh = 1
        while sh < w:
            s = s + pltpu.roll(s, sh, axis=0)
            sh *= 2
        cnt = jnp.minimum(w, pos + 1).astype(F32)
        d = s[HALO_U:] / cnt - xg[HALO_U:]
        mix_ref[:, lanes] = _pool_linear(d, gi, pw_ref, ps_ref).astype(BF16)

    kvh = jnp.where(keep_kv, kvh_ref[...], 0.0)
    k_all = jnp.concatenate([kvh[:, :KV_DIM], kv_ref[:, :KV_DIM]], axis=0)
    v_all = jnp.concatenate([kvh[:, KV_DIM:], kv_ref[:, KV_DIM:]], axis=0)
    head = _kv_head_of_lane((2 * WINDOW, KV_DIM))
    rows = GROUP * WINDOW
    qi = lax.broadcasted_iota(jnp.int32, (rows, 2 * WINDOW), 0) % WINDOW
    kj = lax.broadcasted_iota(jnp.int32, (rows, 2 * WINDOW), 1)
    band = (kj > qi) & (kj <= qi + WINDOW)
    for n in range(MIX_TQ // WINDOW):
        kw = k_all[n * WINDOW:(n + 2) * WINDOW]
        vw = v_all[n * WINDOW:(n + 2) * WINDOW]
        qn = q_ref[n * WINDOW:(n + 1) * WINDOW, :].astype(BF16)
        qs = jnp.concatenate(
            [qn[:, g * KV_DIM:(g + 1) * KV_DIM] for g in range(GROUP)], axis=0)
        valid = band
        if n == 0:
            valid = band & (kj >= lo)
        probs = []
        for k in range(N_KV_HEADS):
            kk = jnp.where(head == k, kw, 0.0).astype(BF16)
            s = lax.dot_general(qs, kk, (((1,), (1,)), ((), ())),
                                preferred_element_type=F32)
            s = jnp.where(valid, s, -jnp.inf)
            sink = jnp.concatenate(
                [jnp.full((WINDOW, 1), sink_ref[k * GROUP + g], F32)
                 for g in range(GROUP)], axis=0)
            m = jnp.maximum(jnp.max(s, axis=-1, keepdims=True), sink)
            p = jnp.exp(s - m)
            den = jnp.sum(p, axis=-1, keepdims=True) + jnp.exp(sink - m)
            probs.append((p / den).astype(BF16))
        pcat = jnp.concatenate(probs, axis=1)
        vcat = jnp.concatenate(
            [jnp.where(head == k, vw, 0.0).astype(BF16)
             for k in range(N_KV_HEADS)], axis=0)
        o = jnp.dot(pcat, vcat, preferred_element_type=F32)
        for g in range(GROUP):
            mix_ref[n * WINDOW:(n + 1) * WINDOW,
                    POOL_DIM + g * KV_DIM:POOL_DIM + (g + 1) * KV_DIM] = (
                        o[g * WINDOW:(g + 1) * WINDOW].astype(BF16))

    o_ref[...] = x_ref[...] + jnp.dot(mix_ref[...], wo_ref[...],
                                      preferred_element_type=F32)


def _prompt_mix(sinks, proj, x, pool_w, pool_scale, w_out):
    blocks_kv = MIX_TQ // WINDOW
    blocks_u = MIX_TQ // HALO_U
    return pl.pallas_call(
        _prompt_mix_body,
        out_shape=jax.ShapeDtypeStruct((M_ROWS, D_MODEL), F32),
        grid=(PROMPT_ROWS // MIX_TQ,),
        in_specs=[
            pl.BlockSpec(memory_space=pltpu.SMEM),
            pl.BlockSpec((MIX_TQ, POOL_DIM), lambda i: (i, 0)),
            pl.BlockSpec((MIX_TQ, Q_DIM), lambda i: (i, 1)),
            pl.BlockSpec((MIX_TQ, 2 * KV_DIM), lambda i: (i, 4)),
            pl.BlockSpec((WINDOW, 2 * KV_DIM),
                         lambda i: (jnp.maximum(i * blocks_kv - 1, 0), 4)),
            pl.BlockSpec((HALO_U, POOL_DIM),
                         lambda i: (jnp.maximum(i * blocks_u - 1, 0), 0)),
            pl.BlockSpec((MIX_TQ, D_MODEL), lambda i: (i, 0)),
            pl.BlockSpec((len(POOL_WINDOWS), POOL_GROUP_DIM, POOL_GROUP_DIM),
                         lambda i: (0, 0, 0)),
            pl.BlockSpec((1, POOL_DIM), lambda i: (0, 0)),
            pl.BlockSpec((D_MODEL, D_MODEL), lambda i: (0, 0)),
        ],
        out_specs=pl.BlockSpec((MIX_TQ, D_MODEL), lambda i: (i, 0)),
        scratch_shapes=[pltpu.VMEM((MIX_TQ, D_MODEL), BF16)],
        compiler_params=_params("parallel"),
        name="prompt_mix",
    )(sinks, proj, proj, proj, proj, proj, x, pool_w, pool_scale, w_out)


def _sample_pool_body(hist_ref, proj_ref, d_ref, newpool_ref):
    for gi, w in enumerate(POOL_WINDOWS):
        base = gi * POOL_GROUP_DIM
        seq = [hist_ref[:, j * POOL_DIM + base:j * POOL_DIM + base + POOL_GROUP_DIM]
               for j in range(POOL_HIST)]
        seq += [proj_ref[:, t * IN_DIM + base:t * IN_DIM + base + POOL_GROUP_DIM]
                for t in range(DEC_SEQ)]
        for t in range(DEC_SEQ):
            last = POOL_HIST + t
            s = seq[last - w + 1]
            for j in range(last - w + 2, last + 1):
                s = s + seq[j]
            cnt = float(min(w, N_HIST_SAMPLE + t + 1))
            d_ref[:, t * POOL_DIM + base:t * POOL_DIM + base + POOL_GROUP_DIM] = (
                s / cnt - seq[last])
    for j in range(POOL_HIST - DEC_SEQ):
        newpool_ref[:, j * POOL_DIM:(j + 1) * POOL_DIM] = (
            hist_ref[:, (j + DEC_SEQ) * POOL_DIM:(j + DEC_SEQ + 1) * POOL_DIM])
    for t in range(DEC_SEQ):
        j = POOL_HIST - DEC_SEQ + t
        newpool_ref[:, j * POOL_DIM:(j + 1) * POOL_DIM] = (
            proj_ref[:, t * IN_DIM:t * IN_DIM + POOL_DIM])


def _sample_pool(hist_flat, proj_flat):
    return pl.pallas_call(
        _sample_pool_body,
        out_shape=(jax.ShapeDtypeStruct((DEC_BATCH, DEC_SEQ * POOL_DIM), F32),
                   jax.ShapeDtypeStruct((DEC_BATCH, POOL_HIST * POOL_DIM), F32)),
        grid=(DEC_BATCH // POOL_SB,),
        in_specs=[
            pl.BlockSpec((POOL_SB, POOL_HIST * POOL_DIM), lambda i: (i, 0)),
            pl.BlockSpec((POOL_SB, DEC_SEQ * IN_DIM), lambda i: (i, 0)),
        ],
        out_specs=(pl.BlockSpec((POOL_SB, DEC_SEQ * POOL_DIM), lambda i: (i, 0)),
                   pl.BlockSpec((POOL_SB, POOL_HIST * POOL_DIM), lambda i: (i, 0))),
        compiler_params=_params("parallel"),
        name="sample_pool",
    )(hist_flat, proj_flat)


def _sample_attn_body(sink_ref, d_ref, q_ref, kv_ref, ck_ref, cv_ref, x_ref,
                      pw_ref, ps_ref, wo_ref, xin_ref, o_ref, mix_ref):
    del xin_ref
    for gi in range(len(POOL_WINDOWS)):
        lanes = slice(gi * POOL_GROUP_DIM, (gi + 1) * POOL_GROUP_DIM)
        mix_ref[:, lanes] = _pool_linear(d_ref[:, lanes], gi, pw_ref, ps_ref).astype(BF16)

    pair = 2 * DEC_SEQ
    grp = GROUP * pair
    rows = N_KV_HEADS * grp
    keys = 2 * N_BUF
    assert keys == KV_DIM
    head = _kv_head_of_lane((rows, KV_DIM))
    row = lax.broadcasted_iota(jnp.int32, (rows, keys), 0)
    row_head = row // grp
    in_pair = row % pair
    low = in_pair < DEC_SEQ
    tok = in_pair % DEC_SEQ
    col = lax.broadcasted_iota(jnp.int32, (rows, keys), 1)
    new_col = col - N_BUF
    valid_cache = (col < N_BUF) & (col > tok)
    valid_new = ((new_col >= 0) & (new_col < pair)
                 & ((new_col // DEC_SEQ) == (in_pair // DEC_SEQ))
                 & ((new_col % DEC_SEQ) <= tok))
    valid = valid_cache | valid_new
    sink = jnp.concatenate(
        [jnp.full((pair, 1), sink_ref[k * GROUP + g], F32)
         for k in range(N_KV_HEADS) for g in range(GROUP)], axis=0)
    pad = jnp.zeros((keys - N_BUF - pair, KV_DIM), F32)

    for j in range(ATTN_SB // 2):
        r0 = j * pair
        q8 = q_ref[r0:r0 + pair, :]
        qs = jnp.concatenate(
            [q8[:, g * KV_DIM:(g + 1) * KV_DIM] for g in range(GROUP)], axis=0)
        qrep = jnp.concatenate([qs] * N_KV_HEADS, axis=0)
        lhs = jnp.where(head == row_head, qrep, 0.0).astype(BF16)
        knew = kv_ref[r0:r0 + pair, :KV_DIM]
        vnew = kv_ref[r0:r0 + pair, KV_DIM:]
        scores, values = [], []
        for b in (2 * j, 2 * j + 1):
            kx = jnp.concatenate([ck_ref[b], knew, pad], axis=0).astype(BF16)
            values.append(jnp.concatenate([cv_ref[b], vnew, pad], axis=0).astype(BF16))
            scores.append(lax.dot_general(lhs, kx, (((1,), (1,)), ((), ())),
                                          preferred_element_type=F32))
        s = jnp.where(low, scores[0], scores[1])
        s = jnp.where(valid, s, -jnp.inf)
        m = jnp.maximum(jnp.max(s, axis=-1, keepdims=True), sink)
        p = jnp.exp(s - m)
        den = jnp.sum(p, axis=-1, keepdims=True) + jnp.exp(sink - m)
        p = (p / den).astype(BF16)
        o = jnp.where(low,
                      jnp.dot(p, values[0], preferred_element_type=F32),
                      jnp.dot(p, values[1], preferred_element_type=F32))
        o = jnp.where(head == row_head, o, 0.0)
        og = o[0:grp]
        for k in range(1, N_KV_HEADS):
            og = og + o[k * grp:(k + 1) * grp]
        for g in range(GROUP):
            mix_ref[r0:r0 + pair,
                    POOL_DIM + g * KV_DIM:POOL_DIM + (g + 1) * KV_DIM] = (
                        og[g * pair:(g + 1) * pair].astype(BF16))

    o_ref[...] = x_ref[...] + jnp.dot(mix_ref[...], wo_ref[...],
                                      preferred_element_type=F32)


def _sample_attn(sinks, d_rows, proj, cache_k, cache_v, x, pool_w, pool_scale,
                 w_out, x_mixed):
    row0 = PROMPT_ROWS // ATTN_ROWS
    return pl.pallas_call(
        _sample_attn_body,
        out_shape=jax.ShapeDtypeStruct((M_ROWS, D_MODEL), F32),
        grid=(DEC_BATCH // ATTN_SB,),
        in_specs=[
            pl.BlockSpec(memory_space=pltpu.SMEM),
            pl.BlockSpec((ATTN_ROWS, POOL_DIM), lambda i: (i, 0)),
            pl.BlockSpec((ATTN_ROWS, Q_DIM), lambda i: (row0 + i, 1)),
            pl.BlockSpec((ATTN_ROWS, 2 * KV_DIM), lambda i: (row0 + i, 4)),
            pl.BlockSpec((ATTN_SB, N_BUF, KV_DIM), lambda i: (i, 0, 0)),
            pl.BlockSpec((ATTN_SB, N_BUF, KV_DIM), lambda i: (i, 0, 0)),
            pl.BlockSpec((ATTN_ROWS, D_MODEL), lambda i: (row0 + i, 0)),
            pl.BlockSpec((len(POOL_WINDOWS), POOL_GROUP_DIM, POOL_GROUP_DIM),
                         lambda i: (0, 0, 0)),
            pl.BlockSpec((1, POOL_DIM), lambda i: (0, 0)),
            pl.BlockSpec((D_MODEL, D_MODEL), lambda i: (0, 0)),
            pl.BlockSpec(memory_space=pl.ANY),
        ],
        out_specs=pl.BlockSpec((ATTN_ROWS, D_MODEL), lambda i: (row0 + i, 0)),
        scratch_shapes=[pltpu.VMEM((ATTN_ROWS, D_MODEL), BF16)],
        input_output_aliases={10: 0},
        compiler_params=_params("parallel"),
        name="sample_attn",
    )(sinks, d_rows, proj, proj, cache_k, cache_v, x, pool_w, pool_scale, w_out,
      x_mixed)


def _group_major(a, axis):
    shape = a.shape
    a = a.reshape(shape[:axis] + (N_KV_HEADS, GROUP, HEAD_DIM) + shape[axis + 1:])
    a = jnp.swapaxes(a, axis, axis + 1)
    return a.reshape(shape)


def kernel(x_prompt, x_sample, cache_k, cache_v, state_pool, norm_ffn1, ffn1_gate,
           ffn1_up, ffn1_down, norm_mix, w_in, pool_w, pool_scale, attn_sinks,
           w_out, norm_ffn2, ffn2_gate, ffn2_up, ffn2_down, final_norm):
    x = jnp.concatenate([x_prompt.reshape(PROMPT_ROWS, D_MODEL),
                         x_sample.reshape(SAMPLE_ROWS, D_MODEL)], axis=0)
    gf = final_norm.reshape(1, D_MODEL)
    kp_l, vp_l, pp_l, ks_l, vs_l, ps_l = [], [], [], [], [], []
    for l in range(DEPTH):
        wq = _group_major(w_in[l][:, POOL_DIM:POOL_DIM + Q_DIM], 1) * ATTN_SCALE
        w_in_l = jnp.concatenate(
            [w_in[l][:, :POOL_DIM], wq, w_in[l][:, POOL_DIM + Q_DIM:]], axis=1).astype(BF16)
        w_out_l = jnp.concatenate(
            [w_out[l][:POOL_DIM], _group_major(w_out[l][POOL_DIM:], 0)], axis=0).astype(BF16)
        pool_w_l = pool_w[l].astype(BF16)
        pool_scale_l = pool_scale[l].reshape(1, POOL_DIM)
        sinks_l = attn_sinks[l].astype(F32)

        x = _ffn(x, norm_ffn1[l].reshape(1, D_MODEL), ffn1_gate[l].astype(BF16),
                 ffn1_up[l].astype(BF16), ffn1_down[l].astype(BF16), gf,
                 final_norm=False)
        proj = _inproj(x, norm_mix[l].reshape(1, D_MODEL), w_in_l)

        proj_p = proj[:PROMPT_ROWS].reshape(BATCH, SEQ, IN_DIM)
        keep = min(WINDOW, SEQ)
        kp_l.append(proj_p[:, SEQ - keep:, POOL_DIM + Q_DIM:POOL_DIM + Q_DIM + KV_DIM]
                    .reshape(BATCH, keep, N_KV_HEADS, HEAD_DIM))
        vp_l.append(proj_p[:, SEQ - keep:, POOL_DIM + Q_DIM + KV_DIM:]
                    .reshape(BATCH, keep, N_KV_HEADS, HEAD_DIM))
        pp_l.append(proj_p[:, SEQ - POOL_HIST:, :POOL_DIM])

        proj_s = proj[PROMPT_ROWS:]
        k_new = proj_s[:, POOL_DIM + Q_DIM:POOL_DIM + Q_DIM + KV_DIM].reshape(
            DEC_BATCH, DEC_SEQ, N_KV_HEADS, HEAD_DIM)
        v_new = proj_s[:, POOL_DIM + Q_DIM + KV_DIM:].reshape(
            DEC_BATCH, DEC_SEQ, N_KV_HEADS, HEAD_DIM)
        ks_l.append(jnp.concatenate([cache_k[l], k_new], axis=1)[:, DEC_SEQ:])
        vs_l.append(jnp.concatenate([cache_v[l], v_new], axis=1)[:, DEC_SEQ:])

        d_flat, new_pool = _sample_pool(
            state_pool[l].reshape(DEC_BATCH, POOL_HIST * POOL_DIM),
            proj_s.reshape(DEC_BATCH, DEC_SEQ * IN_DIM))
        ps_l.append(new_pool.reshape(DEC_BATCH, POOL_HIST, POOL_DIM))

        x_mixed = _prompt_mix(sinks_l, proj, x, pool_w_l, pool_scale_l, w_out_l)
        x = _sample_attn(sinks_l, d_flat.reshape(SAMPLE_ROWS, POOL_DIM), proj,
                         cache_k[l].reshape(DEC_BATCH, N_BUF, KV_DIM),
                         cache_v[l].reshape(DEC_BATCH, N_BUF, KV_DIM),
                         x, pool_w_l, pool_scale_l, w_out_l, x_mixed)

        x = _ffn(x, norm_ffn2[l].reshape(1, D_MODEL), ffn2_gate[l].astype(BF16),
                 ffn2_up[l].astype(BF16), ffn2_down[l].astype(BF16), gf,
                 final_norm=(l == DEPTH - 1))

    y_prompt = x[:PROMPT_ROWS].reshape(BATCH, SEQ, D_MODEL)
    y_sample = x[PROMPT_ROWS:].reshape(DEC_BATCH, DEC_SEQ, D_MODEL)
    return (y_prompt, y_sample, jnp.stack(kp_l), jnp.stack(vp_l), jnp.stack(pp_l),
            jnp.stack(ks_l), jnp.stack(vs_l), jnp.stack(ps_l))
```

```python
import functools

import jax
import jax.numpy as jnp
from jax import lax
from jax.experimental import pallas as pl
from jax.experimental.pallas import tpu as pltpu

F32 = jnp.float32
BF16 = jnp.bfloat16

D_MODEL = 2048
BATCH = 2
SEQ = 4096
DEPTH = 2
DEC_BATCH = 128
DEC_SEQ = 4
PAST_LEN = 8192

POOL_DIM = 1024
POOL_WINDOWS = (2, 4, 8, 16)
POOL_GROUP_DIM = POOL_DIM // len(POOL_WINDOWS)
POOL_HIST = max(POOL_WINDOWS) - 1
HEAD_DIM = 64
N_HEADS = 16
N_KV_HEADS = 4
GROUP = N_HEADS // N_KV_HEADS
Q_DIM = N_HEADS * HEAD_DIM
KV_DIM = N_KV_HEADS * HEAD_DIM
IN_DIM = POOL_DIM + Q_DIM + 2 * KV_DIM
WINDOW = 128
FFN_DIM = 5632
RMS_EPS = 1e-5
ATTN_SCALE = HEAD_DIM ** -0.5
N_BUF = min(WINDOW, PAST_LEN)
N_HIST_SAMPLE = min(POOL_HIST, PAST_LEN)

PROMPT_ROWS = BATCH * SEQ
SAMPLE_ROWS = DEC_BATCH * DEC_SEQ
M_ROWS = PROMPT_ROWS + SAMPLE_ROWS

VMEM_LIMIT_BYTES = 56 * 1024 * 1024

FFN_TM = 1088
FFN_TF = 512
PROJ_TM = 544
MIX_TQ = 512
HALO_U = 16
POOL_SB = 32
ATTN_SB = 16
ATTN_ROWS = ATTN_SB * DEC_SEQ


def _rms(x, g):
    return x * lax.rsqrt(jnp.mean(x * x, axis=-1, keepdims=True) + RMS_EPS) * g


def _params(*sem):
    return pltpu.CompilerParams(dimension_semantics=sem,
                                vmem_limit_bytes=VMEM_LIMIT_BYTES)


def _ffn_body(x_hbm, g_ref, wg_ref, wu_ref, wd_ref, gf_ref, o_hbm,
              acc_ref, xn_ref, in_sem, out_sem, *, final_norm):
    i = pl.program_id(0)
    f = pl.program_id(1)
    n_tiles = pl.num_programs(0)
    last_f = pl.num_programs(1) - 1
    slot = i % 2

    def x_copy(tile, s):
        return pltpu.make_async_copy(x_hbm.at[pl.ds(tile * FFN_TM, FFN_TM)],
                                     acc_ref.at[s], in_sem.at[s])

    def o_copy(tile, s):
        return pltpu.make_async_copy(acc_ref.at[s],
                                     o_hbm.at[pl.ds(tile * FFN_TM, FFN_TM)],
                                     out_sem.at[s])

    @pl.when((i == 0) & (f == 0))
    def _():
        x_copy(0, 0).start()

    @pl.when(f == 0)
    def _():
        x_copy(i, slot).wait()
        xn_ref[...] = _rms(acc_ref[slot], g_ref[...]).astype(BF16)

    a = xn_ref[...]
    gate = jnp.dot(a, wg_ref[...], preferred_element_type=F32)
    up = jnp.dot(a, wu_ref[...], preferred_element_type=F32)
    h = (gate * jax.nn.sigmoid(gate) * up).astype(BF16)
    acc_ref[slot] += 0.5 * jnp.dot(h, wd_ref[...], preferred_element_type=F32)

    @pl.when(f == 1)
    def _():
        @pl.when(i >= 1)
        def _():
            o_copy(i - 1, 1 - slot).wait()

        @pl.when(i + 1 < n_tiles)
        def _():
            x_copy(i + 1, 1 - slot).start()

    @pl.when(f == last_f)
    def _():
        if final_norm:
            acc_ref[slot] = _rms(acc_ref[slot], gf_ref[...])
        o_copy(i, slot).start()

        @pl.when(i == n_tiles - 1)
        def _():
            o_copy(i, slot).wait()


def _ffn(x, g, wg, wu, wd, gf, *, final_norm):
    grid = (M_ROWS // FFN_TM, FFN_DIM // FFN_TF)
    assert grid[1] >= 2
    return pl.pallas_call(
        functools.partial(_ffn_body, final_norm=final_norm),
        out_shape=jax.ShapeDtypeStruct((M_ROWS, D_MODEL), F32),
        grid=grid,
        in_specs=[
            pl.BlockSpec(memory_space=pl.ANY),
            pl.BlockSpec((1, D_MODEL), lambda i, f: (0, 0)),
            pl.BlockSpec((D_MODEL, FFN_TF), lambda i, f: (0, f)),
            pl.BlockSpec((D_MODEL, FFN_TF), lambda i, f: (0, f)),
            pl.BlockSpec((FFN_TF, D_MODEL), lambda i, f: (f, 0)),
            pl.BlockSpec((1, D_MODEL), lambda i, f: (0, 0)),
        ],
        out_specs=pl.BlockSpec(memory_space=pl.ANY),
        scratch_shapes=[pltpu.VMEM((2, FFN_TM, D_MODEL), F32),
                        pltpu.VMEM((FFN_TM, D_MODEL), BF16),
                        pltpu.SemaphoreType.DMA((2,)),
                        pltpu.SemaphoreType.DMA((2,))],
        compiler_params=_params("arbitrary", "arbitrary"),
        name="ffn_final" if final_norm else "ffn",
    )(x, g, wg, wu, wd, gf)


def _inproj_body(x_ref, g_ref, w_ref, o_ref):
    xn = _rms(x_ref[...], g_ref[...]).astype(BF16)
    o_ref[...] = jnp.dot(xn, w_ref[...], preferred_element_type=F32)


def _inproj(x, g, w):
    return pl.pallas_call(
        _inproj_body,
        out_shape=jax.ShapeDtypeStruct((M_ROWS, IN_DIM), F32),
        grid=(M_ROWS // PROJ_TM,),
        in_specs=[
            pl.BlockSpec((PROJ_TM, D_MODEL), lambda i: (i, 0)),
            pl.BlockSpec((1, D_MODEL), lambda i: (0, 0)),
            pl.BlockSpec((D_MODEL, IN_DIM), lambda i: (0, 0)),
        ],
        out_specs=pl.BlockSpec((PROJ_TM, IN_DIM), lambda i: (i, 0)),
        compiler_params=_params("parallel"),
        name="inproj",
    )(x, g, w)


def _kv_head_of_lane(shape):
    return lax.broadcasted_iota(jnp.int32, shape, 1) // HEAD_DIM


def _pool_linear(d, gi, pw_ref, ps_ref):
    lanes = slice(gi * POOL_GROUP_DIM, (gi + 1) * POOL_GROUP_DIM)
    y = jnp.dot(d.astype(BF16), pw_ref[gi], preferred_element_type=F32)
    return y * ps_ref[:, lanes]


def _prompt_mix_body(sink_ref, u_ref, q_ref, kv_ref, kvh_ref, uh_ref, x_ref,
                     pw_ref, ps_ref, wo_ref, o_ref, mix_ref):
    tiles_per_seq = SEQ // MIX_TQ
    tile = pl.program_id(0) % tiles_per_seq
    first = tile == 0

    lo = jnp.where(first, WINDOW, 0)
    keep_u = jnp.full((HALO_U, POOL_DIM), lo, jnp.int32) == 0
    keep_kv = jnp.full((WINDOW, 2 * KV_DIM), lo, jnp.int32) == 0
    uh = jnp.where(keep_u, uh_ref[...], 0.0)
    pos = tile * MIX_TQ + lax.broadcasted_iota(jnp.int32, (MIX_TQ, 1), 0)
    for gi, w in enumerate(POOL_WINDOWS):
        lanes = slice(gi * POOL_GROUP_DIM, (gi + 1) * POOL_GROUP_DIM)
        xg = jnp.concatenate([uh[:, lanes], u_ref[:, lanes]], axis=0)
        s = xg
        sh = 1
        while sh < w:
            s = s + pltpu.roll(s, sh, axis=0)
            sh *= 2
        cnt = jnp.minimum(w, pos + 1).astype(F32)
        d = s[HALO_U:] / cnt - xg[HALO_U:]
        mix_ref[:, lanes] = _pool_linear(d, gi, pw_ref, ps_ref).astype(BF16)

    kvh = jnp.where(keep_kv, kvh_ref[...], 0.0)
    k_all = jnp.concatenate([kvh[:, :KV_DIM], kv_ref[:, :KV_DIM]], axis=0)
    v_all = jnp.concatenate([kvh[:, KV_DIM:], kv_ref[:, KV_DIM:]], axis=0)
    head = _kv_head_of_lane((2 * WINDOW, KV_DIM))
    rows = GROUP * WINDOW
    qi = lax.broadcasted_iota(jnp.int32, (rows, 2 * WINDOW), 0) % WINDOW
    kj = lax.broadcasted_iota(jnp.int32, (rows, 2 * WINDOW), 1)
    band = (kj > qi) & (kj <= qi + WINDOW)
    for n in range(MIX_TQ // WINDOW):
        kw = k_all[n * WINDOW:(n + 2) * WINDOW]
        vw = v_all[n * WINDOW:(n + 2) * WINDOW]
        qn = q_ref[n * WINDOW:(n + 1) * WINDOW, :].astype(BF16)
        qs = jnp.concatenate(
            [qn[:, g * KV_DIM:(g + 1) * KV_DIM] for g in range(GROUP)], axis=0)
        valid = band
        if n == 0:
            valid = band & (kj >= lo)
        probs = []
        for k in range(N_KV_HEADS):
            kk = jnp.where(head == k, kw, 0.0).astype(BF16)
            s = lax.dot_general(qs, kk, (((1,), (1,)), ((), ())),
                                preferred_element_type=F32)
            s = jnp.where(valid, s, -jnp.inf)
            sink = jnp.concatenate(
                [jnp.full((WINDOW, 1), sink_ref[k * GROUP + g], F32)
                 for g in range(GROUP)], axis=0)
            m = jnp.maximum(jnp.max(s, axis=-1, keepdims=True), sink)
            p = jnp.exp(s - m)
            den = jnp.sum(p, axis=-1, keepdims=True) + jnp.exp(sink - m)
            probs.append((p / den).astype(BF16))
        pcat = jnp.concatenate(probs, axis=1)
        vcat = jnp.concatenate(
            [jnp.where(head == k, vw, 0.0).astype(BF16)
             for k in range(N_KV_HEADS)], axis=0)
        o = jnp.dot(pcat, vcat, preferred_element_type=F32)
        for g in range(GROUP):
            mix_ref[n * WINDOW:(n + 1) * WINDOW,
                    POOL_DIM + g * KV_DIM:POOL_DIM + (g + 1) * KV_DIM] = (
                        o[g * WINDOW:(g + 1) * WINDOW].astype(BF16))

    o_ref[...] = x_ref[...] + jnp.dot(mix_ref[...], wo_ref[...],
                                      preferred_element_type=F32)


def _prompt_mix(sinks, proj, x, pool_w, pool_scale, w_out):
    blocks_kv = MIX_TQ // WINDOW
    blocks_u = MIX_TQ // HALO_U
    return pl.pallas_call(
        _prompt_mix_body,
        out_shape=jax.ShapeDtypeStruct((M_ROWS, D_MODEL), F32),
        grid=(PROMPT_ROWS // MIX_TQ,),
        in_specs=[
            pl.BlockSpec(memory_space=pltpu.SMEM),
            pl.BlockSpec((MIX_TQ, POOL_DIM), lambda i: (i, 0)),
            pl.BlockSpec((MIX_TQ, Q_DIM), lambda i: (i, 1)),
            pl.BlockSpec((MIX_TQ, 2 * KV_DIM), lambda i: (i, 4)),
            pl.BlockSpec((WINDOW, 2 * KV_DIM),
                         lambda i: (jnp.maximum(i * blocks_kv - 1, 0), 4)),
            pl.BlockSpec((HALO_U, POOL_DIM),
                         lambda i: (jnp.maximum(i * blocks_u - 1, 0), 0)),
            pl.BlockSpec((MIX_TQ, D_MODEL), lambda i: (i, 0)),
            pl.BlockSpec((len(POOL_WINDOWS), POOL_GROUP_DIM, POOL_GROUP_DIM),
                         lambda i: (0, 0, 0)),
            pl.BlockSpec((1, POOL_DIM), lambda i: (0, 0)),
            pl.BlockSpec((D_MODEL, D_MODEL), lambda i: (0, 0)),
        ],
        out_specs=pl.BlockSpec((MIX_TQ, D_MODEL), lambda i: (i, 0)),
        scratch_shapes=[pltpu.VMEM((MIX_TQ, D_MODEL), BF16)],
        compiler_params=_params("parallel"),
        name="prompt_mix",
    )(sinks, proj, proj, proj, proj, proj, x, pool_w, pool_scale, w_out)


def _sample_pool_body(hist_ref, proj_ref, d_ref, newpool_ref):
    for gi, w in enumerate(POOL_WINDOWS):
        base = gi * POOL_GROUP_DIM
        seq = [hist_ref[:, j * POOL_DIM + base:j * POOL_DIM + base + POOL_GROUP_DIM]
               for j in range(POOL_HIST)]
        seq += [proj_ref[:, t * IN_DIM + base:t * IN_DIM + base + POOL_GROUP_DIM]
                for t in range(DEC_SEQ)]
        for t in range(DEC_SEQ):
            last = POOL_HIST + t
            s = seq[last - w + 1]
            for j in range(last - w + 2, last + 1):
                s = s + seq[j]
            cnt = float(min(w, N_HIST_SAMPLE + t + 1))
            d_ref[:, t * POOL_DIM + base:t * POOL_DIM + base + POOL_GROUP_DIM] = (
                s / cnt - seq[last])
    for j in range(POOL_HIST - DEC_SEQ):
        newpool_ref[:, j * POOL_DIM:(j + 1) * POOL_DIM] = (
            hist_ref[:, (j + DEC_SEQ) * POOL_DIM:(j + DEC_SEQ + 1) * POOL_DIM])
    for t in range(DEC_SEQ):
        j = POOL_HIST - DEC_SEQ + t
        newpool_ref[:, j * POOL_DIM:(j + 1) * POOL_DIM] = (
            proj_ref[:, t * IN_DIM:t * IN_DIM + POOL_DIM])


def _sample_pool(hist_flat, proj_flat):
    return pl.pallas_call(
        _sample_pool_body,
        out_shape=(jax.ShapeDtypeStruct((DEC_BATCH, DEC_SEQ * POOL_DIM), F32),
                   jax.ShapeDtypeStruct((DEC_BATCH, POOL_HIST * POOL_DIM), F32)),
        grid=(DEC_BATCH // POOL_SB,),
        in_specs=[
            pl.BlockSpec((POOL_SB, POOL_HIST * POOL_DIM), lambda i: (i, 0)),
            pl.BlockSpec((POOL_SB, DEC_SEQ * IN_DIM), lambda i: (i, 0)),
        ],
        out_specs=(pl.BlockSpec((POOL_SB, DEC_SEQ * POOL_DIM), lambda i: (i, 0)),
                   pl.BlockSpec((POOL_SB, POOL_HIST * POOL_DIM), lambda i: (i, 0))),
        compiler_params=_params("parallel"),
        name="sample_pool",
    )(hist_flat, proj_flat)


def _sample_attn_body(sink_ref, d_ref, q_ref, kv_ref, ck_ref, cv_ref, x_ref,
                      pw_ref, ps_ref, wo_ref, xin_ref, o_ref, mix_ref):
    del xin_ref
    for gi in range(len(POOL_WINDOWS)):
        lanes = slice(gi * POOL_GROUP_DIM, (gi + 1) * POOL_GROUP_DIM)
        mix_ref[:, lanes] = _pool_linear(d_ref[:, lanes], gi, pw_ref, ps_ref).astype(BF16)

    pair = 2 * DEC_SEQ
    grp = GROUP * pair
    rows = N_KV_HEADS * grp
    keys = 2 * N_BUF
    assert keys == KV_DIM
    head = _kv_head_of_lane((rows, KV_DIM))
    row = lax.broadcasted_iota(jnp.int32, (rows, keys), 0)
    row_head = row // grp
    in_pair = row % pair
    low = in_pair < DEC_SEQ
    tok = in_pair % DEC_SEQ
    col = lax.broadcasted_iota(jnp.int32, (rows, keys), 1)
    new_col = col - N_BUF
    valid_cache = (col < N_BUF) & (col > tok)
    valid_new = ((new_col >= 0) & (new_col < pair)
                 & ((new_col // DEC_SEQ) == (in_pair // DEC_SEQ))
                 & ((new_col % DEC_SEQ) <= tok))
    valid = valid_cache | valid_new
    sink = jnp.concatenate(
        [jnp.full((pair, 1), sink_ref[k * GROUP + g], F32)
         for k in range(N_KV_HEADS) for g in range(GROUP)], axis=0)
    pad = jnp.zeros((keys - N_BUF - pair, KV_DIM), F32)

    for j in range(ATTN_SB // 2):
        r0 = j * pair
        q8 = q_ref[r0:r0 + pair, :]
        qs = jnp.concatenate(
            [q8[:, g * KV_DIM:(g + 1) * KV_DIM] for g in range(GROUP)], axis=0)
        qrep = jnp.concatenate([qs] * N_KV_HEADS, axis=0)
        lhs = jnp.where(head == row_head, qrep, 0.0).astype(BF16)
        knew = kv_ref[r0:r0 + pair, :KV_DIM]
        vnew = kv_ref[r0:r0 + pair, KV_DIM:]
        scores, values = [], []
        for b in (2 * j, 2 * j + 1):
            kx = jnp.concatenate([ck_ref[b], knew, pad], axis=0).astype(BF16)
            values.append(jnp.concatenate([cv_ref[b], vnew, pad], axis=0).astype(BF16))
            scores.append(lax.dot_general(lhs, kx, (((1,), (1,)), ((), ())),
                                          preferred_element_type=F32))
        s = jnp.where(low, scores[0], scores[1])
        s = jnp.where(valid, s, -jnp.inf)
        m = jnp.maximum(jnp.max(s, axis=-1, keepdims=True), sink)
        p = jnp.exp(s - m)
        den = jnp.sum(p, axis=-1, keepdims=True) + jnp.exp(sink - m)
        p = (p / den).astype(BF16)
        o = jnp.where(low,
                      jnp.dot(p, values[0], preferred_element_type=F32),
                      jnp.dot(p, values[1], preferred_element_type=F32))
        o = jnp.where(head == row_head, o, 0.0)
        og = o[0:grp]
        for k in range(1, N_KV_HEADS):
            og = og + o[k * grp:(k + 1) * grp]
        for g in range(GROUP):
            mix_ref[r0:r0 + pair,
                    POOL_DIM + g * KV_DIM:POOL_DIM + (g + 1) * KV_DIM] = (
                        og[g * pair:(g + 1) * pair].astype(BF16))

    o_ref[...] = x_ref[...] + jnp.dot(mix_ref[...], wo_ref[...],
                                      preferred_element_type=F32)


def _sample_attn(sinks, d_rows, proj, cache_k, cache_v, x, pool_w, pool_scale,
                 w_out, x_mixed):
    row0 = PROMPT_ROWS // ATTN_ROWS
    return pl.pallas_call(
        _sample_attn_body,
        out_shape=jax.ShapeDtypeStruct((M_ROWS, D_MODEL), F32),
        grid=(DEC_BATCH // ATTN_SB,),
        in_specs=[
            pl.BlockSpec(memory_space=pltpu.SMEM),
            pl.BlockSpec((ATTN_ROWS, POOL_DIM), lambda i: (i, 0)),
            pl.BlockSpec((ATTN_ROWS, Q_DIM), lambda i: (row0 + i, 1)),
            pl.BlockSpec((ATTN_ROWS, 2 * KV_DIM), lambda i: (row0 + i, 4)),
            pl.BlockSpec((ATTN_SB, N_BUF, KV_DIM), lambda i: (i, 0, 0)),
            pl.BlockSpec((ATTN_SB, N_BUF, KV_DIM), lambda i: (i, 0, 0)),
            pl.BlockSpec((ATTN_ROWS, D_MODEL), lambda i: (row0 + i, 0)),
            pl.BlockSpec((len(POOL_WINDOWS), POOL_GROUP_DIM, POOL_GROUP_DIM),
                         lambda i: (0, 0, 0)),
            pl.BlockSpec((1, POOL_DIM), lambda i: (0, 0)),
            pl.BlockSpec((D_MODEL, D_MODEL), lambda i: (0, 0)),
            pl.BlockSpec(memory_space=pl.ANY),
        ],
        out_specs=pl.BlockSpec((ATTN_ROWS, D_MODEL), lambda i: (row0 + i, 0)),
        scratch_shapes=[pltpu.VMEM((ATTN_ROWS, D_MODEL), BF16)],
        input_output_aliases={10: 0},
        compiler_params=_params("parallel"),
        name="sample_attn",
    )(sinks, d_rows, proj, proj, cache_k, cache_v, x, pool_w, pool_scale, w_out,
      x_mixed)


def _group_major(a, axis):
    shape = a.shape
    a = a.reshape(shape[:axis] + (N_KV_HEADS, GROUP, HEAD_DIM) + shape[axis + 1:])
    a = jnp.swapaxes(a, axis, axis + 1)
    return a.reshape(shape)


def kernel(x_prompt, x_sample, cache_k, cache_v, state_pool, norm_ffn1, ffn1_gate,
           ffn1_up, ffn1_down, norm_mix, w_in, pool_w, pool_scale, attn_sinks,
           w_out, norm_ffn2, ffn2_gate, ffn2_up, ffn2_down, final_norm):
    x = jnp.concatenate([x_prompt.reshape(PROMPT_ROWS, D_MODEL),
                         x_sample.reshape(SAMPLE_ROWS, D_MODEL)], axis=0)
    gf = final_norm.reshape(1, D_MODEL)
    kp_l, vp_l, pp_l, ks_l, vs_l, ps_l = [], [], [], [], [], []
    for l in range(DEPTH):
        wq = _group_major(w_in[l][:, POOL_DIM:POOL_DIM + Q_DIM], 1) * ATTN_SCALE
        w_in_l = jnp.concatenate(
            [w_in[l][:, :POOL_DIM], wq, w_in[l][:, POOL_DIM + Q_DIM:]], axis=1).astype(BF16)
        w_out_l = jnp.concatenate(
            [w_out[l][:POOL_DIM], _group_major(w_out[l][POOL_DIM:], 0)], axis=0).astype(BF16)
        pool_w_l = pool_w[l].astype(BF16)
        pool_scale_l = pool_scale[l].reshape(1, POOL_DIM)
        sinks_l = attn_sinks[l].astype(F32)

        x = _ffn(x, norm_ffn1[l].reshape(1, D_MODEL), ffn1_gate[l].astype(BF16),
                 ffn1_up[l].astype(BF16), ffn1_down[l].astype(BF16), gf,
                 final_norm=False)
        proj = _inproj(x, norm_mix[l].reshape(1, D_MODEL), w_in_l)

        keep = min(WINDOW, SEQ)
        k0 = POOL_DIM + Q_DIM
        tails = [proj[(b + 1) * SEQ - keep:(b + 1) * SEQ] for b in range(BATCH)]
        kp_l.append(jnp.stack([t[:, k0:k0 + KV_DIM] for t in tails])
                    .reshape(BATCH, keep, N_KV_HEADS, HEAD_DIM))
        vp_l.append(jnp.stack([t[:, k0 + KV_DIM:] for t in tails])
                    .reshape(BATCH, keep, N_KV_HEADS, HEAD_DIM))
        pp_l.append(jnp.stack([t[keep - POOL_HIST:, :POOL_DIM] for t in tails]))

        proj_s = proj[PROMPT_ROWS:]
        k_new = proj_s[:, POOL_DIM + Q_DIM:POOL_DIM + Q_DIM + KV_DIM].reshape(
            DEC_BATCH, DEC_SEQ, N_KV_HEADS, HEAD_DIM)
        v_new = proj_s[:, POOL_DIM + Q_DIM + KV_DIM:].reshape(
            DEC_BATCH, DEC_SEQ, N_KV_HEADS, HEAD_DIM)
        ks_l.append(jnp.concatenate([cache_k[l], k_new], axis=1)[:, DEC_SEQ:])
        vs_l.append(jnp.concatenate([cache_v[l], v_new], axis=1)[:, DEC_SEQ:])

        d_flat, new_pool = _sample_pool(
            state_pool[l].reshape(DEC_BATCH, POOL_HIST * POOL_DIM),
            proj_s.reshape(DEC_BATCH, DEC_SEQ * IN_DIM))
        ps_l.append(new_pool.reshape(DEC_BATCH, POOL_HIST, POOL_DIM))

        x_mixed = _prompt_mix(sinks_l, proj, x, pool_w_l, pool_scale_l, w_out_l)
        x = _sample_attn(sinks_l, d_flat.reshape(SAMPLE_ROWS, POOL_DIM), proj,
                         cache_k[l].reshape(DEC_BATCH, N_BUF, KV_DIM),
                         cache_v[l].reshape(DEC_BATCH, N_BUF, KV_DIM),
                         x, pool_w_l, pool_scale_l, w_out_l, x_mixed)

        x = _ffn(x, norm_ffn2[l].reshape(1, D_MODEL), ffn2_gate[l].astype(BF16),
                 ffn2_up[l].astype(BF16), ffn2_down[l].astype(BF16), gf,
                 final_norm=(l == DEPTH - 1))

    y_prompt = x[:PROMPT_ROWS].reshape(BATCH, SEQ, D_MODEL)
    y_sample = x[PROMPT_ROWS:].reshape(DEC_BATCH, DEC_SEQ, D_MODEL)
    return (y_prompt, y_sample, jnp.stack(kp_l), jnp.stack(vp_l), jnp.stack(pp_l),
            jnp.stack(ks_l), jnp.stack(vs_l), jnp.stack(ps_l))
```

```python
import functools

import jax
import jax.numpy as jnp
from jax import lax
from jax.experimental import pallas as pl
from jax.experimental.pallas import tpu as pltpu

F32 = jnp.float32
BF16 = jnp.bfloat16

D_MODEL = 2048
BATCH = 2
SEQ = 4096
DEPTH = 2
DEC_BATCH = 128
DEC_SEQ = 4
PAST_LEN = 8192

POOL_DIM = 1024
POOL_WINDOWS = (2, 4, 8, 16)
POOL_GROUP_DIM = POOL_DIM // len(POOL_WINDOWS)
POOL_HIST = max(POOL_WINDOWS) - 1
HEAD_DIM = 64
N_HEADS = 16
N_KV_HEADS = 4
GROUP = N_HEADS // N_KV_HEADS
Q_DIM = N_HEADS * HEAD_DIM
KV_DIM = N_KV_HEADS * HEAD_DIM
IN_DIM = POOL_DIM + Q_DIM + 2 * KV_DIM
WINDOW = 128
FFN_DIM = 5632
RMS_EPS = 1e-5
ATTN_SCALE = HEAD_DIM ** -0.5
N_BUF = min(WINDOW, PAST_LEN)
N_HIST_SAMPLE = min(POOL_HIST, PAST_LEN)

PROMPT_ROWS = BATCH * SEQ
SAMPLE_ROWS = DEC_BATCH * DEC_SEQ
M_ROWS = PROMPT_ROWS + SAMPLE_ROWS

VMEM_LIMIT_BYTES = 56 * 1024 * 1024

FFN_TM = 1088
FFN_TF = 512
FFN_TILES = M_ROWS // FFN_TM
FFN_CHUNKS = FFN_DIM // FFN_TF
FFN_FULL_TILES = PROMPT_ROWS // FFN_TM
FFN_SPLIT = PROMPT_ROWS - FFN_FULL_TILES * FFN_TM
CAST_ROWS = D_MODEL // FFN_TILES
PROJ_TM = 512
PROJ_PROMPT_TILES = PROMPT_ROWS // PROJ_TM
MIX_TQ = 512
HALO_U = 16
POOL_SB = 32
ATTN_SB = 16
ATTN_ROWS = ATTN_SB * DEC_SEQ

assert FFN_TILES * FFN_TM == M_ROWS and FFN_CHUNKS * FFN_TF == FFN_DIM
assert FFN_FULL_TILES + 1 == FFN_TILES and FFN_SPLIT + SAMPLE_ROWS == FFN_TM
assert FFN_SPLIT % 8 == 0 and CAST_ROWS * FFN_TILES == D_MODEL
assert FFN_CHUNKS >= 2
assert SAMPLE_ROWS == PROJ_TM and PROJ_PROMPT_TILES * PROJ_TM == PROMPT_ROWS


def _rms(x, g):
    return x * lax.rsqrt(jnp.mean(x * x, axis=-1, keepdims=True) + RMS_EPS) * g


def _params(*sem):
    return pltpu.CompilerParams(dimension_semantics=sem,
                                vmem_limit_bytes=VMEM_LIMIT_BYTES)


def _ffn_body(*refs, final_norm, cast_next):
    xp_hbm, xs_hbm, g_ref, wg_ref, wu_ref, wd_ref, gf_ref = refs[:7]
    refs = refs[7:]
    if cast_next:
        ng_ref, nu_ref, nd_ref = refs[:3]
        refs = refs[3:]
    op_hbm, os_hbm = refs[:2]
    refs = refs[2:]
    if cast_next:
        cg_ref, cu_ref, cd_ref = refs[:3]
        refs = refs[3:]
    acc_ref, xn_ref, in_sem, out_sem = refs

    i = pl.program_id(0)
    f = pl.program_id(1)
    slot = i % 2

    def copies(p_hbm, s_hbm, tile, s, sem, mixed, to_vmem):
        if mixed:
            pairs = [(p_hbm.at[pl.ds(FFN_FULL_TILES * FFN_TM, FFN_SPLIT)],
                      acc_ref.at[s, pl.ds(0, FFN_SPLIT)]),
                     (s_hbm, acc_ref.at[s, pl.ds(FFN_SPLIT, SAMPLE_ROWS)])]
        else:
            pairs = [(p_hbm.at[pl.ds(tile * FFN_TM, FFN_TM)], acc_ref.at[s])]
        return [pltpu.make_async_copy(h, v, sem.at[s]) if to_vmem
                else pltpu.make_async_copy(v, h, sem.at[s]) for h, v in pairs]

    def on_tile(tile, s, op, to_vmem):
        p_hbm, s_hbm, sem = ((xp_hbm, xs_hbm, in_sem) if to_vmem
                             else (op_hbm, os_hbm, out_sem))

        @pl.when(tile < FFN_FULL_TILES)
        def _():
            for c in copies(p_hbm, s_hbm, tile, s, sem, False, to_vmem):
                op(c)

        @pl.when(tile == FFN_FULL_TILES)
        def _():
            for c in copies(p_hbm, s_hbm, tile, s, sem, True, to_vmem):
                op(c)

    start = lambda c: c.start()
    wait = lambda c: c.wait()

    @pl.when((i == 0) & (f == 0))
    def _():
        on_tile(i, slot, start, True)

    @pl.when(f == 0)
    def _():
        on_tile(i, slot, wait, True)
        xn_ref[...] = _rms(acc_ref[slot], g_ref[...]).astype(BF16)

    a = xn_ref[...]
    gate = jnp.dot(a, wg_ref[...], preferred_element_type=F32)
    up = jnp.dot(a, wu_ref[...], preferred_element_type=F32)
    h = (gate * jax.nn.sigmoid(gate) * up).astype(BF16)
    acc_ref[slot] += 0.5 * jnp.dot(h, wd_ref[...], preferred_element_type=F32)

    if cast_next:
        cg_ref[...] = ng_ref[...].astype(BF16)
        cu_ref[...] = nu_ref[...].astype(BF16)
        cd_ref[...] = nd_ref[...].astype(BF16)

    @pl.when(f == 1)
    def _():
        @pl.when(i >= 1)
        def _():
            on_tile(i - 1, 1 - slot, wait, False)

        @pl.when(i + 1 < FFN_TILES)
        def _():
            on_tile(i + 1, 1 - slot, start, True)

    @pl.when(f == FFN_CHUNKS - 1)
    def _():
        if final_norm:
            acc_ref[slot] = _rms(acc_ref[slot], gf_ref[...])
        on_tile(i, slot, start, False)

        @pl.when(i == FFN_TILES - 1)
        def _():
            on_tile(i, slot, wait, False)


def _ffn(xp, xs, g, wg, wu, wd, gf, next_w=None, *, final_norm=False):
    cast_next = next_w is not None
    vec = pl.BlockSpec((1, D_MODEL), lambda i, f: (0, 0))
    in_specs = [
        pl.BlockSpec(memory_space=pl.ANY),
        pl.BlockSpec(memory_space=pl.ANY),
        vec,
        pl.BlockSpec((D_MODEL, FFN_TF), lambda i, f: (0, f)),
        pl.BlockSpec((D_MODEL, FFN_TF), lambda i, f: (0, f)),
        pl.BlockSpec((FFN_TF, D_MODEL), lambda i, f: (f, 0)),
        vec,
    ]
    out_shape = [jax.ShapeDtypeStruct((PROMPT_ROWS, D_MODEL), F32),
                 jax.ShapeDtypeStruct((SAMPLE_ROWS, D_MODEL), F32)]
    out_specs = [pl.BlockSpec(memory_space=pl.ANY), pl.BlockSpec(memory_space=pl.ANY)]
    args = [xp, xs, g, wg, wu, wd, gf]
    if cast_next:
        nl = next_w[0]
        in_specs += [
            pl.BlockSpec((None, CAST_ROWS, FFN_TF), lambda i, f: (nl, i, f)),
            pl.BlockSpec((None, CAST_ROWS, FFN_TF), lambda i, f: (nl, i, f)),
            pl.BlockSpec((None, FFN_TF, CAST_ROWS), lambda i, f: (nl, f, i)),
        ]
        out_shape += [jax.ShapeDtypeStruct((D_MODEL, FFN_DIM), BF16),
                      jax.ShapeDtypeStruct((D_MODEL, FFN_DIM), BF16),
                      jax.ShapeDtypeStruct((FFN_DIM, D_MODEL), BF16)]
        out_specs += [
            pl.BlockSpec((CAST_ROWS, FFN_TF), lambda i, f: (i, f)),
            pl.BlockSpec((CAST_ROWS, FFN_TF), lambda i, f: (i, f)),
            pl.BlockSpec((FFN_TF, CAST_ROWS), lambda i, f: (f, i)),
        ]
        args += list(next_w[1:])
    outs = pl.pallas_call(
        functools.partial(_ffn_body, final_norm=final_norm, cast_next=cast_next),
        out_shape=out_shape,
        grid=(FFN_TILES, FFN_CHUNKS),
        in_specs=in_specs,
        out_specs=out_specs,
        scratch_shapes=[pltpu.VMEM((2, FFN_TM, D_MODEL), F32),
                        pltpu.VMEM((FFN_TM, D_MODEL), BF16),
                        pltpu.SemaphoreType.DMA((2,)),
                        pltpu.SemaphoreType.DMA((2,))],
        compiler_params=_params("arbitrary", "arbitrary"),
        name="ffn_final" if final_norm else "ffn",
    )(*args)
    return outs[0], outs[1], tuple(outs[2:])


def _cast_ffn_body(g_ref, u_ref, d_ref, og_ref, ou_ref, od_ref):
    og_ref[...] = g_ref[...].astype(BF16)
    ou_ref[...] = u_ref[...].astype(BF16)
    od_ref[...] = d_ref[...].astype(BF16)


def _cast_ffn(layer, gate, up, down):
    return pl.pallas_call(
        _cast_ffn_body,
        out_shape=(jax.ShapeDtypeStruct((D_MODEL, FFN_DIM), BF16),
                   jax.ShapeDtypeStruct((D_MODEL, FFN_DIM), BF16),
                   jax.ShapeDtypeStruct((FFN_DIM, D_MODEL), BF16)),
        grid=(FFN_CHUNKS,),
        in_specs=[
            pl.BlockSpec((None, D_MODEL, FFN_TF), lambda j: (layer, 0, j)),
            pl.BlockSpec((None, D_MODEL, FFN_TF), lambda j: (layer, 0, j)),
            pl.BlockSpec((None, FFN_TF, D_MODEL), lambda j: (layer, j, 0)),
        ],
        out_specs=(pl.BlockSpec((D_MODEL, FFN_TF), lambda j: (0, j)),
                   pl.BlockSpec((D_MODEL, FFN_TF), lambda j: (0, j)),
                   pl.BlockSpec((FFN_TF, D_MODEL), lambda j: (j, 0))),
        compiler_params=_params("parallel"),
        name="cast_ffn",
    )(gate, up, down)


def _inproj_body(xp_ref, xs_ref, g_ref, w_ref, pp_ref, ps_ref):
    i = pl.program_id(0)

    def run(x_ref, o_ref):
        xn = _rms(x_ref[...], g_ref[...]).astype(BF16)
        o_ref[...] = jnp.dot(xn, w_ref[...], preferred_element_type=F32)

    @pl.when(i < PROJ_PROMPT_TILES)
    def _():
        run(xp_ref, pp_ref)

    @pl.when(i == PROJ_PROMPT_TILES)
    def _():
        run(xs_ref, ps_ref)


def _inproj(xp, xs, g, w):
    prompt_tile = lambda i: (jnp.minimum(i, PROJ_PROMPT_TILES - 1), 0)
    fixed = lambda i: (0, 0)
    return pl.pallas_call(
        _inproj_body,
        out_shape=(jax.ShapeDtypeStruct((PROMPT_ROWS, IN_DIM), F32),
                   jax.ShapeDtypeStruct((SAMPLE_ROWS, IN_DIM), F32)),
        grid=(PROJ_PROMPT_TILES + 1,),
        in_specs=[
            pl.BlockSpec((PROJ_TM, D_MODEL), prompt_tile),
            pl.BlockSpec((PROJ_TM, D_MODEL), fixed),
            pl.BlockSpec((1, D_MODEL), fixed),
            pl.BlockSpec((D_MODEL, IN_DIM), fixed),
        ],
        out_specs=(pl.BlockSpec((PROJ_TM, IN_DIM), prompt_tile),
                   pl.BlockSpec((PROJ_TM, IN_DIM), fixed)),
        compiler_params=_params("arbitrary"),
        name="inproj",
    )(xp, xs, g, w)


def _kv_head_of_lane(shape):
    return lax.broadcasted_iota(jnp.int32, shape, 1) // HEAD_DIM


def _pool_linear(d, gi, pw_ref, ps_ref):
    lanes = slice(gi * POOL_GROUP_DIM, (gi + 1) * POOL_GROUP_DIM)
    y = jnp.dot(d.astype(BF16), pw_ref[gi], preferred_element_type=F32)
    return y * ps_ref[:, lanes]


def _prompt_mix_body(sink_ref, u_ref, q_ref, kv_ref, kvh_ref, uh_ref, x_ref,
                     pw_ref, ps_ref, wo_ref, o_ref, mix_ref):
    tiles_per_seq = SEQ // MIX_TQ
    tile = pl.program_id(0) % tiles_per_seq
    first = tile == 0

    lo = jnp.where(first, WINDOW, 0)
    keep_u = jnp.full((HALO_U, POOL_DIM), lo, jnp.int32) == 0
    keep_kv = jnp.full((WINDOW, 2 * KV_DIM), lo, jnp.int32) == 0
    uh = jnp.where(keep_u, uh_ref[...], 0.0)
    pos = tile * MIX_TQ + lax.broadcasted_iota(jnp.int32, (MIX_TQ, 1), 0)
    for gi, w in enumerate(POOL_WINDOWS):
        lanes = slice(gi * POOL_GROUP_DIM, (gi + 1) * POOL_GROUP_DIM)
        xg = jnp.concatenate([uh[:, lanes], u_ref[:, lanes]], axis=0)
        s = xg
        sh = 1
        while sh < w:
            s = s + pltpu.roll(s, sh, axis=0)
            sh *= 2
        cnt = jnp.minimum(w, pos + 1).astype(F32)
        d = s[HALO_U:] / cnt - xg[HALO_U:]
        mix_ref[:, lanes] = _pool_linear(d, gi, pw_ref, ps_ref).astype(BF16)

    kvh = jnp.where(keep_kv, kvh_ref[...], 0.0)
    k_all = jnp.concatenate([kvh[:, :KV_DIM], kv_ref[:, :KV_DIM]], axis=0)
    v_all = jnp.concatenate([kvh[:, KV_DIM:], kv_ref[:, KV_DIM:]], axis=0)
    head = _kv_head_of_lane((2 * WINDOW, KV_DIM))
    rows = GROUP * WINDOW
    qi = lax.broadcasted_iota(jnp.int32, (rows, 2 * WINDOW), 0) % WINDOW
    kj = lax.broadcasted_iota(jnp.int32, (rows, 2 * WINDOW), 1)
    band = (kj > qi) & (kj <= qi + WINDOW)
    for n in range(MIX_TQ // WINDOW):
        kw = k_all[n * WINDOW:(n + 2) * WINDOW]
        vw = v_all[n * WINDOW:(n + 2) * WINDOW]
        qn = q_ref[n * WINDOW:(n + 1) * WINDOW, :].astype(BF16)
        qs = jnp.concatenate(
            [qn[:, g * KV_DIM:(g + 1) * KV_DIM] for g in range(GROUP)], axis=0)
        valid = band
        if n == 0:
            valid = band & (kj >= lo)
        probs = []
        for k in range(N_KV_HEADS):
            kk = jnp.where(head == k, kw, 0.0).astype(BF16)
            s = lax.dot_general(qs, kk, (((1,), (1,)), ((), ())),
                                preferred_element_type=F32)
            s = jnp.where(valid, s, -jnp.inf)
            sink = jnp.concatenate(
                [jnp.full((WINDOW, 1), sink_ref[k * GROUP + g], F32)
                 for g in range(GROUP)], axis=0)
            m = jnp.maximum(jnp.max(s, axis=-1, keepdims=True), sink)
            p = jnp.exp(s - m)
            den = jnp.sum(p, axis=-1, keepdims=True) + jnp.exp(sink - m)
            probs.append((p / den).astype(BF16))
        pcat = jnp.concatenate(probs, axis=1)
        vcat = jnp.concatenate(
            [jnp.where(head == k, vw, 0.0).astype(BF16)
             for k in range(N_KV_HEADS)], axis=0)
        o = jnp.dot(pcat, vcat, preferred_element_type=F32)
        for g in range(GROUP):
            mix_ref[n * WINDOW:(n + 1) * WINDOW,
                    POOL_DIM + g * KV_DIM:POOL_DIM + (g + 1) * KV_DIM] = (
                        o[g * WINDOW:(g + 1) * WINDOW].astype(BF16))

    o_ref[...] = x_ref[...] + jnp.dot(mix_ref[...], wo_ref[...],
                                      preferred_element_type=F32)


def _prompt_mix(sinks, proj, x, pool_w, pool_scale, w_out):
    blocks_kv = MIX_TQ // WINDOW
    blocks_u = MIX_TQ // HALO_U
    return pl.pallas_call(
        _prompt_mix_body,
        out_shape=jax.ShapeDtypeStruct((PROMPT_ROWS, D_MODEL), F32),
        grid=(PROMPT_ROWS // MIX_TQ,),
        in_specs=[
            pl.BlockSpec(memory_space=pltpu.SMEM),
            pl.BlockSpec((MIX_TQ, POOL_DIM), lambda i: (i, 0)),
            pl.BlockSpec((MIX_TQ, Q_DIM), lambda i: (i, 1)),
            pl.BlockSpec((MIX_TQ, 2 * KV_DIM), lambda i: (i, 4)),
            pl.BlockSpec((WINDOW, 2 * KV_DIM),
                         lambda i: (jnp.maximum(i * blocks_kv - 1, 0), 4)),
            pl.BlockSpec((HALO_U, POOL_DIM),
                         lambda i: (jnp.maximum(i * blocks_u - 1, 0), 0)),
            pl.BlockSpec((MIX_TQ, D_MODEL), lambda i: (i, 0)),
            pl.BlockSpec((len(POOL_WINDOWS), POOL_GROUP_DIM, POOL_GROUP_DIM),
                         lambda i: (0, 0, 0)),
            pl.BlockSpec((1, POOL_DIM), lambda i: (0, 0)),
            pl.BlockSpec((D_MODEL, D_MODEL), lambda i: (0, 0)),
        ],
        out_specs=pl.BlockSpec((MIX_TQ, D_MODEL), lambda i: (i, 0)),
        scratch_shapes=[pltpu.VMEM((MIX_TQ, D_MODEL), BF16)],
        compiler_params=_params("parallel"),
        name="prompt_mix",
    )(sinks, proj, proj, proj, proj, proj, x, pool_w, pool_scale, w_out)


def _sample_pool_body(hist_ref, proj_ref, d_ref, newpool_ref):
    for gi, w in enumerate(POOL_WINDOWS):
        base = gi * POOL_GROUP_DIM
        seq = [hist_ref[j, :, base:base + POOL_GROUP_DIM] for j in range(POOL_HIST)]
        seq += [proj_ref[:, t * IN_DIM + base:t * IN_DIM + base + POOL_GROUP_DIM]
                for t in range(DEC_SEQ)]
        for t in range(DEC_SEQ):
            last = POOL_HIST + t
            s = seq[last - w + 1]
            for j in range(last - w + 2, last + 1):
                s = s + seq[j]
            cnt = float(min(w, N_HIST_SAMPLE + t + 1))
            d_ref[:, t * POOL_DIM + base:t * POOL_DIM + base + POOL_GROUP_DIM] = (
                s / cnt - seq[last])
    for j in range(POOL_HIST - DEC_SEQ):
        newpool_ref[j] = hist_ref[j + DEC_SEQ]
    for t in range(DEC_SEQ):
        newpool_ref[POOL_HIST - DEC_SEQ + t] = proj_ref[:, t * IN_DIM:t * IN_DIM + POOL_DIM]


def _sample_pool(layer, state_t, proj_flat):
    return pl.pallas_call(
        _sample_pool_body,
        out_shape=(jax.ShapeDtypeStruct((DEC_BATCH, DEC_SEQ * POOL_DIM), F32),
                   jax.ShapeDtypeStruct((POOL_HIST, DEC_BATCH, POOL_DIM), F32)),
        grid=(DEC_BATCH // POOL_SB,),
        in_specs=[
            pl.BlockSpec((None, POOL_HIST, POOL_SB, POOL_DIM), lambda i: (layer, 0, i, 0)),
            pl.BlockSpec((POOL_SB, DEC_SEQ * IN_DIM), lambda i: (i, 0)),
        ],
        out_specs=(pl.BlockSpec((POOL_SB, DEC_SEQ * POOL_DIM), lambda i: (i, 0)),
                   pl.BlockSpec((POOL_HIST, POOL_SB, POOL_DIM), lambda i: (0, i, 0))),
        compiler_params=_params("parallel"),
        name="sample_pool",
    )(state_t, proj_flat)


def _sample_attn_body(sink_ref, d_ref, q_ref, kv_ref, ck_ref, cv_ref, x_ref,
                      pw_ref, ps_ref, wo_ref, o_ref, mix_ref):
    for gi in range(len(POOL_WINDOWS)):
        lanes = slice(gi * POOL_GROUP_DIM, (gi + 1) * POOL_GROUP_DIM)
        mix_ref[:, lanes] = _pool_linear(d_ref[:, lanes], gi, pw_ref, ps_ref).astype(BF16)

    pair = 2 * DEC_SEQ
    grp = GROUP * pair
    rows = N_KV_HEADS * grp
    keys = 2 * N_BUF
    assert keys == KV_DIM
    head = _kv_head_of_lane((rows, KV_DIM))
    row = lax.broadcasted_iota(jnp.int32, (rows, keys), 0)
    row_head = row // grp
    in_pair = row % pair
    low = in_pair < DEC_SEQ
    tok = in_pair % DEC_SEQ
    col = lax.broadcasted_iota(jnp.int32, (rows, keys), 1)
    new_col = col - N_BUF
    valid_cache = (col < N_BUF) & (col > tok)
    valid_new = ((new_col >= 0) & (new_col < pair)
                 & ((new_col // DEC_SEQ) == (in_pair // DEC_SEQ))
                 & ((new_col % DEC_SEQ) <= tok))
    valid = valid_cache | valid_new
    sink = jnp.concatenate(
        [jnp.full((pair, 1), sink_ref[k * GROUP + g], F32)
         for k in range(N_KV_HEADS) for g in range(GROUP)], axis=0)
    pad = jnp.zeros((keys - N_BUF - pair, KV_DIM), F32)

    for j in range(ATTN_SB // 2):
        r0 = j * pair
        q8 = q_ref[r0:r0 + pair, :]
        qs = jnp.concatenate(
            [q8[:, g * KV_DIM:(g + 1) * KV_DIM] for g in range(GROUP)], axis=0)
        qrep = jnp.concatenate([qs] * N_KV_HEADS, axis=0)
        lhs = jnp.where(head == row_head, qrep, 0.0).astype(BF16)
        knew = kv_ref[r0:r0 + pair, :KV_DIM]
        vnew = kv_ref[r0:r0 + pair, KV_DIM:]
        scores, values = [], []
        for b in (2 * j, 2 * j + 1):
            kx = jnp.concatenate([ck_ref[b], knew, pad], axis=0).astype(BF16)
            values.append(jnp.concatenate([cv_ref[b], vnew, pad], axis=0).astype(BF16))
            scores.append(lax.dot_general(lhs, kx, (((1,), (1,)), ((), ())),
                                          preferred_element_type=F32))
        s = jnp.where(low, scores[0], scores[1])
        s = jnp.where(valid, s, -jnp.inf)
        m = jnp.maximum(jnp.max(s, axis=-1, keepdims=True), sink)
        p = jnp.exp(s - m)
        den = jnp.sum(p, axis=-1, keepdims=True) + jnp.exp(sink - m)
        p = (p / den).astype(BF16)
        o = jnp.where(low,
                      jnp.dot(p, values[0], preferred_element_type=F32),
                      jnp.dot(p, values[1], preferred_element_type=F32))
        o = jnp.where(head == row_head, o, 0.0)
        og = o[0:grp]
        for k in range(1, N_KV_HEADS):
            og = og + o[k * grp:(k + 1) * grp]
        for g in range(GROUP):
            mix_ref[r0:r0 + pair,
                    POOL_DIM + g * KV_DIM:POOL_DIM + (g + 1) * KV_DIM] = (
                        og[g * pair:(g + 1) * pair].astype(BF16))

    o_ref[...] = x_ref[...] + jnp.dot(mix_ref[...], wo_ref[...],
                                      preferred_element_type=F32)


def _sample_attn(sinks, d_rows, proj, cache_k, cache_v, x, pool_w, pool_scale, w_out):
    return pl.pallas_call(
        _sample_attn_body,
        out_shape=jax.ShapeDtypeStruct((SAMPLE_ROWS, D_MODEL), F32),
        grid=(DEC_BATCH // ATTN_SB,),
        in_specs=[
            pl.BlockSpec(memory_space=pltpu.SMEM),
            pl.BlockSpec((ATTN_ROWS, POOL_DIM), lambda i: (i, 0)),
            pl.BlockSpec((ATTN_ROWS, Q_DIM), lambda i: (i, 1)),
            pl.BlockSpec((ATTN_ROWS, 2 * KV_DIM), lambda i: (i, 4)),
            pl.BlockSpec((ATTN_SB, N_BUF, KV_DIM), lambda i: (i, 0, 0)),
            pl.BlockSpec((ATTN_SB, N_BUF, KV_DIM), lambda i: (i, 0, 0)),
            pl.BlockSpec((ATTN_ROWS, D_MODEL), lambda i: (i, 0)),
            pl.BlockSpec((len(POOL_WINDOWS), POOL_GROUP_DIM, POOL_GROUP_DIM),
                         lambda i: (0, 0, 0)),
            pl.BlockSpec((1, POOL_DIM), lambda i: (0, 0)),
            pl.BlockSpec((D_MODEL, D_MODEL), lambda i: (0, 0)),
        ],
        out_specs=pl.BlockSpec((ATTN_ROWS, D_MODEL), lambda i: (i, 0)),
        scratch_shapes=[pltpu.VMEM((ATTN_ROWS, D_MODEL), BF16)],
        compiler_params=_params("parallel"),
        name="sample_attn",
    )(sinks, d_rows, proj, proj, cache_k, cache_v, x, pool_w, pool_scale, w_out)


def _group_major(a, axis):
    shape = a.shape
    a = a.reshape(shape[:axis] + (N_KV_HEADS, GROUP, HEAD_DIM) + shape[axis + 1:])
    a = jnp.swapaxes(a, axis, axis + 1)
    return a.reshape(shape)


def kernel(x_prompt, x_sample, cache_k, cache_v, state_pool, norm_ffn1, ffn1_gate,
           ffn1_up, ffn1_down, norm_mix, w_in, pool_w, pool_scale, attn_sinks,
           w_out, norm_ffn2, ffn2_gate, ffn2_up, ffn2_down, final_norm):
    xp = x_prompt.reshape(PROMPT_ROWS, D_MODEL)
    xs = x_sample.reshape(SAMPLE_ROWS, D_MODEL)
    gf = final_norm.reshape(1, D_MODEL)
    state_t = jnp.transpose(state_pool, (0, 2, 1, 3))
    keep = min(WINDOW, SEQ)
    assert keep >= POOL_HIST
    k0 = POOL_DIM + Q_DIM

    ffn1_w = (ffn1_gate, ffn1_up, ffn1_down)
    ffn2_w = (ffn2_gate, ffn2_up, ffn2_down)
    w_bf = _cast_ffn(0, *ffn1_w)
    kp_l, vp_l, pp_l, ks_l, vs_l, ps_l = [], [], [], [], [], []
    for l in range(DEPTH):
        last = l == DEPTH - 1
        wq = _group_major(w_in[l][:, POOL_DIM:k0], 1) * ATTN_SCALE
        w_in_l = jnp.concatenate(
            [w_in[l][:, :POOL_DIM], wq, w_in[l][:, k0:]], axis=1).astype(BF16)
        w_out_l = jnp.concatenate(
            [w_out[l][:POOL_DIM], _group_major(w_out[l][POOL_DIM:], 0)], axis=0).astype(BF16)
        pool_w_l = pool_w[l].astype(BF16)
        pool_scale_l = pool_scale[l].reshape(1, POOL_DIM)
        sinks_l = attn_sinks[l].astype(F32)

        xp, xs, w_bf = _ffn(xp, xs, norm_ffn1[l].reshape(1, D_MODEL), *w_bf, gf,
                            next_w=(l,) + ffn2_w)
        proj_p, proj_s = _inproj(xp, xs, norm_mix[l].reshape(1, D_MODEL), w_in_l)

        tails = [proj_p[(b + 1) * SEQ - keep:(b + 1) * SEQ] for b in range(BATCH)]
        kp_l.append(jnp.stack([t[:, k0:k0 + KV_DIM] for t in tails])
                    .reshape(BATCH, keep, N_KV_HEADS, HEAD_DIM))
        vp_l.append(jnp.stack([t[:, k0 + KV_DIM:] for t in tails])
                    .reshape(BATCH, keep, N_KV_HEADS, HEAD_DIM))
        pp_l.append(jnp.stack([t[keep - POOL_HIST:, :POOL_DIM] for t in tails]))

        k_new = proj_s[:, k0:k0 + KV_DIM].reshape(DEC_BATCH, DEC_SEQ, N_KV_HEADS, HEAD_DIM)
        v_new = proj_s[:, k0 + KV_DIM:].reshape(DEC_BATCH, DEC_SEQ, N_KV_HEADS, HEAD_DIM)
        ks_l.append(jnp.concatenate([cache_k[l], k_new], axis=1)[:, DEC_SEQ:])
        vs_l.append(jnp.concatenate([cache_v[l], v_new], axis=1)[:, DEC_SEQ:])

        d_flat, new_pool = _sample_pool(
            l, state_t, proj_s.reshape(DEC_BATCH, DEC_SEQ * IN_DIM))
        ps_l.append(new_pool)

        xp = _prompt_mix(sinks_l, proj_p, xp, pool_w_l, pool_scale_l, w_out_l)
        xs = _sample_attn(sinks_l, d_flat.reshape(SAMPLE_ROWS, POOL_DIM), proj_s,
                          cache_k[l].reshape(DEC_BATCH, N_BUF, KV_DIM),
                          cache_v[l].reshape(DEC_BATCH, N_BUF, KV_DIM),
                          xs, pool_w_l, pool_scale_l, w_out_l)

        xp, xs, w_bf = _ffn(xp, xs, norm_ffn2[l].reshape(1, D_MODEL), *w_bf, gf,
                            next_w=None if last else (l + 1,) + ffn1_w,
                            final_norm=last)

    y_prompt = xp.reshape(BATCH, SEQ, D_MODEL)
    y_sample = xs.reshape(DEC_BATCH, DEC_SEQ, D_MODEL)
    new_pool_sample = jnp.transpose(jnp.stack(ps_l), (0, 2, 1, 3))
    return (y_prompt, y_sample, jnp.stack(kp_l), jnp.stack(vp_l), jnp.stack(pp_l),
            jnp.stack(ks_l), jnp.stack(vs_l), new_pool_sample)
```

```python
import functools

import jax
import jax.numpy as jnp
from jax import lax
from jax.experimental import pallas as pl
from jax.experimental.pallas import tpu as pltpu

F32 = jnp.float32
BF16 = jnp.bfloat16

D_MODEL = 2048
BATCH = 2
SEQ = 4096
DEPTH = 2
DEC_BATCH = 128
DEC_SEQ = 4
PAST_LEN = 8192

POOL_DIM = 1024
POOL_WINDOWS = (2, 4, 8, 16)
POOL_GROUP_DIM = POOL_DIM // len(POOL_WINDOWS)
POOL_HIST = max(POOL_WINDOWS) - 1
HEAD_DIM = 64
N_HEADS = 16
N_KV_HEADS = 4
GROUP = N_HEADS // N_KV_HEADS
Q_DIM = N_HEADS * HEAD_DIM
KV_DIM = N_KV_HEADS * HEAD_DIM
IN_DIM = POOL_DIM + Q_DIM + 2 * KV_DIM
WINDOW = 128
FFN_DIM = 5632
RMS_EPS = 1e-5
ATTN_SCALE = HEAD_DIM ** -0.5
N_BUF = min(WINDOW, PAST_LEN)
N_HIST_SAMPLE = min(POOL_HIST, PAST_LEN)

PROMPT_ROWS = BATCH * SEQ
SAMPLE_ROWS = DEC_BATCH * DEC_SEQ
M_ROWS = PROMPT_ROWS + SAMPLE_ROWS

VMEM_LIMIT_BYTES = 56 * 1024 * 1024

FFN_TM = 1088
FFN_TF = 512
FFN_TILES = M_ROWS // FFN_TM
FFN_CHUNKS = FFN_DIM // FFN_TF
FFN_FULL_TILES = PROMPT_ROWS // FFN_TM
FFN_SPLIT = PROMPT_ROWS - FFN_FULL_TILES * FFN_TM
CAST_ROWS = D_MODEL // FFN_TILES
PROJ_TM = 512
PROJ_PROMPT_TILES = PROMPT_ROWS // PROJ_TM
MIX_TQ = 512
HALO_U = 16
POOL_SB = 32
ATTN_SB = 32
ATTN_ROWS = ATTN_SB * DEC_SEQ

assert FFN_TILES * FFN_TM == M_ROWS and FFN_CHUNKS * FFN_TF == FFN_DIM
assert FFN_FULL_TILES + 1 == FFN_TILES and FFN_SPLIT + SAMPLE_ROWS == FFN_TM
assert FFN_SPLIT % 8 == 0 and CAST_ROWS * FFN_TILES == D_MODEL
assert FFN_CHUNKS >= 2
assert SAMPLE_ROWS == PROJ_TM and PROJ_PROMPT_TILES * PROJ_TM == PROMPT_ROWS


def _rms(x, g):
    return x * lax.rsqrt(jnp.mean(x * x, axis=-1, keepdims=True) + RMS_EPS) * g


def _params(*sem):
    return pltpu.CompilerParams(dimension_semantics=sem,
                                vmem_limit_bytes=VMEM_LIMIT_BYTES)


def _ffn_body(*refs, final_norm, cast_next):
    xp_hbm, xs_hbm, g_ref, wg_ref, wu_ref, wd_ref, gf_ref = refs[:7]
    refs = refs[7:]
    if cast_next:
        ng_ref, nu_ref, nd_ref = refs[:3]
        refs = refs[3:]
    op_hbm, os_hbm = refs[:2]
    refs = refs[2:]
    if cast_next:
        cg_ref, cu_ref, cd_ref = refs[:3]
        refs = refs[3:]
    acc_ref, xn_ref, in_sem, out_sem = refs

    i = pl.program_id(0)
    f = pl.program_id(1)
    slot = i % 2

    def copies(p_hbm, s_hbm, tile, s, sem, mixed, to_vmem):
        if mixed:
            pairs = [(p_hbm.at[pl.ds(FFN_FULL_TILES * FFN_TM, FFN_SPLIT)],
                      acc_ref.at[s, pl.ds(0, FFN_SPLIT)]),
                     (s_hbm, acc_ref.at[s, pl.ds(FFN_SPLIT, SAMPLE_ROWS)])]
        else:
            pairs = [(p_hbm.at[pl.ds(tile * FFN_TM, FFN_TM)], acc_ref.at[s])]
        return [pltpu.make_async_copy(h, v, sem.at[s]) if to_vmem
                else pltpu.make_async_copy(v, h, sem.at[s]) for h, v in pairs]

    def on_tile(tile, s, op, to_vmem):
        p_hbm, s_hbm, sem = ((xp_hbm, xs_hbm, in_sem) if to_vmem
                             else (op_hbm, os_hbm, out_sem))

        @pl.when(tile < FFN_FULL_TILES)
        def _():
            for c in copies(p_hbm, s_hbm, tile, s, sem, False, to_vmem):
                op(c)

        @pl.when(tile == FFN_FULL_TILES)
        def _():
            for c in copies(p_hbm, s_hbm, tile, s, sem, True, to_vmem):
                op(c)

    start = lambda c: c.start()
    wait = lambda c: c.wait()

    @pl.when((i == 0) & (f == 0))
    def _():
        on_tile(i, slot, start, True)

    @pl.when(f == 0)
    def _():
        on_tile(i, slot, wait, True)
        xn_ref[...] = _rms(acc_ref[slot], g_ref[...]).astype(BF16)

    a = xn_ref[...]
    gate = jnp.dot(a, wg_ref[...], preferred_element_type=F32)
    up = jnp.dot(a, wu_ref[...], preferred_element_type=F32)
    h = (gate * jax.nn.sigmoid(gate) * up).astype(BF16)
    acc_ref[slot] += 0.5 * jnp.dot(h, wd_ref[...], preferred_element_type=F32)

    if cast_next:
        cg_ref[...] = ng_ref[...].astype(BF16)
        cu_ref[...] = nu_ref[...].astype(BF16)
        cd_ref[...] = nd_ref[...].astype(BF16)

    @pl.when(f == 1)
    def _():
        @pl.when(i >= 1)
        def _():
            on_tile(i - 1, 1 - slot, wait, False)

        @pl.when(i + 1 < FFN_TILES)
        def _():
            on_tile(i + 1, 1 - slot, start, True)

    @pl.when(f == FFN_CHUNKS - 1)
    def _():
        if final_norm:
            acc_ref[slot] = _rms(acc_ref[slot], gf_ref[...])
        on_tile(i, slot, start, False)

        @pl.when(i == FFN_TILES - 1)
        def _():
            on_tile(i, slot, wait, False)


def _ffn(xp, xs, g, wg, wu, wd, gf, next_w=None, *, final_norm=False):
    cast_next = next_w is not None
    vec = pl.BlockSpec((1, D_MODEL), lambda i, f: (0, 0))
    in_specs = [
        pl.BlockSpec(memory_space=pl.ANY),
        pl.BlockSpec(memory_space=pl.ANY),
        vec,
        pl.BlockSpec((D_MODEL, FFN_TF), lambda i, f: (0, f)),
        pl.BlockSpec((D_MODEL, FFN_TF), lambda i, f: (0, f)),
        pl.BlockSpec((FFN_TF, D_MODEL), lambda i, f: (f, 0)),
        vec,
    ]
    out_shape = [jax.ShapeDtypeStruct((PROMPT_ROWS, D_MODEL), F32),
                 jax.ShapeDtypeStruct((SAMPLE_ROWS, D_MODEL), F32)]
    out_specs = [pl.BlockSpec(memory_space=pl.ANY), pl.BlockSpec(memory_space=pl.ANY)]
    args = [xp, xs, g, wg, wu, wd, gf]
    if cast_next:
        nl = next_w[0]
        in_specs += [
            pl.BlockSpec((None, CAST_ROWS, FFN_TF), lambda i, f: (nl, i, f)),
            pl.BlockSpec((None, CAST_ROWS, FFN_TF), lambda i, f: (nl, i, f)),
            pl.BlockSpec((None, FFN_TF, CAST_ROWS), lambda i, f: (nl, f, i)),
        ]
        out_shape += [jax.ShapeDtypeStruct((D_MODEL, FFN_DIM), BF16),
                      jax.ShapeDtypeStruct((D_MODEL, FFN_DIM), BF16),
                      jax.ShapeDtypeStruct((FFN_DIM, D_MODEL), BF16)]
        out_specs += [
            pl.BlockSpec((CAST_ROWS, FFN_TF), lambda i, f: (i, f)),
            pl.BlockSpec((CAST_ROWS, FFN_TF), lambda i, f: (i, f)),
            pl.BlockSpec((FFN_TF, CAST_ROWS), lambda i, f: (f, i)),
        ]
        args += list(next_w[1:])
    outs = pl.pallas_call(
        functools.partial(_ffn_body, final_norm=final_norm, cast_next=cast_next),
        out_shape=out_shape,
        grid=(FFN_TILES, FFN_CHUNKS),
        in_specs=in_specs,
        out_specs=out_specs,
        scratch_shapes=[pltpu.VMEM((2, FFN_TM, D_MODEL), F32),
                        pltpu.VMEM((FFN_TM, D_MODEL), BF16),
                        pltpu.SemaphoreType.DMA((2,)),
                        pltpu.SemaphoreType.DMA((2,))],
        compiler_params=_params("arbitrary", "arbitrary"),
        name="ffn_final" if final_norm else "ffn",
    )(*args)
    return outs[0], outs[1], tuple(outs[2:])


def _cast_ffn_body(g_ref, u_ref, d_ref, og_ref, ou_ref, od_ref):
    og_ref[...] = g_ref[...].astype(BF16)
    ou_ref[...] = u_ref[...].astype(BF16)
    od_ref[...] = d_ref[...].astype(BF16)


def _cast_ffn(layer, gate, up, down):
    return pl.pallas_call(
        _cast_ffn_body,
        out_shape=(jax.ShapeDtypeStruct((D_MODEL, FFN_DIM), BF16),
                   jax.ShapeDtypeStruct((D_MODEL, FFN_DIM), BF16),
                   jax.ShapeDtypeStruct((FFN_DIM, D_MODEL), BF16)),
        grid=(FFN_CHUNKS,),
        in_specs=[
            pl.BlockSpec((None, D_MODEL, FFN_TF), lambda j: (layer, 0, j)),
            pl.BlockSpec((None, D_MODEL, FFN_TF), lambda j: (layer, 0, j)),
            pl.BlockSpec((None, FFN_TF, D_MODEL), lambda j: (layer, j, 0)),
        ],
        out_specs=(pl.BlockSpec((D_MODEL, FFN_TF), lambda j: (0, j)),
                   pl.BlockSpec((D_MODEL, FFN_TF), lambda j: (0, j)),
                   pl.BlockSpec((FFN_TF, D_MODEL), lambda j: (j, 0))),
        compiler_params=_params("parallel"),
        name="cast_ffn",
    )(gate, up, down)


def _inproj_body(xp_ref, xs_ref, g_ref, w_ref, wkvt_ref, pp_ref, ps_ref, kvt_ref):
    i = pl.program_id(0)

    @pl.when(i < PROJ_PROMPT_TILES)
    def _():
        xn = _rms(xp_ref[...], g_ref[...]).astype(BF16)
        pp_ref[...] = jnp.dot(xn, w_ref[...], preferred_element_type=F32)

    @pl.when(i == PROJ_PROMPT_TILES)
    def _():
        xn = _rms(xs_ref[...], g_ref[...]).astype(BF16)
        ps_ref[...] = jnp.dot(xn, w_ref[...], preferred_element_type=F32)
        kvt_ref[...] = lax.dot_general(wkvt_ref[...], xn, (((1,), (1,)), ((), ())),
                                       preferred_element_type=F32)


def _inproj(xp, xs, g, w, w_kv_t):
    prompt_tile = lambda i: (jnp.minimum(i, PROJ_PROMPT_TILES - 1), 0)
    fixed = lambda i: (0, 0)
    return pl.pallas_call(
        _inproj_body,
        out_shape=(jax.ShapeDtypeStruct((PROMPT_ROWS, IN_DIM), F32),
                   jax.ShapeDtypeStruct((SAMPLE_ROWS, IN_DIM), F32),
                   jax.ShapeDtypeStruct((2 * KV_DIM, SAMPLE_ROWS), F32)),
        grid=(PROJ_PROMPT_TILES + 1,),
        in_specs=[
            pl.BlockSpec((PROJ_TM, D_MODEL), prompt_tile),
            pl.BlockSpec((PROJ_TM, D_MODEL), fixed),
            pl.BlockSpec((1, D_MODEL), fixed),
            pl.BlockSpec((D_MODEL, IN_DIM), fixed),
            pl.BlockSpec((2 * KV_DIM, D_MODEL), fixed),
        ],
        out_specs=(pl.BlockSpec((PROJ_TM, IN_DIM), prompt_tile),
                   pl.BlockSpec((PROJ_TM, IN_DIM), fixed),
                   pl.BlockSpec((2 * KV_DIM, SAMPLE_ROWS), fixed)),
        compiler_params=_params("arbitrary"),
        name="inproj",
    )(xp, xs, g, w, w_kv_t)


def _kv_head_of_lane(shape):
    return lax.broadcasted_iota(jnp.int32, shape, 1) // HEAD_DIM


def _pool_linear(d, gi, pw_ref, ps_ref):
    lanes = slice(gi * POOL_GROUP_DIM, (gi + 1) * POOL_GROUP_DIM)
    y = jnp.dot(d.astype(BF16), pw_ref[gi], preferred_element_type=F32)
    return y * ps_ref[:, lanes]


def _prompt_mix_body(sink_ref, u_ref, q_ref, kv_ref, kvh_ref, uh_ref, x_ref,
                     pw_ref, ps_ref, wo_ref, o_ref, mix_ref):
    tiles_per_seq = SEQ // MIX_TQ
    tile = pl.program_id(0) % tiles_per_seq
    first = tile == 0

    lo = jnp.where(first, WINDOW, 0)
    keep_u = jnp.full((HALO_U, POOL_DIM), lo, jnp.int32) == 0
    keep_kv = jnp.full((WINDOW, 2 * KV_DIM), lo, jnp.int32) == 0
    uh = jnp.where(keep_u, uh_ref[...], 0.0)
    pos = tile * MIX_TQ + lax.broadcasted_iota(jnp.int32, (MIX_TQ, 1), 0)
    for gi, w in enumerate(POOL_WINDOWS):
        lanes = slice(gi * POOL_GROUP_DIM, (gi + 1) * POOL_GROUP_DIM)
        xg = jnp.concatenate([uh[:, lanes], u_ref[:, lanes]], axis=0)
        s = xg
        sh = 1
        while sh < w:
            s = s + pltpu.roll(s, sh, axis=0)
            sh *= 2
        cnt = jnp.minimum(w, pos + 1).astype(F32)
        d = s[HALO_U:] / cnt - xg[HALO_U:]
        mix_ref[:, lanes] = _pool_linear(d, gi, pw_ref, ps_ref).astype(BF16)

    kvh = jnp.where(keep_kv, kvh_ref[...], 0.0)
    k_all = jnp.concatenate([kvh[:, :KV_DIM], kv_ref[:, :KV_DIM]], axis=0)
    v_all = jnp.concatenate([kvh[:, KV_DIM:], kv_ref[:, KV_DIM:]], axis=0)
    head = _kv_head_of_lane((2 * WINDOW, KV_DIM))
    rows = GROUP * WINDOW
    qi = lax.broadcasted_iota(jnp.int32, (rows, 2 * WINDOW), 0) % WINDOW
    kj = lax.broadcasted_iota(jnp.int32, (rows, 2 * WINDOW), 1)
    band = (kj > qi) & (kj <= qi + WINDOW)
    for n in range(MIX_TQ // WINDOW):
        kw = k_all[n * WINDOW:(n + 2) * WINDOW]
        vw = v_all[n * WINDOW:(n + 2) * WINDOW]
        qn = q_ref[n * WINDOW:(n + 1) * WINDOW, :].astype(BF16)
        qs = jnp.concatenate(
            [qn[:, g * KV_DIM:(g + 1) * KV_DIM] for g in range(GROUP)], axis=0)
        valid = band
        if n == 0:
            valid = band & (kj >= lo)
        probs = []
        for k in range(N_KV_HEADS):
            kk = jnp.where(head == k, kw, 0.0).astype(BF16)
            s = lax.dot_general(qs, kk, (((1,), (1,)), ((), ())),
                                preferred_element_type=F32)
            s = jnp.where(valid, s, -jnp.inf)
            sink = jnp.concatenate(
                [jnp.full((WINDOW, 1), sink_ref[k * GROUP + g], F32)
                 for g in range(GROUP)], axis=0)
            m = jnp.maximum(jnp.max(s, axis=-1, keepdims=True), sink)
            p = jnp.exp(s - m)
            den = jnp.sum(p, axis=-1, keepdims=True) + jnp.exp(sink - m)
            probs.append((p / den).astype(BF16))
        pcat = jnp.concatenate(probs, axis=1)
        vcat = jnp.concatenate(
            [jnp.where(head == k, vw, 0.0).astype(BF16)
             for k in range(N_KV_HEADS)], axis=0)
        o = jnp.dot(pcat, vcat, preferred_element_type=F32)
        blk = slice(n * WINDOW, (n + 1) * WINDOW)
        for g in range(GROUP):
            mix_ref[blk, POOL_DIM + g * KV_DIM:POOL_DIM + (g + 1) * KV_DIM] = (
                o[g * WINDOW:(g + 1) * WINDOW].astype(BF16))
        o_ref[blk, :] = x_ref[blk, :] + jnp.dot(mix_ref[blk, :], wo_ref[...],
                                                preferred_element_type=F32)


def _prompt_mix(sinks, proj, x, pool_w, pool_scale, w_out):
    blocks_kv = MIX_TQ // WINDOW
    blocks_u = MIX_TQ // HALO_U
    return pl.pallas_call(
        _prompt_mix_body,
        out_shape=jax.ShapeDtypeStruct((PROMPT_ROWS, D_MODEL), F32),
        grid=(PROMPT_ROWS // MIX_TQ,),
        in_specs=[
            pl.BlockSpec(memory_space=pltpu.SMEM),
            pl.BlockSpec((MIX_TQ, POOL_DIM), lambda i: (i, 0)),
            pl.BlockSpec((MIX_TQ, Q_DIM), lambda i: (i, 1)),
            pl.BlockSpec((MIX_TQ, 2 * KV_DIM), lambda i: (i, 4)),
            pl.BlockSpec((WINDOW, 2 * KV_DIM),
                         lambda i: (jnp.maximum(i * blocks_kv - 1, 0), 4)),
            pl.BlockSpec((HALO_U, POOL_DIM),
                         lambda i: (jnp.maximum(i * blocks_u - 1, 0), 0)),
            pl.BlockSpec((MIX_TQ, D_MODEL), lambda i: (i, 0)),
            pl.BlockSpec((len(POOL_WINDOWS), POOL_GROUP_DIM, POOL_GROUP_DIM),
                         lambda i: (0, 0, 0)),
            pl.BlockSpec((1, POOL_DIM), lambda i: (0, 0)),
            pl.BlockSpec((D_MODEL, D_MODEL), lambda i: (0, 0)),
        ],
        out_specs=pl.BlockSpec((MIX_TQ, D_MODEL), lambda i: (i, 0)),
        scratch_shapes=[pltpu.VMEM((MIX_TQ, D_MODEL), BF16)],
        compiler_params=_params("parallel"),
        name="prompt_mix",
    )(sinks, proj, proj, proj, proj, proj, x, pool_w, pool_scale, w_out)


def _sample_pool_body(hist_ref, proj_ref, d_ref, newpool_ref):
    for gi, w in enumerate(POOL_WINDOWS):
        base = gi * POOL_GROUP_DIM
        seq = [hist_ref[j, :, base:base + POOL_GROUP_DIM] for j in range(POOL_HIST)]
        seq += [proj_ref[:, t * IN_DIM + base:t * IN_DIM + base + POOL_GROUP_DIM]
                for t in range(DEC_SEQ)]
        for t in range(DEC_SEQ):
            last = POOL_HIST + t
            s = seq[last - w + 1]
            for j in range(last - w + 2, last + 1):
                s = s + seq[j]
            cnt = float(min(w, N_HIST_SAMPLE + t + 1))
            d_ref[:, t * POOL_DIM + base:t * POOL_DIM + base + POOL_GROUP_DIM] = (
                s / cnt - seq[last])
    for j in range(POOL_HIST - DEC_SEQ):
        newpool_ref[j] = hist_ref[j + DEC_SEQ]
    for t in range(DEC_SEQ):
        newpool_ref[POOL_HIST - DEC_SEQ + t] = proj_ref[:, t * IN_DIM:t * IN_DIM + POOL_DIM]


def _sample_pool(layer, state_t, proj_flat):
    return pl.pallas_call(
        _sample_pool_body,
        out_shape=(jax.ShapeDtypeStruct((DEC_BATCH, DEC_SEQ * POOL_DIM), F32),
                   jax.ShapeDtypeStruct((POOL_HIST, DEC_BATCH, POOL_DIM), F32)),
        grid=(DEC_BATCH // POOL_SB,),
        in_specs=[
            pl.BlockSpec((None, POOL_HIST, POOL_SB, POOL_DIM), lambda i: (layer, 0, i, 0)),
            pl.BlockSpec((POOL_SB, DEC_SEQ * IN_DIM), lambda i: (i, 0)),
        ],
        out_specs=(pl.BlockSpec((POOL_SB, DEC_SEQ * POOL_DIM), lambda i: (i, 0)),
                   pl.BlockSpec((POOL_HIST, POOL_SB, POOL_DIM), lambda i: (0, i, 0))),
        compiler_params=_params("parallel"),
        name="sample_pool",
    )(state_t, proj_flat)


def _sample_attn_body(sink_ref, d_ref, q_ref, kvt_ref, ck_ref, cv_ref, x_ref,
                      pw_ref, ps_ref, wo_ref, o_ref, nk_ref, nv_ref, mix_ref):
    for gi in range(len(POOL_WINDOWS)):
        lanes = slice(gi * POOL_GROUP_DIM, (gi + 1) * POOL_GROUP_DIM)
        mix_ref[:, lanes] = _pool_linear(d_ref[:, lanes], gi, pw_ref, ps_ref).astype(BF16)

    pair = 2 * DEC_SEQ
    grp = GROUP * pair
    rows = N_KV_HEADS * grp
    new_cols = ATTN_SB * DEC_SEQ
    keys = N_BUF + new_cols
    assert keys == KV_DIM and new_cols == N_BUF
    head = _kv_head_of_lane((rows, KV_DIM))
    row = lax.broadcasted_iota(jnp.int32, (rows, keys), 0)
    row_head = row // grp
    in_pair = row % pair
    low = in_pair < DEC_SEQ
    tok = in_pair % DEC_SEQ
    col = lax.broadcasted_iota(jnp.int32, (rows, keys), 1)
    new_col = col - N_BUF
    valid_cache = (col < N_BUF) & (col > tok)
    causal_new = (new_col >= 0) & ((new_col % DEC_SEQ) <= tok)
    seq_of_col = new_col // DEC_SEQ - in_pair // DEC_SEQ
    sink = jnp.concatenate(
        [jnp.full((pair, 1), sink_ref[k * GROUP + g], F32)
         for k in range(N_KV_HEADS) for g in range(GROUP)], axis=0)
    lane = lax.broadcasted_iota(jnp.int32, (KV_DIM, N_BUF), 1)
    is_old = lane < N_BUF - DEC_SEQ
    kt_new = kvt_ref[:KV_DIM, :]
    vt_new = kvt_ref[KV_DIM:, :]
    kt_new_bf = kt_new.astype(BF16)
    vt_new_bf = vt_new.astype(BF16)
    nt = (((1,), (1,)), ((), ()))

    for j in range(ATTN_SB // 2):
        r0 = j * pair
        q8 = q_ref[r0:r0 + pair, :]
        qs = jnp.concatenate(
            [q8[:, g * KV_DIM:(g + 1) * KV_DIM] for g in range(GROUP)], axis=0)
        qrep = jnp.concatenate([qs] * N_KV_HEADS, axis=0)
        lhs = jnp.where(head == row_head, qrep, 0.0).astype(BF16)
        scores, values = [], []
        for b in (2 * j, 2 * j + 1):
            kt = ck_ref[b]
            vt = cv_ref[b]
            scores.append(jnp.dot(lhs, kt.astype(BF16), preferred_element_type=F32))
            values.append(vt.astype(BF16))
            shift_new = (N_BUF - DEC_SEQ - b * DEC_SEQ) % N_BUF
            nk_ref[b] = jnp.where(is_old, pltpu.roll(kt, N_BUF - DEC_SEQ, axis=1),
                                  pltpu.roll(kt_new, shift_new, axis=1))
            nv_ref[b] = jnp.where(is_old, pltpu.roll(vt, N_BUF - DEC_SEQ, axis=1),
                                  pltpu.roll(vt_new, shift_new, axis=1))
        s_new = jnp.dot(lhs, kt_new_bf, preferred_element_type=F32)
        s = jnp.concatenate([jnp.where(low[:, :N_BUF], scores[0], scores[1]), s_new],
                            axis=1)
        valid = valid_cache | (causal_new & (seq_of_col == 2 * j))
        s = jnp.where(valid, s, -jnp.inf)
        m = jnp.maximum(jnp.max(s, axis=-1, keepdims=True), sink)
        p = jnp.exp(s - m)
        den = jnp.sum(p, axis=-1, keepdims=True) + jnp.exp(sink - m)
        p = (p / den).astype(BF16)
        p_old = p[:, :N_BUF]
        o = jnp.where(low,
                      lax.dot_general(p_old, values[0], nt, preferred_element_type=F32),
                      lax.dot_general(p_old, values[1], nt, preferred_element_type=F32))
        o = o + lax.dot_general(p[:, N_BUF:], vt_new_bf, nt, preferred_element_type=F32)
        o = jnp.where(head == row_head, o, 0.0)
        og = o[0:grp]
        for k in range(1, N_KV_HEADS):
            og = og + o[k * grp:(k + 1) * grp]
        for g in range(GROUP):
            mix_ref[r0:r0 + pair,
                    POOL_DIM + g * KV_DIM:POOL_DIM + (g + 1) * KV_DIM] = (
                        og[g * pair:(g + 1) * pair].astype(BF16))

    o_ref[...] = x_ref[...] + jnp.dot(mix_ref[...], wo_ref[...],
                                      preferred_element_type=F32)


def _sample_attn(layer, sinks, d_rows, proj, kvt, cache_kt, cache_vt, x, pool_w,
                 pool_scale, w_out):
    cache_spec = pl.BlockSpec((None, ATTN_SB, KV_DIM, N_BUF), lambda i: (layer, i, 0, 0))
    new_cache = jax.ShapeDtypeStruct((DEC_BATCH, KV_DIM, N_BUF), F32)
    new_cache_spec = pl.BlockSpec((ATTN_SB, KV_DIM, N_BUF), lambda i: (i, 0, 0))
    return pl.pallas_call(
        _sample_attn_body,
        out_shape=(jax.ShapeDtypeStruct((SAMPLE_ROWS, D_MODEL), F32),
                   new_cache, new_cache),
        grid=(DEC_BATCH // ATTN_SB,),
        in_specs=[
            pl.BlockSpec(memory_space=pltpu.SMEM),
            pl.BlockSpec((ATTN_ROWS, POOL_DIM), lambda i: (i, 0)),
            pl.BlockSpec((ATTN_ROWS, Q_DIM), lambda i: (i, 1)),
            pl.BlockSpec((2 * KV_DIM, ATTN_ROWS), lambda i: (0, i)),
            cache_spec,
            cache_spec,
            pl.BlockSpec((ATTN_ROWS, D_MODEL), lambda i: (i, 0)),
            pl.BlockSpec((len(POOL_WINDOWS), POOL_GROUP_DIM, POOL_GROUP_DIM),
                         lambda i: (0, 0, 0)),
            pl.BlockSpec((1, POOL_DIM), lambda i: (0, 0)),
            pl.BlockSpec((D_MODEL, D_MODEL), lambda i: (0, 0),
                         pipeline_mode=pl.Buffered(1)),
        ],
        out_specs=(pl.BlockSpec((ATTN_ROWS, D_MODEL), lambda i: (i, 0)),
                   new_cache_spec, new_cache_spec),
        scratch_shapes=[pltpu.VMEM((ATTN_ROWS, D_MODEL), BF16)],
        compiler_params=_params("parallel"),
        name="sample_attn",
    )(sinks, d_rows, proj, kvt, cache_kt, cache_vt, x, pool_w, pool_scale, w_out)


def _group_major(a, axis):
    shape = a.shape
    a = a.reshape(shape[:axis] + (N_KV_HEADS, GROUP, HEAD_DIM) + shape[axis + 1:])
    a = jnp.swapaxes(a, axis, axis + 1)
    return a.reshape(shape)


def kernel(x_prompt, x_sample, cache_k, cache_v, state_pool, norm_ffn1, ffn1_gate,
           ffn1_up, ffn1_down, norm_mix, w_in, pool_w, pool_scale, attn_sinks,
           w_out, norm_ffn2, ffn2_gate, ffn2_up, ffn2_down, final_norm):
    xp = x_prompt.reshape(PROMPT_ROWS, D_MODEL)
    xs = x_sample.reshape(SAMPLE_ROWS, D_MODEL)
    gf = final_norm.reshape(1, D_MODEL)
    state_t = jnp.transpose(state_pool, (0, 2, 1, 3))

    def to_cache_t(c):
        return jnp.transpose(c, (0, 1, 3, 4, 2)).reshape(DEPTH, DEC_BATCH, KV_DIM, N_BUF)

    def from_cache_t(c):
        c = c.reshape(DEPTH, DEC_BATCH, N_KV_HEADS, HEAD_DIM, N_BUF)
        return jnp.transpose(c, (0, 1, 4, 2, 3))

    cache_kt = to_cache_t(cache_k)
    cache_vt = to_cache_t(cache_v)
    keep = min(WINDOW, SEQ)
    assert keep >= POOL_HIST
    k0 = POOL_DIM + Q_DIM

    ffn1_w = (ffn1_gate, ffn1_up, ffn1_down)
    ffn2_w = (ffn2_gate, ffn2_up, ffn2_down)
    w_bf = _cast_ffn(0, *ffn1_w)
    kp_l, vp_l, pp_l, ks_l, vs_l, ps_l = [], [], [], [], [], []
    for l in range(DEPTH):
        last = l == DEPTH - 1
        wq = _group_major(w_in[l][:, POOL_DIM:k0], 1) * ATTN_SCALE
        w_in_l = jnp.concatenate(
            [w_in[l][:, :POOL_DIM], wq, w_in[l][:, k0:]], axis=1).astype(BF16)
        w_out_l = jnp.concatenate(
            [w_out[l][:POOL_DIM], _group_major(w_out[l][POOL_DIM:], 0)], axis=0).astype(BF16)
        pool_w_l = pool_w[l].astype(BF16)
        pool_scale_l = pool_scale[l].reshape(1, POOL_DIM)
        sinks_l = attn_sinks[l].astype(F32)

        xp, xs, w_bf = _ffn(xp, xs, norm_ffn1[l].reshape(1, D_MODEL), *w_bf, gf,
                            next_w=(l,) + ffn2_w)
        proj_p, proj_s, kvt = _inproj(xp, xs, norm_mix[l].reshape(1, D_MODEL), w_in_l,
                                      w_in[l][:, k0:].T.astype(BF16))

        tails = [proj_p[(b + 1) * SEQ - keep:(b + 1) * SEQ] for b in range(BATCH)]
        kp_l.append(jnp.stack([t[:, k0:k0 + KV_DIM] for t in tails])
                    .reshape(BATCH, keep, N_KV_HEADS, HEAD_DIM))
        vp_l.append(jnp.stack([t[:, k0 + KV_DIM:] for t in tails])
                    .reshape(BATCH, keep, N_KV_HEADS, HEAD_DIM))
        pp_l.append(jnp.stack([t[keep - POOL_HIST:, :POOL_DIM] for t in tails]))

        d_flat, new_pool = _sample_pool(
            l, state_t, proj_s.reshape(DEC_BATCH, DEC_SEQ * IN_DIM))
        ps_l.append(new_pool)

        xp = _prompt_mix(sinks_l, proj_p, xp, pool_w_l, pool_scale_l, w_out_l)
        xs, new_kt, new_vt = _sample_attn(
            l, sinks_l, d_flat.reshape(SAMPLE_ROWS, POOL_DIM), proj_s, kvt,
            cache_kt, cache_vt, xs, pool_w_l, pool_scale_l, w_out_l)
        ks_l.append(new_kt)
        vs_l.append(new_vt)

        xp, xs, w_bf = _ffn(xp, xs, norm_ffn2[l].reshape(1, D_MODEL), *w_bf, gf,
                            next_w=None if last else (l + 1,) + ffn1_w,
                            final_norm=last)

    y_prompt = xp.reshape(BATCH, SEQ, D_MODEL)
    y_sample = xs.reshape(DEC_BATCH, DEC_SEQ, D_MODEL)
    new_pool_sample = jnp.transpose(jnp.stack(ps_l), (0, 2, 1, 3))
    return (y_prompt, y_sample, jnp.stack(kp_l), jnp.stack(vp_l), jnp.stack(pp_l),
            from_cache_t(jnp.stack(ks_l)), from_cache_t(jnp.stack(vs_l)),
            new_pool_sample)
```

```python
import functools

import jax
import jax.numpy as jnp
from jax import lax
from jax.experimental import pallas as pl
from jax.experimental.pallas import tpu as pltpu

F32 = jnp.float32
BF16 = jnp.bfloat16

D_MODEL = 2048
BATCH = 2
SEQ = 4096
DEPTH = 2
DEC_BATCH = 128
DEC_SEQ = 4
PAST_LEN = 8192

POOL_DIM = 1024
POOL_WINDOWS = (2, 4, 8, 16)
POOL_GROUP_DIM = POOL_DIM // len(POOL_WINDOWS)
POOL_HIST = max(POOL_WINDOWS) - 1
HEAD_DIM = 64
N_HEADS = 16
N_KV_HEADS = 4
GROUP = N_HEADS // N_KV_HEADS
Q_DIM = N_HEADS * HEAD_DIM
KV_DIM = N_KV_HEADS * HEAD_DIM
IN_DIM = POOL_DIM + Q_DIM + 2 * KV_DIM
WINDOW = 128
FFN_DIM = 5632
RMS_EPS = 1e-5
ATTN_SCALE = HEAD_DIM ** -0.5
N_BUF = min(WINDOW, PAST_LEN)
N_HIST_SAMPLE = min(POOL_HIST, PAST_LEN)

PROMPT_ROWS = BATCH * SEQ
SAMPLE_ROWS = DEC_BATCH * DEC_SEQ
M_ROWS = PROMPT_ROWS + SAMPLE_ROWS

VMEM_LIMIT_BYTES = 56 * 1024 * 1024

FFN_TM = 1088
FFN_TF = 512
FFN_TILES = M_ROWS // FFN_TM
FFN_CHUNKS = FFN_DIM // FFN_TF
FFN_FULL_TILES = PROMPT_ROWS // FFN_TM
FFN_SPLIT = PROMPT_ROWS - FFN_FULL_TILES * FFN_TM
CAST_ROWS = D_MODEL // FFN_TILES
PROJ_TM = 512
PROJ_PROMPT_TILES = PROMPT_ROWS // PROJ_TM
MIX_TQ = 512
HALO_U = 16
OUT_BLOCKS = 2
ATT_CHUNK = 64
POOL_SB = 32
ATTN_SB = 32
ATTN_ROWS = ATTN_SB * DEC_SEQ

assert FFN_TILES * FFN_TM == M_ROWS and FFN_CHUNKS * FFN_TF == FFN_DIM
assert FFN_FULL_TILES + 1 == FFN_TILES and FFN_SPLIT + SAMPLE_ROWS == FFN_TM
assert FFN_SPLIT % 8 == 0 and CAST_ROWS * FFN_TILES == D_MODEL
assert FFN_CHUNKS >= 2
assert SAMPLE_ROWS == PROJ_TM and PROJ_PROMPT_TILES * PROJ_TM == PROMPT_ROWS


def _rms(x, g):
    return x * lax.rsqrt(jnp.mean(x * x, axis=-1, keepdims=True) + RMS_EPS) * g


def _params(*sem):
    return pltpu.CompilerParams(dimension_semantics=sem,
                                vmem_limit_bytes=VMEM_LIMIT_BYTES)


def _ffn_body(*refs, final_norm, cast_next):
    xp_hbm, xs_hbm, g_ref, wg_ref, wu_ref, wd_ref, gf_ref = refs[:7]
    refs = refs[7:]
    if cast_next:
        ng_ref, nu_ref, nd_ref = refs[:3]
        refs = refs[3:]
    op_hbm, os_hbm = refs[:2]
    refs = refs[2:]
    if cast_next:
        cg_ref, cu_ref, cd_ref = refs[:3]
        refs = refs[3:]
    acc_ref, xn_ref, in_sem, out_sem = refs

    i = pl.program_id(0)
    f = pl.program_id(1)
    slot = i % 2

    def copies(p_hbm, s_hbm, tile, s, sem, mixed, to_vmem):
        if mixed:
            pairs = [(p_hbm.at[pl.ds(FFN_FULL_TILES * FFN_TM, FFN_SPLIT)],
                      acc_ref.at[s, pl.ds(0, FFN_SPLIT)]),
                     (s_hbm, acc_ref.at[s, pl.ds(FFN_SPLIT, SAMPLE_ROWS)])]
        else:
            pairs = [(p_hbm.at[pl.ds(tile * FFN_TM, FFN_TM)], acc_ref.at[s])]
        return [pltpu.make_async_copy(h, v, sem.at[s]) if to_vmem
                else pltpu.make_async_copy(v, h, sem.at[s]) for h, v in pairs]

    def on_tile(tile, s, op, to_vmem):
        p_hbm, s_hbm, sem = ((xp_hbm, xs_hbm, in_sem) if to_vmem
                             else (op_hbm, os_hbm, out_sem))

        @pl.when(tile < FFN_FULL_TILES)
        def _():
            for c in copies(p_hbm, s_hbm, tile, s, sem, False, to_vmem):
                op(c)

        @pl.when(tile == FFN_FULL_TILES)
        def _():
            for c in copies(p_hbm, s_hbm, tile, s, sem, True, to_vmem):
                op(c)

    start = lambda c: c.start()
    wait = lambda c: c.wait()

    @pl.when((i == 0) & (f == 0))
    def _():
        on_tile(i, slot, start, True)

    @pl.when(f == 0)
    def _():
        on_tile(i, slot, wait, True)
        xn_ref[...] = _rms(acc_ref[slot], g_ref[...]).astype(BF16)

    a = xn_ref[...]
    gate = jnp.dot(a, wg_ref[...], preferred_element_type=F32)
    up = jnp.dot(a, wu_ref[...], preferred_element_type=F32)
    h = (gate * jax.nn.sigmoid(gate) * up).astype(BF16)
    acc_ref[slot] += 0.5 * jnp.dot(h, wd_ref[...], preferred_element_type=F32)

    if cast_next:
        cg_ref[...] = ng_ref[...].astype(BF16)
        cu_ref[...] = nu_ref[...].astype(BF16)
        cd_ref[...] = nd_ref[...].astype(BF16)

    @pl.when(f == 1)
    def _():
        @pl.when(i >= 1)
        def _():
            on_tile(i - 1, 1 - slot, wait, False)

        @pl.when(i + 1 < FFN_TILES)
        def _():
            on_tile(i + 1, 1 - slot, start, True)

    @pl.when(f == FFN_CHUNKS - 1)
    def _():
        if final_norm:
            acc_ref[slot] = _rms(acc_ref[slot], gf_ref[...])
        on_tile(i, slot, start, False)

        @pl.when(i == FFN_TILES - 1)
        def _():
            on_tile(i, slot, wait, False)


def _ffn(xp, xs, g, wg, wu, wd, gf, next_w=None, *, final_norm=False):
    cast_next = next_w is not None
    vec = pl.BlockSpec((1, D_MODEL), lambda i, f: (0, 0))
    in_specs = [
        pl.BlockSpec(memory_space=pl.ANY),
        pl.BlockSpec(memory_space=pl.ANY),
        vec,
        pl.BlockSpec((D_MODEL, FFN_TF), lambda i, f: (0, f)),
        pl.BlockSpec((D_MODEL, FFN_TF), lambda i, f: (0, f)),
        pl.BlockSpec((FFN_TF, D_MODEL), lambda i, f: (f, 0)),
        vec,
    ]
    out_shape = [jax.ShapeDtypeStruct((PROMPT_ROWS, D_MODEL), F32),
                 jax.ShapeDtypeStruct((SAMPLE_ROWS, D_MODEL), F32)]
    out_specs = [pl.BlockSpec(memory_space=pl.ANY), pl.BlockSpec(memory_space=pl.ANY)]
    args = [xp, xs, g, wg, wu, wd, gf]
    if cast_next:
        nl = next_w[0]
        in_specs += [
            pl.BlockSpec((None, CAST_ROWS, FFN_TF), lambda i, f: (nl, i, f)),
            pl.BlockSpec((None, CAST_ROWS, FFN_TF), lambda i, f: (nl, i, f)),
            pl.BlockSpec((None, FFN_TF, CAST_ROWS), lambda i, f: (nl, f, i)),
        ]
        out_shape += [jax.ShapeDtypeStruct((D_MODEL, FFN_DIM), BF16),
                      jax.ShapeDtypeStruct((D_MODEL, FFN_DIM), BF16),
                      jax.ShapeDtypeStruct((FFN_DIM, D_MODEL), BF16)]
        out_specs += [
            pl.BlockSpec((CAST_ROWS, FFN_TF), lambda i, f: (i, f)),
            pl.BlockSpec((CAST_ROWS, FFN_TF), lambda i, f: (i, f)),
            pl.BlockSpec((FFN_TF, CAST_ROWS), lambda i, f: (f, i)),
        ]
        args += list(next_w[1:])
    outs = pl.pallas_call(
        functools.partial(_ffn_body, final_norm=final_norm, cast_next=cast_next),
        out_shape=out_shape,
        grid=(FFN_TILES, FFN_CHUNKS),
        in_specs=in_specs,
        out_specs=out_specs,
        scratch_shapes=[pltpu.VMEM((2, FFN_TM, D_MODEL), F32),
                        pltpu.VMEM((FFN_TM, D_MODEL), BF16),
                        pltpu.SemaphoreType.DMA((2,)),
                        pltpu.SemaphoreType.DMA((2,))],
        compiler_params=_params("arbitrary", "arbitrary"),
        name="ffn_final" if final_norm else "ffn",
    )(*args)
    return outs[0], outs[1], tuple(outs[2:])


def _cast_ffn_body(g_ref, u_ref, d_ref, og_ref, ou_ref, od_ref):
    og_ref[...] = g_ref[...].astype(BF16)
    ou_ref[...] = u_ref[...].astype(BF16)
    od_ref[...] = d_ref[...].astype(BF16)


def _cast_ffn(layer, gate, up, down):
    return pl.pallas_call(
        _cast_ffn_body,
        out_shape=(jax.ShapeDtypeStruct((D_MODEL, FFN_DIM), BF16),
                   jax.ShapeDtypeStruct((D_MODEL, FFN_DIM), BF16),
                   jax.ShapeDtypeStruct((FFN_DIM, D_MODEL), BF16)),
        grid=(FFN_CHUNKS,),
        in_specs=[
            pl.BlockSpec((None, D_MODEL, FFN_TF), lambda j: (layer, 0, j)),
            pl.BlockSpec((None, D_MODEL, FFN_TF), lambda j: (layer, 0, j)),
            pl.BlockSpec((None, FFN_TF, D_MODEL), lambda j: (layer, j, 0)),
        ],
        out_specs=(pl.BlockSpec((D_MODEL, FFN_TF), lambda j: (0, j)),
                   pl.BlockSpec((D_MODEL, FFN_TF), lambda j: (0, j)),
                   pl.BlockSpec((FFN_TF, D_MODEL), lambda j: (j, 0))),
        compiler_params=_params("parallel"),
        name="cast_ffn",
    )(gate, up, down)


def _inproj_body(xp_ref, xs_ref, g_ref, w_ref, wkvt_ref, pp_ref, ps_ref, kvt_ref):
    i = pl.program_id(0)

    @pl.when(i < PROJ_PROMPT_TILES)
    def _():
        xn = _rms(xp_ref[...], g_ref[...]).astype(BF16)
        pp_ref[...] = jnp.dot(xn, w_ref[...], preferred_element_type=F32)

    @pl.when(i == PROJ_PROMPT_TILES)
    def _():
        xn = _rms(xs_ref[...], g_ref[...]).astype(BF16)
        ps_ref[...] = jnp.dot(xn, w_ref[...], preferred_element_type=F32)
        kvt_ref[...] = lax.dot_general(wkvt_ref[...], xn, (((1,), (1,)), ((), ())),
                                       preferred_element_type=F32)


def _inproj(xp, xs, g, w, w_kv_t):
    prompt_tile = lambda i: (jnp.minimum(i, PROJ_PROMPT_TILES - 1), 0)
    fixed = lambda i: (0, 0)
    return pl.pallas_call(
        _inproj_body,
        out_shape=(jax.ShapeDtypeStruct((PROMPT_ROWS, IN_DIM), F32),
                   jax.ShapeDtypeStruct((SAMPLE_ROWS, IN_DIM), F32),
                   jax.ShapeDtypeStruct((2 * KV_DIM, SAMPLE_ROWS), F32)),
        grid=(PROJ_PROMPT_TILES + 1,),
        in_specs=[
            pl.BlockSpec((PROJ_TM, D_MODEL), prompt_tile),
            pl.BlockSpec((PROJ_TM, D_MODEL), fixed),
            pl.BlockSpec((1, D_MODEL), fixed),
            pl.BlockSpec((D_MODEL, IN_DIM), fixed),
            pl.BlockSpec((2 * KV_DIM, D_MODEL), fixed),
        ],
        out_specs=(pl.BlockSpec((PROJ_TM, IN_DIM), prompt_tile),
                   pl.BlockSpec((PROJ_TM, IN_DIM), fixed),
                   pl.BlockSpec((2 * KV_DIM, SAMPLE_ROWS), fixed)),
        compiler_params=_params("arbitrary"),
        name="inproj",
    )(xp, xs, g, w, w_kv_t)


def _kv_head_of_lane(shape):
    return lax.broadcasted_iota(jnp.int32, shape, 1) // HEAD_DIM


def _pool_linear(d, gi, pw_ref, ps_ref):
    lanes = slice(gi * POOL_GROUP_DIM, (gi + 1) * POOL_GROUP_DIM)
    y = jnp.dot(d.astype(BF16), pw_ref[gi], preferred_element_type=F32)
    return y * ps_ref[:, lanes]


def _prompt_mix_body(sink_ref, u_ref, q_ref, kv_ref, kvh_ref, uh_ref, x_ref,
                     pw_ref, ps_ref, wo_ref, o_ref, mix_ref, p_ref):
    tiles_per_seq = SEQ // MIX_TQ
    tile = pl.program_id(0) % tiles_per_seq
    first = tile == 0

    lo = jnp.where(first, WINDOW, 0)
    keep_u = jnp.full((HALO_U, POOL_DIM), lo, jnp.int32) == 0
    keep_kv = jnp.full((WINDOW, 2 * KV_DIM), lo, jnp.int32) == 0
    uh = jnp.where(keep_u, uh_ref[...], 0.0)
    pos = tile * MIX_TQ + lax.broadcasted_iota(jnp.int32, (MIX_TQ, 1), 0)
    for gi, w in enumerate(POOL_WINDOWS):
        lanes = slice(gi * POOL_GROUP_DIM, (gi + 1) * POOL_GROUP_DIM)
        xg = jnp.concatenate([uh[:, lanes], u_ref[:, lanes]], axis=0)
        s = xg
        sh = 1
        while sh < w:
            s = s + pltpu.roll(s, sh, axis=0)
            sh *= 2
        cnt = jnp.minimum(w, pos + 1).astype(F32)
        d = s[HALO_U:] / cnt - xg[HALO_U:]
        mix_ref[:, lanes] = _pool_linear(d, gi, pw_ref, ps_ref).astype(BF16)

    kvh = jnp.where(keep_kv, kvh_ref[...], 0.0)
    k_all = jnp.concatenate([kvh[:, :KV_DIM], kv_ref[:, :KV_DIM]], axis=0)
    v_all = jnp.concatenate([kvh[:, KV_DIM:], kv_ref[:, KV_DIM:]], axis=0)
    head = _kv_head_of_lane((2 * WINDOW, KV_DIM))
    qi = lax.broadcasted_iota(jnp.int32, (ATT_CHUNK, 2 * WINDOW), 0)
    kj = lax.broadcasted_iota(jnp.int32, (ATT_CHUNK, 2 * WINDOW), 1)
    for n in range(MIX_TQ // WINDOW):
        kw = k_all[n * WINDOW:(n + 2) * WINDOW]
        vw = v_all[n * WINDOW:(n + 2) * WINDOW]
        qn = q_ref[n * WINDOW:(n + 1) * WINDOW, :].astype(BF16)
        qs = jnp.concatenate(
            [qn[:, g * KV_DIM:(g + 1) * KV_DIM] for g in range(GROUP)], axis=0)
        biases = []
        for c in range(WINDOW // ATT_CHUNK):
            q_pos = qi + c * ATT_CHUNK
            valid = (kj > q_pos) & (kj <= q_pos + WINDOW)
            if n == 0:
                valid = valid & (kj >= lo)
            biases.append(jnp.where(valid, 0.0, -jnp.inf))
        for k in range(N_KV_HEADS):
            kk = jnp.where(head == k, kw, 0.0).astype(BF16)
            s = lax.dot_general(qs, kk, (((1,), (1,)), ((), ())),
                                preferred_element_type=F32)
            for g in range(GROUP):
                sink = sink_ref[k * GROUP + g]
                for c in range(WINDOW // ATT_CHUNK):
                    r0 = g * WINDOW + c * ATT_CHUNK
                    sc = s[r0:r0 + ATT_CHUNK] + biases[c]
                    m = jnp.maximum(jnp.max(sc, axis=-1, keepdims=True), sink)
                    p = jnp.exp(sc - m)
                    den = jnp.sum(p, axis=-1, keepdims=True) + jnp.exp(sink - m)
                    p_ref[r0:r0 + ATT_CHUNK, k * 2 * WINDOW:(k + 1) * 2 * WINDOW] = (
                        p * (1.0 / den)).astype(BF16)
        vcat = jnp.concatenate(
            [jnp.where(head == k, vw, 0.0).astype(BF16)
             for k in range(N_KV_HEADS)], axis=0)
        o = jnp.dot(p_ref[...], vcat, preferred_element_type=F32)
        blk = slice(n * WINDOW, (n + 1) * WINDOW)
        for g in range(GROUP):
            mix_ref[blk, POOL_DIM + g * KV_DIM:POOL_DIM + (g + 1) * KV_DIM] = (
                o[g * WINDOW:(g + 1) * WINDOW].astype(BF16))
        if (n + 1) % OUT_BLOCKS == 0:
            done = slice((n + 1 - OUT_BLOCKS) * WINDOW, (n + 1) * WINDOW)
            o_ref[done, :] = x_ref[done, :] + jnp.dot(mix_ref[done, :], wo_ref[...],
                                                      preferred_element_type=F32)


def _prompt_mix(sinks, proj, x, pool_w, pool_scale, w_out):
    blocks_kv = MIX_TQ // WINDOW
    blocks_u = MIX_TQ // HALO_U
    return pl.pallas_call(
        _prompt_mix_body,
        out_shape=jax.ShapeDtypeStruct((PROMPT_ROWS, D_MODEL), F32),
        grid=(PROMPT_ROWS // MIX_TQ,),
        in_specs=[
            pl.BlockSpec(memory_space=pltpu.SMEM),
            pl.BlockSpec((MIX_TQ, POOL_DIM), lambda i: (i, 0)),
            pl.BlockSpec((MIX_TQ, Q_DIM), lambda i: (i, 1)),
            pl.BlockSpec((MIX_TQ, 2 * KV_DIM), lambda i: (i, 4)),
            pl.BlockSpec((WINDOW, 2 * KV_DIM),
                         lambda i: (jnp.maximum(i * blocks_kv - 1, 0), 4)),
            pl.BlockSpec((HALO_U, POOL_DIM),
                         lambda i: (jnp.maximum(i * blocks_u - 1, 0), 0)),
            pl.BlockSpec((MIX_TQ, D_MODEL), lambda i: (i, 0)),
            pl.BlockSpec((len(POOL_WINDOWS), POOL_GROUP_DIM, POOL_GROUP_DIM),
                         lambda i: (0, 0, 0)),
            pl.BlockSpec((1, POOL_DIM), lambda i: (0, 0)),
            pl.BlockSpec((D_MODEL, D_MODEL), lambda i: (0, 0)),
        ],
        out_specs=pl.BlockSpec((MIX_TQ, D_MODEL), lambda i: (i, 0)),
        scratch_shapes=[pltpu.VMEM((MIX_TQ, D_MODEL), BF16),
                        pltpu.VMEM((GROUP * WINDOW, N_KV_HEADS * 2 * WINDOW), BF16)],
        compiler_params=_params("parallel"),
        name="prompt_mix",
    )(sinks, proj, proj, proj, proj, proj, x, pool_w, pool_scale, w_out)


def _sample_pool_body(hist_ref, proj_ref, d_ref, newpool_ref):
    for gi, w in enumerate(POOL_WINDOWS):
        base = gi * POOL_GROUP_DIM
        seq = [hist_ref[j, :, base:base + POOL_GROUP_DIM] for j in range(POOL_HIST)]
        seq += [proj_ref[:, t * IN_DIM + base:t * IN_DIM + base + POOL_GROUP_DIM]
                for t in range(DEC_SEQ)]
        for t in range(DEC_SEQ):
            last = POOL_HIST + t
            s = seq[last - w + 1]
            for j in range(last - w + 2, last + 1):
                s = s + seq[j]
            cnt = float(min(w, N_HIST_SAMPLE + t + 1))
            d_ref[:, t * POOL_DIM + base:t * POOL_DIM + base + POOL_GROUP_DIM] = (
                s / cnt - seq[last])
    for j in range(POOL_HIST - DEC_SEQ):
        newpool_ref[j] = hist_ref[j + DEC_SEQ]
    for t in range(DEC_SEQ):
        newpool_ref[POOL_HIST - DEC_SEQ + t] = proj_ref[:, t * IN_DIM:t * IN_DIM + POOL_DIM]


def _sample_pool(layer, state_t, proj_flat):
    return pl.pallas_call(
        _sample_pool_body,
        out_shape=(jax.ShapeDtypeStruct((DEC_BATCH, DEC_SEQ * POOL_DIM), F32),
                   jax.ShapeDtypeStruct((POOL_HIST, DEC_BATCH, POOL_DIM), F32)),
        grid=(DEC_BATCH // POOL_SB,),
        in_specs=[
            pl.BlockSpec((None, POOL_HIST, POOL_SB, POOL_DIM), lambda i: (layer, 0, i, 0)),
            pl.BlockSpec((POOL_SB, DEC_SEQ * IN_DIM), lambda i: (i, 0)),
        ],
        out_specs=(pl.BlockSpec((POOL_SB, DEC_SEQ * POOL_DIM), lambda i: (i, 0)),
                   pl.BlockSpec((POOL_HIST, POOL_SB, POOL_DIM), lambda i: (0, i, 0))),
        compiler_params=_params("parallel"),
        name="sample_pool",
    )(state_t, proj_flat)


def _sample_attn_body(sink_ref, d_ref, q_ref, kvt_ref, ck_ref, cv_ref, x_ref,
                      pw_ref, ps_ref, wo_ref, o_ref, nk_ref, nv_ref, mix_ref):
    for gi in range(len(POOL_WINDOWS)):
        lanes = slice(gi * POOL_GROUP_DIM, (gi + 1) * POOL_GROUP_DIM)
        mix_ref[:, lanes] = _pool_linear(d_ref[:, lanes], gi, pw_ref, ps_ref).astype(BF16)

    pair = 2 * DEC_SEQ
    grp = GROUP * pair
    rows = N_KV_HEADS * grp
    new_cols = ATTN_SB * DEC_SEQ
    keys = N_BUF + new_cols
    assert keys == KV_DIM and new_cols == N_BUF
    head = _kv_head_of_lane((rows, KV_DIM))
    row = lax.broadcasted_iota(jnp.int32, (rows, keys), 0)
    row_head = row // grp
    in_pair = row % pair
    low = in_pair < DEC_SEQ
    tok = in_pair % DEC_SEQ
    col = lax.broadcasted_iota(jnp.int32, (rows, keys), 1)
    new_col = col - N_BUF
    valid_cache = (col < N_BUF) & (col > tok)
    causal_new = (new_col >= 0) & ((new_col % DEC_SEQ) <= tok)
    seq_of_col = new_col // DEC_SEQ - in_pair // DEC_SEQ
    sink = jnp.concatenate(
        [jnp.full((pair, 1), sink_ref[k * GROUP + g], F32)
         for k in range(N_KV_HEADS) for g in range(GROUP)], axis=0)
    lane = lax.broadcasted_iota(jnp.int32, (KV_DIM, N_BUF), 1)
    is_old = lane < N_BUF - DEC_SEQ
    kt_new = kvt_ref[:KV_DIM, :]
    vt_new = kvt_ref[KV_DIM:, :]
    kt_new_bf = kt_new.astype(BF16)
    vt_new_bf = vt_new.astype(BF16)
    nt = (((1,), (1,)), ((), ()))

    for j in range(ATTN_SB // 2):
        r0 = j * pair
        q8 = q_ref[r0:r0 + pair, :]
        qs = jnp.concatenate(
            [q8[:, g * KV_DIM:(g + 1) * KV_DIM] for g in range(GROUP)], axis=0)
        qrep = jnp.concatenate([qs] * N_KV_HEADS, axis=0)
        lhs = jnp.where(head == row_head, qrep, 0.0).astype(BF16)
        scores, values = [], []
        for b in (2 * j, 2 * j + 1):
            kt = ck_ref[b]
            vt = cv_ref[b]
            scores.append(jnp.dot(lhs, kt.astype(BF16), preferred_element_type=F32))
            values.append(vt.astype(BF16))
            shift_new = (N_BUF - DEC_SEQ - b * DEC_SEQ) % N_BUF
            nk_ref[b] = jnp.where(is_old, pltpu.roll(kt, N_BUF - DEC_SEQ, axis=1),
                                  pltpu.roll(kt_new, shift_new, axis=1))
            nv_ref[b] = jnp.where(is_old, pltpu.roll(vt, N_BUF - DEC_SEQ, axis=1),
                                  pltpu.roll(vt_new, shift_new, axis=1))
        s_new = jnp.dot(lhs, kt_new_bf, preferred_element_type=F32)
        s = jnp.concatenate([jnp.where(low[:, :N_BUF], scores[0], scores[1]), s_new],
                            axis=1)
        valid = valid_cache | (causal_new & (seq_of_col == 2 * j))
        s = jnp.where(valid, s, -jnp.inf)
        m = jnp.maximum(jnp.max(s, axis=-1, keepdims=True), sink)
        p = jnp.exp(s - m)
        den = jnp.sum(p, axis=-1, keepdims=True) + jnp.exp(sink - m)
        p = (p / den).astype(BF16)
        p_old = p[:, :N_BUF]
        o = jnp.where(low,
                      lax.dot_general(p_old, values[0], nt, preferred_element_type=F32),
                      lax.dot_general(p_old, values[1], nt, preferred_element_type=F32))
        o = o + lax.dot_general(p[:, N_BUF:], vt_new_bf, nt, preferred_element_type=F32)
        o = jnp.where(head == row_head, o, 0.0)
        og = o[0:grp]
        for k in range(1, N_KV_HEADS):
            og = og + o[k * grp:(k + 1) * grp]
        for g in range(GROUP):
            mix_ref[r0:r0 + pair,
                    POOL_DIM + g * KV_DIM:POOL_DIM + (g + 1) * KV_DIM] = (
                        og[g * pair:(g + 1) * pair].astype(BF16))

    o_ref[...] = x_ref[...] + jnp.dot(mix_ref[...], wo_ref[...],
                                      preferred_element_type=F32)


def _sample_attn(layer, sinks, d_rows, proj, kvt, cache_kt, cache_vt, x, pool_w,
                 pool_scale, w_out):
    cache_spec = pl.BlockSpec((None, ATTN_SB, KV_DIM, N_BUF), lambda i: (layer, i, 0, 0))
    new_cache = jax.ShapeDtypeStruct((DEC_BATCH, KV_DIM, N_BUF), F32)
    new_cache_spec = pl.BlockSpec((ATTN_SB, KV_DIM, N_BUF), lambda i: (i, 0, 0))
    return pl.pallas_call(
        _sample_attn_body,
        out_shape=(jax.ShapeDtypeStruct((SAMPLE_ROWS, D_MODEL), F32),
                   new_cache, new_cache),
        grid=(DEC_BATCH // ATTN_SB,),
        in_specs=[
            pl.BlockSpec(memory_space=pltpu.SMEM),
            pl.BlockSpec((ATTN_ROWS, POOL_DIM), lambda i: (i, 0)),
            pl.BlockSpec((ATTN_ROWS, Q_DIM), lambda i: (i, 1)),
            pl.BlockSpec((2 * KV_DIM, ATTN_ROWS), lambda i: (0, i)),
            cache_spec,
            cache_spec,
            pl.BlockSpec((ATTN_ROWS, D_MODEL), lambda i: (i, 0)),
            pl.BlockSpec((len(POOL_WINDOWS), POOL_GROUP_DIM, POOL_GROUP_DIM),
                         lambda i: (0, 0, 0)),
            pl.BlockSpec((1, POOL_DIM), lambda i: (0, 0)),
            pl.BlockSpec((D_MODEL, D_MODEL), lambda i: (0, 0),
                         pipeline_mode=pl.Buffered(1)),
        ],
        out_specs=(pl.BlockSpec((ATTN_ROWS, D_MODEL), lambda i: (i, 0)),
                   new_cache_spec, new_cache_spec),
        scratch_shapes=[pltpu.VMEM((ATTN_ROWS, D_MODEL), BF16)],
        compiler_params=_params("parallel"),
        name="sample_attn",
    )(sinks, d_rows, proj, kvt, cache_kt, cache_vt, x, pool_w, pool_scale, w_out)


def _group_major(a, axis):
    shape = a.shape
    a = a.reshape(shape[:axis] + (N_KV_HEADS, GROUP, HEAD_DIM) + shape[axis + 1:])
    a = jnp.swapaxes(a, axis, axis + 1)
    return a.reshape(shape)


def kernel(x_prompt, x_sample, cache_k, cache_v, state_pool, norm_ffn1, ffn1_gate,
           ffn1_up, ffn1_down, norm_mix, w_in, pool_w, pool_scale, attn_sinks,
           w_out, norm_ffn2, ffn2_gate, ffn2_up, ffn2_down, final_norm):
    xp = x_prompt.reshape(PROMPT_ROWS, D_MODEL)
    xs = x_sample.reshape(SAMPLE_ROWS, D_MODEL)
    gf = final_norm.reshape(1, D_MODEL)
    state_t = jnp.transpose(state_pool, (0, 2, 1, 3))

    def to_cache_t(c):
        return jnp.transpose(c, (0, 1, 3, 4, 2)).reshape(DEPTH, DEC_BATCH, KV_DIM, N_BUF)

    def from_cache_t(c):
        c = c.reshape(DEPTH, DEC_BATCH, N_KV_HEADS, HEAD_DIM, N_BUF)
        return jnp.transpose(c, (0, 1, 4, 2, 3))

    cache_kt = to_cache_t(cache_k)
    cache_vt = to_cache_t(cache_v)
    keep = min(WINDOW, SEQ)
    assert keep >= POOL_HIST
    k0 = POOL_DIM + Q_DIM

    ffn1_w = (ffn1_gate, ffn1_up, ffn1_down)
    ffn2_w = (ffn2_gate, ffn2_up, ffn2_down)
    w_bf = _cast_ffn(0, *ffn1_w)
    kp_l, vp_l, pp_l, ks_l, vs_l, ps_l = [], [], [], [], [], []
    for l in range(DEPTH):
        last = l == DEPTH - 1
        wq = _group_major(w_in[l][:, POOL_DIM:k0], 1) * ATTN_SCALE
        w_in_l = jnp.concatenate(
            [w_in[l][:, :POOL_DIM], wq, w_in[l][:, k0:]], axis=1).astype(BF16)
        w_out_l = jnp.concatenate(
            [w_out[l][:POOL_DIM], _group_major(w_out[l][POOL_DIM:], 0)], axis=0).astype(BF16)
        pool_w_l = pool_w[l].astype(BF16)
        pool_scale_l = pool_scale[l].reshape(1, POOL_DIM)
        sinks_l = attn_sinks[l].astype(F32)

        xp, xs, w_bf = _ffn(xp, xs, norm_ffn1[l].reshape(1, D_MODEL), *w_bf, gf,
                            next_w=(l,) + ffn2_w)
        proj_p, proj_s, kvt = _inproj(xp, xs, norm_mix[l].reshape(1, D_MODEL), w_in_l,
                                      w_in[l][:, k0:].T.astype(BF16))

        tails = [proj_p[(b + 1) * SEQ - keep:(b + 1) * SEQ] for b in range(BATCH)]
        kp_l.append(jnp.stack([t[:, k0:k0 + KV_DIM] for t in tails])
                    .reshape(BATCH, keep, N_KV_HEADS, HEAD_DIM))
        vp_l.append(jnp.stack([t[:, k0 + KV_DIM:] for t in tails])
                    .reshape(BATCH, keep, N_KV_HEADS, HEAD_DIM))
        pp_l.append(jnp.stack([t[keep - POOL_HIST:, :POOL_DIM] for t in tails]))

        d_flat, new_pool = _sample_pool(
            l, state_t, proj_s.reshape(DEC_BATCH, DEC_SEQ * IN_DIM))
        ps_l.append(new_pool)

        xp = _prompt_mix(sinks_l, proj_p, xp, pool_w_l, pool_scale_l, w_out_l)
        xs, new_kt, new_vt = _sample_attn(
            l, sinks_l, d_flat.reshape(SAMPLE_ROWS, POOL_DIM), proj_s, kvt,
            cache_kt, cache_vt, xs, pool_w_l, pool_scale_l, w_out_l)
        ks_l.append(new_kt)
        vs_l.append(new_vt)

        xp, xs, w_bf = _ffn(xp, xs, norm_ffn2[l].reshape(1, D_MODEL), *w_bf, gf,
                            next_w=None if last else (l + 1,) + ffn1_w,
                            final_norm=last)

    y_prompt = xp.reshape(BATCH, SEQ, D_MODEL)
    y_sample = xs.reshape(DEC_BATCH, DEC_SEQ, D_MODEL)
    new_pool_sample = jnp.transpose(jnp.stack(ps_l), (0, 2, 1, 3))
    return (y_prompt, y_sample, jnp.stack(kp_l), jnp.stack(vp_l), jnp.stack(pp_l),
            from_cache_t(jnp.stack(ks_l)), from_cache_t(jnp.stack(vs_l)),
            new_pool_sample)
```

```python
import functools

import jax
import jax.numpy as jnp
from jax import lax
from jax.experimental import pallas as pl
from jax.experimental.pallas import tpu as pltpu

F32 = jnp.float32
BF16 = jnp.bfloat16

D_MODEL = 2048
BATCH = 2
SEQ = 4096
DEPTH = 2
DEC_BATCH = 128
DEC_SEQ = 4
PAST_LEN = 8192

POOL_DIM = 1024
POOL_WINDOWS = (2, 4, 8, 16)
POOL_GROUP_DIM = POOL_DIM // len(POOL_WINDOWS)
POOL_HIST = max(POOL_WINDOWS) - 1
HEAD_DIM = 64
N_HEADS = 16
N_KV_HEADS = 4
GROUP = N_HEADS // N_KV_HEADS
Q_DIM = N_HEADS * HEAD_DIM
KV_DIM = N_KV_HEADS * HEAD_DIM
IN_DIM = POOL_DIM + Q_DIM + 2 * KV_DIM
WINDOW = 128
FFN_DIM = 5632
RMS_EPS = 1e-5
ATTN_SCALE = HEAD_DIM ** -0.5
N_BUF = min(WINDOW, PAST_LEN)
N_HIST_SAMPLE = min(POOL_HIST, PAST_LEN)

PROMPT_ROWS = BATCH * SEQ
SAMPLE_ROWS = DEC_BATCH * DEC_SEQ
M_ROWS = PROMPT_ROWS + SAMPLE_ROWS

VMEM_LIMIT_BYTES = 56 * 1024 * 1024

FFN_TM = 1088
FFN_TF = 512
FFN_TILES = M_ROWS // FFN_TM
FFN_CHUNKS = FFN_DIM // FFN_TF
FFN_FULL_TILES = PROMPT_ROWS // FFN_TM
FFN_SPLIT = PROMPT_ROWS - FFN_FULL_TILES * FFN_TM
CAST_ROWS = D_MODEL // FFN_TILES
WIN_ROWS = 32
WIN_STEPS = D_MODEL // (FFN_TILES * WIN_ROWS)
WOUT_STEPS = D_MODEL // (FFN_TILES * HEAD_DIM)
PROJ_TM = 512
PROJ_PROMPT_TILES = PROMPT_ROWS // PROJ_TM
MIX_TQ = 512
HALO_U = 16
OUT_BLOCKS = 2
ATT_CHUNK = 64
POOL_SB = 32
ATTN_SB = 32
ATTN_ROWS = ATTN_SB * DEC_SEQ

assert FFN_TILES * FFN_TM == M_ROWS and FFN_CHUNKS * FFN_TF == FFN_DIM
assert FFN_FULL_TILES + 1 == FFN_TILES and FFN_SPLIT + SAMPLE_ROWS == FFN_TM
assert FFN_SPLIT % 8 == 0 and CAST_ROWS * FFN_TILES == D_MODEL
assert FFN_CHUNKS >= 2
assert WIN_STEPS * FFN_TILES * WIN_ROWS == D_MODEL and WIN_STEPS <= FFN_CHUNKS
assert WOUT_STEPS * FFN_TILES * HEAD_DIM == D_MODEL and WOUT_STEPS <= FFN_CHUNKS
assert SAMPLE_ROWS == PROJ_TM and PROJ_PROMPT_TILES * PROJ_TM == PROMPT_ROWS


def _rms(x, g):
    return x * lax.rsqrt(jnp.mean(x * x, axis=-1, keepdims=True) + RMS_EPS) * g


def _params(*sem):
    return pltpu.CompilerParams(dimension_semantics=sem,
                                vmem_limit_bytes=VMEM_LIMIT_BYTES)


def _ffn_body(*refs, final_norm, cast_next, prep_mix):
    xp_hbm, xs_hbm, g_ref, wg_ref, wu_ref, wd_ref, gf_ref = refs[:7]
    refs = refs[7:]
    if cast_next:
        ng_ref, nu_ref, nd_ref = refs[:3]
        refs = refs[3:]
    if prep_mix:
        win_ref, wout_ref = refs[:2]
        refs = refs[2:]
    op_hbm, os_hbm = refs[:2]
    refs = refs[2:]
    if cast_next:
        cg_ref, cu_ref, cd_ref = refs[:3]
        refs = refs[3:]
    if prep_mix:
        cwin_ref, cwout_ref = refs[:2]
        refs = refs[2:]
    acc_ref, xn_ref, in_sem, out_sem = refs

    i = pl.program_id(0)
    f = pl.program_id(1)
    slot = i % 2

    def copies(p_hbm, s_hbm, tile, s, sem, mixed, to_vmem):
        if mixed:
            pairs = [(p_hbm.at[pl.ds(FFN_FULL_TILES * FFN_TM, FFN_SPLIT)],
                      acc_ref.at[s, pl.ds(0, FFN_SPLIT)]),
                     (s_hbm, acc_ref.at[s, pl.ds(FFN_SPLIT, SAMPLE_ROWS)])]
        else:
            pairs = [(p_hbm.at[pl.ds(tile * FFN_TM, FFN_TM)], acc_ref.at[s])]
        return [pltpu.make_async_copy(h, v, sem.at[s]) if to_vmem
                else pltpu.make_async_copy(v, h, sem.at[s]) for h, v in pairs]

    def on_tile(tile, s, op, to_vmem):
        p_hbm, s_hbm, sem = ((xp_hbm, xs_hbm, in_sem) if to_vmem
                             else (op_hbm, os_hbm, out_sem))

        @pl.when(tile < FFN_FULL_TILES)
        def _():
            for c in copies(p_hbm, s_hbm, tile, s, sem, False, to_vmem):
                op(c)

        @pl.when(tile == FFN_FULL_TILES)
        def _():
            for c in copies(p_hbm, s_hbm, tile, s, sem, True, to_vmem):
                op(c)

    start = lambda c: c.start()
    wait = lambda c: c.wait()

    @pl.when((i == 0) & (f == 0))
    def _():
        on_tile(i, slot, start, True)

    @pl.when(f == 0)
    def _():
        on_tile(i, slot, wait, True)
        xn_ref[...] = _rms(acc_ref[slot], g_ref[...]).astype(BF16)

    a = xn_ref[...]
    gate = jnp.dot(a, wg_ref[...], preferred_element_type=F32)
    up = jnp.dot(a, wu_ref[...], preferred_element_type=F32)
    h = (gate * jax.nn.sigmoid(gate) * up).astype(BF16)
    acc_ref[slot] += 0.5 * jnp.dot(h, wd_ref[...], preferred_element_type=F32)

    if cast_next:
        cg_ref[...] = ng_ref[...].astype(BF16)
        cu_ref[...] = nu_ref[...].astype(BF16)
        cd_ref[...] = nd_ref[...].astype(BF16)

    if prep_mix:
        _prep_w_in(win_ref, cwin_ref)
        cwout_ref[...] = wout_ref[...].astype(BF16)

    @pl.when(f == 1)
    def _():
        @pl.when(i >= 1)
        def _():
            on_tile(i - 1, 1 - slot, wait, False)

        @pl.when(i + 1 < FFN_TILES)
        def _():
            on_tile(i + 1, 1 - slot, start, True)

    @pl.when(f == FFN_CHUNKS - 1)
    def _():
        if final_norm:
            acc_ref[slot] = _rms(acc_ref[slot], gf_ref[...])
        on_tile(i, slot, start, False)

        @pl.when(i == FFN_TILES - 1)
        def _():
            on_tile(i, slot, wait, False)


def _prep_w_in(w_ref, o_ref):
    k0 = POOL_DIM + Q_DIM
    o_ref[:, :POOL_DIM] = w_ref[:, :POOL_DIM].astype(BF16)
    o_ref[:, k0:] = w_ref[:, k0:].astype(BF16)
    lane = lax.broadcasted_iota(jnp.int32, (WIN_ROWS, 2 * HEAD_DIM), 1)
    for g in range(GROUP):
        for kp in range(N_KV_HEADS // 2):
            src = [POOL_DIM + (2 * kp + j) * KV_DIM + (g // 2) * 2 * HEAD_DIM
                   for j in (0, 1)]
            lo, hi = [w_ref[:, c:c + 2 * HEAD_DIM] for c in src]
            if g % 2 == 0:
                hi = pltpu.roll(hi, HEAD_DIM, axis=1)
            else:
                lo = pltpu.roll(lo, HEAD_DIM, axis=1)
            dst = POOL_DIM + g * KV_DIM + kp * 2 * HEAD_DIM
            o_ref[:, dst:dst + 2 * HEAD_DIM] = (
                jnp.where(lane < HEAD_DIM, lo, hi) * ATTN_SCALE).astype(BF16)


def _w_out_src_block(out_blk):
    pool_blocks = POOL_DIM // HEAD_DIM
    h = out_blk - pool_blocks
    src = pool_blocks + (h % N_KV_HEADS) * GROUP + h // N_KV_HEADS
    return jnp.where(out_blk < pool_blocks, out_blk, src)


def _ffn(xp, xs, g, wg, wu, wd, gf, next_w=None, mix_w=None, *, final_norm=False):
    cast_next = next_w is not None
    prep_mix = mix_w is not None
    vec = pl.BlockSpec((1, D_MODEL), lambda i, f: (0, 0))
    in_specs = [
        pl.BlockSpec(memory_space=pl.ANY),
        pl.BlockSpec(memory_space=pl.ANY),
        vec,
        pl.BlockSpec((D_MODEL, FFN_TF), lambda i, f: (0, f)),
        pl.BlockSpec((D_MODEL, FFN_TF), lambda i, f: (0, f)),
        pl.BlockSpec((FFN_TF, D_MODEL), lambda i, f: (f, 0)),
        vec,
    ]
    out_shape = [jax.ShapeDtypeStruct((PROMPT_ROWS, D_MODEL), F32),
                 jax.ShapeDtypeStruct((SAMPLE_ROWS, D_MODEL), F32)]
    out_specs = [pl.BlockSpec(memory_space=pl.ANY), pl.BlockSpec(memory_space=pl.ANY)]
    args = [xp, xs, g, wg, wu, wd, gf]
    if cast_next:
        nl = next_w[0]
        in_specs += [
            pl.BlockSpec((None, CAST_ROWS, FFN_TF), lambda i, f: (nl, i, f)),
            pl.BlockSpec((None, CAST_ROWS, FFN_TF), lambda i, f: (nl, i, f)),
            pl.BlockSpec((None, FFN_TF, CAST_ROWS), lambda i, f: (nl, f, i)),
        ]
        out_shape += [jax.ShapeDtypeStruct((D_MODEL, FFN_DIM), BF16),
                      jax.ShapeDtypeStruct((D_MODEL, FFN_DIM), BF16),
                      jax.ShapeDtypeStruct((FFN_DIM, D_MODEL), BF16)]
        out_specs += [
            pl.BlockSpec((CAST_ROWS, FFN_TF), lambda i, f: (i, f)),
            pl.BlockSpec((CAST_ROWS, FFN_TF), lambda i, f: (i, f)),
            pl.BlockSpec((FFN_TF, CAST_ROWS), lambda i, f: (f, i)),
        ]
        args += list(next_w[1:])
    if prep_mix:
        ml = mix_w[0]
        win_blk = lambda i, f: i * WIN_STEPS + jnp.minimum(f, WIN_STEPS - 1)
        wout_blk = lambda i, f: i * WOUT_STEPS + jnp.minimum(f, WOUT_STEPS - 1)
        in_specs += [
            pl.BlockSpec((None, WIN_ROWS, IN_DIM), lambda i, f: (ml, win_blk(i, f), 0)),
            pl.BlockSpec((None, HEAD_DIM, D_MODEL),
                         lambda i, f: (ml, _w_out_src_block(wout_blk(i, f)), 0)),
        ]
        out_shape += [jax.ShapeDtypeStruct((D_MODEL, IN_DIM), BF16),
                      jax.ShapeDtypeStruct((D_MODEL, D_MODEL), BF16)]
        out_specs += [
            pl.BlockSpec((WIN_ROWS, IN_DIM), lambda i, f: (win_blk(i, f), 0)),
            pl.BlockSpec((HEAD_DIM, D_MODEL), lambda i, f: (wout_blk(i, f), 0)),
        ]
        args += list(mix_w[1:])
    outs = pl.pallas_call(
        functools.partial(_ffn_body, final_norm=final_norm, cast_next=cast_next,
                          prep_mix=prep_mix),
        out_shape=out_shape,
        grid=(FFN_TILES, FFN_CHUNKS),
        in_specs=in_specs,
        out_specs=out_specs,
        scratch_shapes=[pltpu.VMEM((2, FFN_TM, D_MODEL), F32),
                        pltpu.VMEM((FFN_TM, D_MODEL), BF16),
                        pltpu.SemaphoreType.DMA((2,)),
                        pltpu.SemaphoreType.DMA((2,))],
        compiler_params=_params("arbitrary", "arbitrary"),
        name="ffn_final" if final_norm else "ffn",
    )(*args)
    n_cast = 3 if cast_next else 0
    return outs[0], outs[1], tuple(outs[2:2 + n_cast]), tuple(outs[2 + n_cast:])


def _cast_ffn_body(g_ref, u_ref, d_ref, og_ref, ou_ref, od_ref):
    og_ref[...] = g_ref[...].astype(BF16)
    ou_ref[...] = u_ref[...].astype(BF16)
    od_ref[...] = d_ref[...].astype(BF16)


def _cast_ffn(layer, gate, up, down):
    return pl.pallas_call(
        _cast_ffn_body,
        out_shape=(jax.ShapeDtypeStruct((D_MODEL, FFN_DIM), BF16),
                   jax.ShapeDtypeStruct((D_MODEL, FFN_DIM), BF16),
                   jax.ShapeDtypeStruct((FFN_DIM, D_MODEL), BF16)),
        grid=(FFN_CHUNKS,),
        in_specs=[
            pl.BlockSpec((None, D_MODEL, FFN_TF), lambda j: (layer, 0, j)),
            pl.BlockSpec((None, D_MODEL, FFN_TF), lambda j: (layer, 0, j)),
            pl.BlockSpec((None, FFN_TF, D_MODEL), lambda j: (layer, j, 0)),
        ],
        out_specs=(pl.BlockSpec((D_MODEL, FFN_TF), lambda j: (0, j)),
                   pl.BlockSpec((D_MODEL, FFN_TF), lambda j: (0, j)),
                   pl.BlockSpec((FFN_TF, D_MODEL), lambda j: (j, 0))),
        compiler_params=_params("parallel"),
        name="cast_ffn",
    )(gate, up, down)


def _inproj_body(xp_ref, xs_ref, g_ref, w_ref, pp_ref, ps_ref, kvt_ref):
    i = pl.program_id(0)

    @pl.when(i < PROJ_PROMPT_TILES)
    def _():
        xn = _rms(xp_ref[...], g_ref[...]).astype(BF16)
        pp_ref[...] = jnp.dot(xn, w_ref[...], preferred_element_type=F32)

    @pl.when(i == PROJ_PROMPT_TILES)
    def _():
        xn = _rms(xs_ref[...], g_ref[...]).astype(BF16)
        proj = jnp.dot(xn, w_ref[...], preferred_element_type=F32)
        ps_ref[...] = proj
        kvt_ref[...] = proj[:, POOL_DIM + Q_DIM:].T


def _inproj(xp, xs, g, w):
    prompt_tile = lambda i: (jnp.minimum(i, PROJ_PROMPT_TILES - 1), 0)
    fixed = lambda i: (0, 0)
    return pl.pallas_call(
        _inproj_body,
        out_shape=(jax.ShapeDtypeStruct((PROMPT_ROWS, IN_DIM), F32),
                   jax.ShapeDtypeStruct((SAMPLE_ROWS, IN_DIM), F32),
                   jax.ShapeDtypeStruct((2 * KV_DIM, SAMPLE_ROWS), F32)),
        grid=(PROJ_PROMPT_TILES + 1,),
        in_specs=[
            pl.BlockSpec((PROJ_TM, D_MODEL), prompt_tile),
            pl.BlockSpec((PROJ_TM, D_MODEL), fixed),
            pl.BlockSpec((1, D_MODEL), fixed),
            pl.BlockSpec((D_MODEL, IN_DIM), fixed),
        ],
        out_specs=(pl.BlockSpec((PROJ_TM, IN_DIM), prompt_tile),
                   pl.BlockSpec((PROJ_TM, IN_DIM), fixed),
                   pl.BlockSpec((2 * KV_DIM, SAMPLE_ROWS), fixed)),
        compiler_params=_params("arbitrary"),
        name="inproj",
    )(xp, xs, g, w)


def _kv_head_of_lane(shape):
    return lax.broadcasted_iota(jnp.int32, shape, 1) // HEAD_DIM


def _pool_linear(d, gi, pw_ref, ps_ref):
    lanes = slice(gi * POOL_GROUP_DIM, (gi + 1) * POOL_GROUP_DIM)
    y = jnp.dot(d.astype(BF16), pw_ref[gi], preferred_element_type=F32)
    return y * ps_ref[:, lanes]


def _prompt_mix_body(sink_ref, u_ref, q_ref, kv_ref, kvh_ref, uh_ref, x_ref,
                     pw_ref, ps_ref, wo_ref, o_ref, mix_ref, p_ref):
    tiles_per_seq = SEQ // MIX_TQ
    tile = pl.program_id(0) % tiles_per_seq
    first = tile == 0

    lo = jnp.where(first, WINDOW, 0)
    keep_u = jnp.full((HALO_U, POOL_DIM), lo, jnp.int32) == 0
    keep_kv = jnp.full((WINDOW, 2 * KV_DIM), lo, jnp.int32) == 0
    uh = jnp.where(keep_u, uh_ref[...], 0.0)
    pos = tile * MIX_TQ + lax.broadcasted_iota(jnp.int32, (MIX_TQ, 1), 0)
    for gi, w in enumerate(POOL_WINDOWS):
        lanes = slice(gi * POOL_GROUP_DIM, (gi + 1) * POOL_GROUP_DIM)
        xg = jnp.concatenate([uh[:, lanes], u_ref[:, lanes]], axis=0)
        s = xg
        sh = 1
        while sh < w:
            s = s + pltpu.roll(s, sh, axis=0)
            sh *= 2
        cnt = jnp.minimum(w, pos + 1).astype(F32)
        d = s[HALO_U:] / cnt - xg[HALO_U:]
        mix_ref[:, lanes] = _pool_linear(d, gi, pw_ref, ps_ref).astype(BF16)

    kvh = jnp.where(keep_kv, kvh_ref[...], 0.0)
    k_all = jnp.concatenate([kvh[:, :KV_DIM], kv_ref[:, :KV_DIM]], axis=0)
    v_all = jnp.concatenate([kvh[:, KV_DIM:], kv_ref[:, KV_DIM:]], axis=0)
    head = _kv_head_of_lane((2 * WINDOW, KV_DIM))
    qi = lax.broadcasted_iota(jnp.int32, (ATT_CHUNK, 2 * WINDOW), 0)
    kj = lax.broadcasted_iota(jnp.int32, (ATT_CHUNK, 2 * WINDOW), 1)
    for n in range(MIX_TQ // WINDOW):
        kw = k_all[n * WINDOW:(n + 2) * WINDOW]
        vw = v_all[n * WINDOW:(n + 2) * WINDOW]
        qn = q_ref[n * WINDOW:(n + 1) * WINDOW, :].astype(BF16)
        qs = jnp.concatenate(
            [qn[:, g * KV_DIM:(g + 1) * KV_DIM] for g in range(GROUP)], axis=0)
        biases = []
        for c in range(WINDOW // ATT_CHUNK):
            q_pos = qi + c * ATT_CHUNK
            valid = (kj > q_pos) & (kj <= q_pos + WINDOW)
            if n == 0:
                valid = valid & (kj >= lo)
            biases.append(jnp.where(valid, 0.0, -jnp.inf))
        for k in range(N_KV_HEADS):
            kk = jnp.where(head == k, kw, 0.0).astype(BF16)
            s = lax.dot_general(qs, kk, (((1,), (1,)), ((), ())),
                                preferred_element_type=F32)
            for g in range(GROUP):
                sink = sink_ref[k * GROUP + g]
                for c in range(WINDOW // ATT_CHUNK):
                    r0 = g * WINDOW + c * ATT_CHUNK
                    sc = s[r0:r0 + ATT_CHUNK] + biases[c]
                    m = jnp.maximum(jnp.max(sc, axis=-1, keepdims=True), sink)
                    p = jnp.exp(sc - m)
                    den = jnp.sum(p, axis=-1, keepdims=True) + jnp.exp(sink - m)
                    p_ref[r0:r0 + ATT_CHUNK, k * 2 * WINDOW:(k + 1) * 2 * WINDOW] = (
                        p * (1.0 / den)).astype(BF16)
        vcat = jnp.concatenate(
            [jnp.where(head == k, vw, 0.0).astype(BF16)
             for k in range(N_KV_HEADS)], axis=0)
        o = jnp.dot(p_ref[...], vcat, preferred_element_type=F32)
        blk = slice(n * WINDOW, (n + 1) * WINDOW)
        for g in range(GROUP):
            mix_ref[blk, POOL_DIM + g * KV_DIM:POOL_DIM + (g + 1) * KV_DIM] = (
                o[g * WINDOW:(g + 1) * WINDOW].astype(BF16))
        if (n + 1) % OUT_BLOCKS == 0:
            done = slice((n + 1 - OUT_BLOCKS) * WINDOW, (n + 1) * WINDOW)
            o_ref[done, :] = x_ref[done, :] + jnp.dot(mix_ref[done, :], wo_ref[...],
                                                      preferred_element_type=F32)


def _prompt_mix(sinks, proj, x, pool_w, pool_scale, w_out):
    blocks_kv = MIX_TQ // WINDOW
    blocks_u = MIX_TQ // HALO_U
    return pl.pallas_call(
        _prompt_mix_body,
        out_shape=jax.ShapeDtypeStruct((PROMPT_ROWS, D_MODEL), F32),
        grid=(PROMPT_ROWS // MIX_TQ,),
        in_specs=[
            pl.BlockSpec(memory_space=pltpu.SMEM),
            pl.BlockSpec((MIX_TQ, POOL_DIM), lambda i: (i, 0)),
            pl.BlockSpec((MIX_TQ, Q_DIM), lambda i: (i, 1)),
            pl.BlockSpec((MIX_TQ, 2 * KV_DIM), lambda i: (i, 4)),
            pl.BlockSpec((WINDOW, 2 * KV_DIM),
                         lambda i: (jnp.maximum(i * blocks_kv - 1, 0), 4)),
            pl.BlockSpec((HALO_U, POOL_DIM),
                         lambda i: (jnp.maximum(i * blocks_u - 1, 0), 0)),
            pl.BlockSpec((MIX_TQ, D_MODEL), lambda i: (i, 0)),
            pl.BlockSpec((len(POOL_WINDOWS), POOL_GROUP_DIM, POOL_GROUP_DIM),
                         lambda i: (0, 0, 0)),
            pl.BlockSpec((1, POOL_DIM), lambda i: (0, 0)),
            pl.BlockSpec((D_MODEL, D_MODEL), lambda i: (0, 0)),
        ],
        out_specs=pl.BlockSpec((MIX_TQ, D_MODEL), lambda i: (i, 0)),
        scratch_shapes=[pltpu.VMEM((MIX_TQ, D_MODEL), BF16),
                        pltpu.VMEM((GROUP * WINDOW, N_KV_HEADS * 2 * WINDOW), BF16)],
        compiler_params=_params("parallel"),
        name="prompt_mix",
    )(sinks, proj, proj, proj, proj, proj, x, pool_w, pool_scale, w_out)


def _sample_pool_body(hist_ref, proj_ref, d_ref, newpool_ref):
    for gi, w in enumerate(POOL_WINDOWS):
        base = gi * POOL_GROUP_DIM
        seq = [hist_ref[j, :, base:base + POOL_GROUP_DIM] for j in range(POOL_HIST)]
        seq += [proj_ref[:, t * IN_DIM + base:t * IN_DIM + base + POOL_GROUP_DIM]
                for t in range(DEC_SEQ)]
        for t in range(DEC_SEQ):
            last = POOL_HIST + t
            s = seq[last - w + 1]
            for j in range(last - w + 2, last + 1):
                s = s + seq[j]
            cnt = float(min(w, N_HIST_SAMPLE + t + 1))
            d_ref[:, t * POOL_DIM + base:t * POOL_DIM + base + POOL_GROUP_DIM] = (
                s / cnt - seq[last])
    for j in range(POOL_HIST - DEC_SEQ):
        newpool_ref[j] = hist_ref[j + DEC_SEQ]
    for t in range(DEC_SEQ):
        newpool_ref[POOL_HIST - DEC_SEQ + t] = proj_ref[:, t * IN_DIM:t * IN_DIM + POOL_DIM]


def _sample_pool(layer, state_t, proj_flat):
    return pl.pallas_call(
        _sample_pool_body,
        out_shape=(jax.ShapeDtypeStruct((DEC_BATCH, DEC_SEQ * POOL_DIM), F32),
                   jax.ShapeDtypeStruct((POOL_HIST, DEC_BATCH, POOL_DIM), F32)),
        grid=(DEC_BATCH // POOL_SB,),
        in_specs=[
            pl.BlockSpec((None, POOL_HIST, POOL_SB, POOL_DIM), lambda i: (layer, 0, i, 0)),
            pl.BlockSpec((POOL_SB, DEC_SEQ * IN_DIM), lambda i: (i, 0)),
        ],
        out_specs=(pl.BlockSpec((POOL_SB, DEC_SEQ * POOL_DIM), lambda i: (i, 0)),
                   pl.BlockSpec((POOL_HIST, POOL_SB, POOL_DIM), lambda i: (0, i, 0))),
        compiler_params=_params("parallel"),
        name="sample_pool",
    )(state_t, proj_flat)


def _sample_attn_body(sink_ref, d_ref, q_ref, kvt_ref, ck_ref, cv_ref, x_ref,
                      pw_ref, ps_ref, wo_ref, o_ref, nk_ref, nv_ref, mix_ref):
    for gi in range(len(POOL_WINDOWS)):
        lanes = slice(gi * POOL_GROUP_DIM, (gi + 1) * POOL_GROUP_DIM)
        mix_ref[:, lanes] = _pool_linear(d_ref[:, lanes], gi, pw_ref, ps_ref).astype(BF16)

    pair = 2 * DEC_SEQ
    grp = GROUP * pair
    rows = N_KV_HEADS * grp
    new_cols = ATTN_SB * DEC_SEQ
    keys = N_BUF + new_cols
    assert keys == KV_DIM and new_cols == N_BUF
    head = _kv_head_of_lane((rows, KV_DIM))
    row = lax.broadcasted_iota(jnp.int32, (rows, keys), 0)
    row_head = row // grp
    in_pair = row % pair
    low = in_pair < DEC_SEQ
    tok = in_pair % DEC_SEQ
    col = lax.broadcasted_iota(jnp.int32, (rows, keys), 1)
    new_col = col - N_BUF
    valid_cache = (col < N_BUF) & (col > tok)
    causal_new = (new_col >= 0) & ((new_col % DEC_SEQ) <= tok)
    seq_of_col = new_col // DEC_SEQ - in_pair // DEC_SEQ
    sink = jnp.concatenate(
        [jnp.full((pair, 1), sink_ref[k * GROUP + g], F32)
         for k in range(N_KV_HEADS) for g in range(GROUP)], axis=0)
    lane = lax.broadcasted_iota(jnp.int32, (KV_DIM, N_BUF), 1)
    is_old = lane < N_BUF - DEC_SEQ
    kt_new = kvt_ref[:KV_DIM, :]
    vt_new = kvt_ref[KV_DIM:, :]
    kt_new_bf = kt_new.astype(BF16)
    vt_new_bf = vt_new.astype(BF16)
    nt = (((1,), (1,)), ((), ()))

    for j in range(ATTN_SB // 2):
        r0 = j * pair
        q8 = q_ref[r0:r0 + pair, :]
        qs = jnp.concatenate(
            [q8[:, g * KV_DIM:(g + 1) * KV_DIM] for g in range(GROUP)], axis=0)
        qrep = jnp.concatenate([qs] * N_KV_HEADS, axis=0)
        lhs = jnp.where(head == row_head, qrep, 0.0).astype(BF16)
        scores, values = [], []
        for b in (2 * j, 2 * j + 1):
            kt = ck_ref[b]
            vt = cv_ref[b]
            scores.append(jnp.dot(lhs, kt.astype(BF16), preferred_element_type=F32))
            values.append(vt.astype(BF16))
            shift_new = (N_BUF - DEC_SEQ - b * DEC_SEQ) % N_BUF
            nk_ref[b] = jnp.where(is_old, pltpu.roll(kt, N_BUF - DEC_SEQ, axis=1),
                                  pltpu.roll(kt_new, shift_new, axis=1))
            nv_ref[b] = jnp.where(is_old, pltpu.roll(vt, N_BUF - DEC_SEQ, axis=1),
                                  pltpu.roll(vt_new, shift_new, axis=1))
        s_new = jnp.dot(lhs, kt_new_bf, preferred_element_type=F32)
        s = jnp.concatenate([jnp.where(low[:, :N_BUF], scores[0], scores[1]), s_new],
                            axis=1)
        valid = valid_cache | (causal_new & (seq_of_col == 2 * j))
        s = jnp.where(valid, s, -jnp.inf)
        m = jnp.maximum(jnp.max(s, axis=-1, keepdims=True), sink)
        p = jnp.exp(s - m)
        den = jnp.sum(p, axis=-1, keepdims=True) + jnp.exp(sink - m)
        p = (p / den).astype(BF16)
        p_old = p[:, :N_BUF]
        o = jnp.where(low,
                      lax.dot_general(p_old, values[0], nt, preferred_element_type=F32),
                      lax.dot_general(p_old, values[1], nt, preferred_element_type=F32))
        o = o + lax.dot_general(p[:, N_BUF:], vt_new_bf, nt, preferred_element_type=F32)
        o = jnp.where(head == row_head, o, 0.0)
        og = o[0:grp]
        for k in range(1, N_KV_HEADS):
            og = og + o[k * grp:(k + 1) * grp]
        for g in range(GROUP):
            mix_ref[r0:r0 + pair,
                    POOL_DIM + g * KV_DIM:POOL_DIM + (g + 1) * KV_DIM] = (
                        og[g * pair:(g + 1) * pair].astype(BF16))

    o_ref[...] = x_ref[...] + jnp.dot(mix_ref[...], wo_ref[...],
                                      preferred_element_type=F32)


def _sample_attn(layer, sinks, d_rows, proj, kvt, cache_kt, cache_vt, x, pool_w,
                 pool_scale, w_out):
    cache_spec = pl.BlockSpec((None, ATTN_SB, KV_DIM, N_BUF), lambda i: (layer, i, 0, 0))
    new_cache = jax.ShapeDtypeStruct((DEC_BATCH, KV_DIM, N_BUF), F32)
    new_cache_spec = pl.BlockSpec((ATTN_SB, KV_DIM, N_BUF), lambda i: (i, 0, 0))
    return pl.pallas_call(
        _sample_attn_body,
        out_shape=(jax.ShapeDtypeStruct((SAMPLE_ROWS, D_MODEL), F32),
                   new_cache, new_cache),
        grid=(DEC_BATCH // ATTN_SB,),
        in_specs=[
            pl.BlockSpec(memory_space=pltpu.SMEM),
            pl.BlockSpec((ATTN_ROWS, POOL_DIM), lambda i: (i, 0)),
            pl.BlockSpec((ATTN_ROWS, Q_DIM), lambda i: (i, 1)),
            pl.BlockSpec((2 * KV_DIM, ATTN_ROWS), lambda i: (0, i)),
            cache_spec,
            cache_spec,
            pl.BlockSpec((ATTN_ROWS, D_MODEL), lambda i: (i, 0)),
            pl.BlockSpec((len(POOL_WINDOWS), POOL_GROUP_DIM, POOL_GROUP_DIM),
                         lambda i: (0, 0, 0)),
            pl.BlockSpec((1, POOL_DIM), lambda i: (0, 0)),
            pl.BlockSpec((D_MODEL, D_MODEL), lambda i: (0, 0),
                         pipeline_mode=pl.Buffered(1)),
        ],
        out_specs=(pl.BlockSpec((ATTN_ROWS, D_MODEL), lambda i: (i, 0)),
                   new_cache_spec, new_cache_spec),
        scratch_shapes=[pltpu.VMEM((ATTN_ROWS, D_MODEL), BF16)],
        compiler_params=_params("parallel"),
        name="sample_attn",
    )(sinks, d_rows, proj, kvt, cache_kt, cache_vt, x, pool_w, pool_scale, w_out)


def kernel(x_prompt, x_sample, cache_k, cache_v, state_pool, norm_ffn1, ffn1_gate,
           ffn1_up, ffn1_down, norm_mix, w_in, pool_w, pool_scale, attn_sinks,
           w_out, norm_ffn2, ffn2_gate, ffn2_up, ffn2_down, final_norm):
    xp = x_prompt.reshape(PROMPT_ROWS, D_MODEL)
    xs = x_sample.reshape(SAMPLE_ROWS, D_MODEL)
    gf = final_norm.reshape(1, D_MODEL)
    state_t = jnp.transpose(state_pool, (0, 2, 1, 3))

    def to_cache_t(c):
        return jnp.transpose(c, (0, 1, 3, 4, 2)).reshape(DEPTH, DEC_BATCH, KV_DIM, N_BUF)

    def from_cache_t(c):
        c = c.reshape(DEPTH, DEC_BATCH, N_KV_HEADS, HEAD_DIM, N_BUF)
        return jnp.transpose(c, (0, 1, 4, 2, 3))

    cache_kt = to_cache_t(cache_k)
    cache_vt = to_cache_t(cache_v)
    keep = min(WINDOW, SEQ)
    assert keep >= POOL_HIST
    k0 = POOL_DIM + Q_DIM

    ffn1_w = (ffn1_gate, ffn1_up, ffn1_down)
    ffn2_w = (ffn2_gate, ffn2_up, ffn2_down)
    w_bf = _cast_ffn(0, *ffn1_w)
    kp_l, vp_l, pp_l, ks_l, vs_l, ps_l = [], [], [], [], [], []
    for l in range(DEPTH):
        last = l == DEPTH - 1
        pool_w_l = pool_w[l].astype(BF16)
        pool_scale_l = pool_scale[l].reshape(1, POOL_DIM)
        sinks_l = attn_sinks[l].astype(F32)

        xp, xs, w_bf, (w_in_l, w_out_l) = _ffn(
            xp, xs, norm_ffn1[l].reshape(1, D_MODEL), *w_bf, gf,
            next_w=(l,) + ffn2_w, mix_w=(l, w_in, w_out))
        proj_p, proj_s, kvt = _inproj(xp, xs, norm_mix[l].reshape(1, D_MODEL), w_in_l)

        tails = [proj_p[(b + 1) * SEQ - keep:(b + 1) * SEQ] for b in range(BATCH)]
        kp_l.append(jnp.stack([t[:, k0:k0 + KV_DIM] for t in tails])
                    .reshape(BATCH, keep, N_KV_HEADS, HEAD_DIM))
        vp_l.append(jnp.stack([t[:, k0 + KV_DIM:] for t in tails])
                    .reshape(BATCH, keep, N_KV_HEADS, HEAD_DIM))
        pp_l.append(jnp.stack([t[keep - POOL_HIST:, :POOL_DIM] for t in tails]))

        d_flat, new_pool = _sample_pool(
            l, state_t, proj_s.reshape(DEC_BATCH, DEC_SEQ * IN_DIM))
        ps_l.append(new_pool)

        xp = _prompt_mix(sinks_l, proj_p, xp, pool_w_l, pool_scale_l, w_out_l)
        xs, new_kt, new_vt = _sample_attn(
            l, sinks_l, d_flat.reshape(SAMPLE_ROWS, POOL_DIM), proj_s, kvt,
            cache_kt, cache_vt, xs, pool_w_l, pool_scale_l, w_out_l)
        ks_l.append(new_kt)
        vs_l.append(new_vt)

        xp, xs, w_bf, _ = _ffn(xp, xs, norm_ffn2[l].reshape(1, D_MODEL), *w_bf, gf,
                               next_w=None if last else (l + 1,) + ffn1_w,
                               final_norm=last)

    y_prompt = xp.reshape(BATCH, SEQ, D_MODEL)
    y_sample = xs.reshape(DEC_BATCH, DEC_SEQ, D_MODEL)
    new_pool_sample = jnp.transpose(jnp.stack(ps_l), (0, 2, 1, 3))
    return (y_prompt, y_sample, jnp.stack(kp_l), jnp.stack(vp_l), jnp.stack(pp_l),
            from_cache_t(jnp.stack(ks_l)), from_cache_t(jnp.stack(vs_l)),
            new_pool_sample)
```

```python
import functools

import jax
import jax.numpy as jnp
from jax import lax
from jax.experimental import pallas as pl
from jax.experimental.pallas import tpu as pltpu

F32 = jnp.float32
BF16 = jnp.bfloat16

D_MODEL = 2048
BATCH = 2
SEQ = 4096
DEPTH = 2
DEC_BATCH = 128
DEC_SEQ = 4
PAST_LEN = 8192

POOL_DIM = 1024
POOL_WINDOWS = (2, 4, 8, 16)
POOL_GROUP_DIM = POOL_DIM // len(POOL_WINDOWS)
POOL_HIST = max(POOL_WINDOWS) - 1
HEAD_DIM = 64
N_HEADS = 16
N_KV_HEADS = 4
GROUP = N_HEADS // N_KV_HEADS
Q_DIM = N_HEADS * HEAD_DIM
KV_DIM = N_KV_HEADS * HEAD_DIM
IN_DIM = POOL_DIM + Q_DIM + 2 * KV_DIM
WINDOW = 128
FFN_DIM = 5632
RMS_EPS = 1e-5
ATTN_SCALE = HEAD_DIM ** -0.5
N_BUF = min(WINDOW, PAST_LEN)
N_HIST_SAMPLE = min(POOL_HIST, PAST_LEN)

PROMPT_ROWS = BATCH * SEQ
SAMPLE_ROWS = DEC_BATCH * DEC_SEQ
M_ROWS = PROMPT_ROWS + SAMPLE_ROWS

VMEM_LIMIT_BYTES = 56 * 1024 * 1024

FFN_TM = 1088
FFN_TF = 512
FFN_TILES = M_ROWS // FFN_TM
FFN_CHUNKS = FFN_DIM // FFN_TF
FFN_FULL_TILES = PROMPT_ROWS // FFN_TM
FFN_SPLIT = PROMPT_ROWS - FFN_FULL_TILES * FFN_TM
CAST_ROWS = D_MODEL // FFN_TILES
WIN_ROWS = 32
WIN_STEPS = D_MODEL // (FFN_TILES * WIN_ROWS)
WOUT_STEPS = D_MODEL // (FFN_TILES * HEAD_DIM)
PROJ_TM = 512
PROJ_PROMPT_TILES = PROMPT_ROWS // PROJ_TM
MIX_TQ = 512
HALO_U = 16
OUT_BLOCKS = 2
ATT_CHUNK = 64
POOL_SB = 32
ATTN_SB = 32
ATTN_ROWS = ATTN_SB * DEC_SEQ

assert FFN_TILES * FFN_TM == M_ROWS and FFN_CHUNKS * FFN_TF == FFN_DIM
assert FFN_FULL_TILES + 1 == FFN_TILES and FFN_SPLIT + SAMPLE_ROWS == FFN_TM
assert FFN_SPLIT % 8 == 0 and CAST_ROWS * FFN_TILES == D_MODEL
assert FFN_CHUNKS >= 2
assert WIN_STEPS * FFN_TILES * WIN_ROWS == D_MODEL and WIN_STEPS <= FFN_CHUNKS
assert WOUT_STEPS * FFN_TILES * HEAD_DIM == D_MODEL and WOUT_STEPS <= FFN_CHUNKS
assert SAMPLE_ROWS == PROJ_TM and PROJ_PROMPT_TILES * PROJ_TM == PROMPT_ROWS


def _rms(x, g):
    return x * lax.rsqrt(jnp.mean(x * x, axis=-1, keepdims=True) + RMS_EPS) * g


def _params(*sem):
    return pltpu.CompilerParams(dimension_semantics=sem,
                                vmem_limit_bytes=VMEM_LIMIT_BYTES)


def _ffn_body(*refs, final_norm, cast_next, prep_mix):
    xp_hbm, xs_hbm, g_ref, wg_ref, wu_ref, wd_ref, gf_ref = refs[:7]
    refs = refs[7:]
    if cast_next:
        ng_ref, nu_ref, nd_ref = refs[:3]
        refs = refs[3:]
    if prep_mix:
        win_ref, wout_ref = refs[:2]
        refs = refs[2:]
    op_hbm, os_hbm = refs[:2]
    refs = refs[2:]
    if cast_next:
        cg_ref, cu_ref, cd_ref = refs[:3]
        refs = refs[3:]
    if prep_mix:
        cwin_ref, cwout_ref = refs[:2]
        refs = refs[2:]
    acc_ref, xn_ref, in_sem, out_sem = refs

    i = pl.program_id(0)
    f = pl.program_id(1)
    slot = i % 2

    def copies(p_hbm, s_hbm, tile, s, sem, mixed, to_vmem):
        if mixed:
            pairs = [(p_hbm.at[pl.ds(FFN_FULL_TILES * FFN_TM, FFN_SPLIT)],
                      acc_ref.at[s, pl.ds(0, FFN_SPLIT)]),
                     (s_hbm, acc_ref.at[s, pl.ds(FFN_SPLIT, SAMPLE_ROWS)])]
        else:
            pairs = [(p_hbm.at[pl.ds(tile * FFN_TM, FFN_TM)], acc_ref.at[s])]
        return [pltpu.make_async_copy(h, v, sem.at[s]) if to_vmem
                else pltpu.make_async_copy(v, h, sem.at[s]) for h, v in pairs]

    def on_tile(tile, s, op, to_vmem):
        p_hbm, s_hbm, sem = ((xp_hbm, xs_hbm, in_sem) if to_vmem
                             else (op_hbm, os_hbm, out_sem))

        @pl.when(tile < FFN_FULL_TILES)
        def _():
            for c in copies(p_hbm, s_hbm, tile, s, sem, False, to_vmem):
                op(c)

        @pl.when(tile == FFN_FULL_TILES)
        def _():
            for c in copies(p_hbm, s_hbm, tile, s, sem, True, to_vmem):
                op(c)

    start = lambda c: c.start()
    wait = lambda c: c.wait()

    @pl.when((i == 0) & (f == 0))
    def _():
        on_tile(i, slot, start, True)

    @pl.when(f == 0)
    def _():
        on_tile(i, slot, wait, True)
        xn_ref[...] = _rms(acc_ref[slot], g_ref[...]).astype(BF16)

    a = xn_ref[...]
    gate = jnp.dot(a, wg_ref[...], preferred_element_type=F32)
    if cast_next:
        cg_ref[...] = ng_ref[...].astype(BF16)
        cu_ref[...] = nu_ref[...].astype(BF16)
    up = jnp.dot(a, wu_ref[...], preferred_element_type=F32)
    if cast_next:
        cd_ref[...] = nd_ref[...].astype(BF16)
    if prep_mix:
        _prep_w_in(win_ref, cwin_ref)
        cwout_ref[...] = wout_ref[...].astype(BF16)
    h = (gate * jax.nn.sigmoid(gate) * up).astype(BF16)
    acc_ref[slot] += 0.5 * jnp.dot(h, wd_ref[...], preferred_element_type=F32)

    @pl.when(f == 1)
    def _():
        @pl.when(i >= 1)
        def _():
            on_tile(i - 1, 1 - slot, wait, False)

        @pl.when(i + 1 < FFN_TILES)
        def _():
            on_tile(i + 1, 1 - slot, start, True)

    @pl.when(f == FFN_CHUNKS - 1)
    def _():
        if final_norm:
            acc_ref[slot] = _rms(acc_ref[slot], gf_ref[...])
        on_tile(i, slot, start, False)

        @pl.when(i == FFN_TILES - 1)
        def _():
            on_tile(i, slot, wait, False)


def _prep_w_in(w_ref, o_ref):
    k0 = POOL_DIM + Q_DIM
    o_ref[:, :POOL_DIM] = w_ref[:, :POOL_DIM].astype(BF16)
    o_ref[:, k0:] = w_ref[:, k0:].astype(BF16)
    lane = lax.broadcasted_iota(jnp.int32, (WIN_ROWS, 2 * HEAD_DIM), 1)
    for g in range(GROUP):
        for kp in range(N_KV_HEADS // 2):
            src = [POOL_DIM + (2 * kp + j) * KV_DIM + (g // 2) * 2 * HEAD_DIM
                   for j in (0, 1)]
            lo, hi = [w_ref[:, c:c + 2 * HEAD_DIM] for c in src]
            if g % 2 == 0:
                hi = pltpu.roll(hi, HEAD_DIM, axis=1)
            else:
                lo = pltpu.roll(lo, HEAD_DIM, axis=1)
            dst = POOL_DIM + g * KV_DIM + kp * 2 * HEAD_DIM
            o_ref[:, dst:dst + 2 * HEAD_DIM] = (
                jnp.where(lane < HEAD_DIM, lo, hi) * ATTN_SCALE).astype(BF16)


def _w_out_src_block(out_blk):
    pool_blocks = POOL_DIM // HEAD_DIM
    h = out_blk - pool_blocks
    src = pool_blocks + (h % N_KV_HEADS) * GROUP + h // N_KV_HEADS
    return jnp.where(out_blk < pool_blocks, out_blk, src)


def _ffn(xp, xs, g, wg, wu, wd, gf, next_w=None, mix_w=None, *, final_norm=False):
    cast_next = next_w is not None
    prep_mix = mix_w is not None
    vec = pl.BlockSpec((1, D_MODEL), lambda i, f: (0, 0))
    in_specs = [
        pl.BlockSpec(memory_space=pl.ANY),
        pl.BlockSpec(memory_space=pl.ANY),
        vec,
        pl.BlockSpec((D_MODEL, FFN_TF), lambda i, f: (0, f)),
        pl.BlockSpec((D_MODEL, FFN_TF), lambda i, f: (0, f)),
        pl.BlockSpec((FFN_TF, D_MODEL), lambda i, f: (f, 0)),
        vec,
    ]
    out_shape = [jax.ShapeDtypeStruct((PROMPT_ROWS, D_MODEL), F32),
                 jax.ShapeDtypeStruct((SAMPLE_ROWS, D_MODEL), F32)]
    out_specs = [pl.BlockSpec(memory_space=pl.ANY), pl.BlockSpec(memory_space=pl.ANY)]
    args = [xp, xs, g, wg, wu, wd, gf]
    if cast_next:
        nl = next_w[0]
        in_specs += [
            pl.BlockSpec((None, CAST_ROWS, FFN_TF), lambda i, f: (nl, i, f)),
            pl.BlockSpec((None, CAST_ROWS, FFN_TF), lambda i, f: (nl, i, f)),
            pl.BlockSpec((None, FFN_TF, CAST_ROWS), lambda i, f: (nl, f, i)),
        ]
        out_shape += [jax.ShapeDtypeStruct((D_MODEL, FFN_DIM), BF16),
                      jax.ShapeDtypeStruct((D_MODEL, FFN_DIM), BF16),
                      jax.ShapeDtypeStruct((FFN_DIM, D_MODEL), BF16)]
        out_specs += [
            pl.BlockSpec((CAST_ROWS, FFN_TF), lambda i, f: (i, f)),
            pl.BlockSpec((CAST_ROWS, FFN_TF), lambda i, f: (i, f)),
            pl.BlockSpec((FFN_TF, CAST_ROWS), lambda i, f: (f, i)),
        ]
        args += list(next_w[1:])
    if prep_mix:
        ml = mix_w[0]
        win_blk = lambda i, f: i * WIN_STEPS + jnp.minimum(f, WIN_STEPS - 1)
        wout_blk = lambda i, f: i * WOUT_STEPS + jnp.minimum(f, WOUT_STEPS - 1)
        in_specs += [
            pl.BlockSpec((None, WIN_ROWS, IN_DIM), lambda i, f: (ml, win_blk(i, f), 0)),
            pl.BlockSpec((None, HEAD_DIM, D_MODEL),
                         lambda i, f: (ml, _w_out_src_block(wout_blk(i, f)), 0)),
        ]
        out_shape += [jax.ShapeDtypeStruct((D_MODEL, IN_DIM), BF16),
                      jax.ShapeDtypeStruct((D_MODEL, D_MODEL), BF16)]
        out_specs += [
            pl.BlockSpec((WIN_ROWS, IN_DIM), lambda i, f: (win_blk(i, f), 0)),
            pl.BlockSpec((HEAD_DIM, D_MODEL), lambda i, f: (wout_blk(i, f), 0)),
        ]
        args += list(mix_w[1:])
    outs = pl.pallas_call(
        functools.partial(_ffn_body, final_norm=final_norm, cast_next=cast_next,
                          prep_mix=prep_mix),
        out_shape=out_shape,
        grid=(FFN_TILES, FFN_CHUNKS),
        in_specs=in_specs,
        out_specs=out_specs,
        scratch_shapes=[pltpu.VMEM((2, FFN_TM, D_MODEL), F32),
                        pltpu.VMEM((FFN_TM, D_MODEL), BF16),
                        pltpu.SemaphoreType.DMA((2,)),
                        pltpu.SemaphoreType.DMA((2,))],
        compiler_params=_params("arbitrary", "arbitrary"),
        name="ffn_final" if final_norm else "ffn",
    )(*args)
    n_cast = 3 if cast_next else 0
    return outs[0], outs[1], tuple(outs[2:2 + n_cast]), tuple(outs[2 + n_cast:])


def _cast_ffn_body(g_ref, u_ref, d_ref, og_ref, ou_ref, od_ref):
    og_ref[...] = g_ref[...].astype(BF16)
    ou_ref[...] = u_ref[...].astype(BF16)
    od_ref[...] = d_ref[...].astype(BF16)


def _cast_ffn(layer, gate, up, down):
    return pl.pallas_call(
        _cast_ffn_body,
        out_shape=(jax.ShapeDtypeStruct((D_MODEL, FFN_DIM), BF16),
                   jax.ShapeDtypeStruct((D_MODEL, FFN_DIM), BF16),
                   jax.ShapeDtypeStruct((FFN_DIM, D_MODEL), BF16)),
        grid=(FFN_CHUNKS,),
        in_specs=[
            pl.BlockSpec((None, D_MODEL, FFN_TF), lambda j: (layer, 0, j)),
            pl.BlockSpec((None, D_MODEL, FFN_TF), lambda j: (layer, 0, j)),
            pl.BlockSpec((None, FFN_TF, D_MODEL), lambda j: (layer, j, 0)),
        ],
        out_specs=(pl.BlockSpec((D_MODEL, FFN_TF), lambda j: (0, j)),
                   pl.BlockSpec((D_MODEL, FFN_TF), lambda j: (0, j)),
                   pl.BlockSpec((FFN_TF, D_MODEL), lambda j: (j, 0))),
        compiler_params=_params("parallel"),
        name="cast_ffn",
    )(gate, up, down)


def _inproj_body(xp_ref, xs_ref, g_ref, w_ref, pp_ref, ps_ref, kvt_ref):
    i = pl.program_id(0)

    @pl.when(i < PROJ_PROMPT_TILES)
    def _():
        xn = _rms(xp_ref[...], g_ref[...]).astype(BF16)
        pp_ref[...] = jnp.dot(xn, w_ref[...], preferred_element_type=F32)

    @pl.when(i == PROJ_PROMPT_TILES)
    def _():
        xn = _rms(xs_ref[...], g_ref[...]).astype(BF16)
        proj = jnp.dot(xn, w_ref[...], preferred_element_type=F32)
        ps_ref[...] = proj
        kvt_ref[...] = proj[:, POOL_DIM + Q_DIM:].T


def _inproj(xp, xs, g, w):
    prompt_tile = lambda i: (jnp.minimum(i, PROJ_PROMPT_TILES - 1), 0)
    fixed = lambda i: (0, 0)
    return pl.pallas_call(
        _inproj_body,
        out_shape=(jax.ShapeDtypeStruct((PROMPT_ROWS, IN_DIM), F32),
                   jax.ShapeDtypeStruct((SAMPLE_ROWS, IN_DIM), F32),
                   jax.ShapeDtypeStruct((2 * KV_DIM, SAMPLE_ROWS), F32)),
        grid=(PROJ_PROMPT_TILES + 1,),
        in_specs=[
            pl.BlockSpec((PROJ_TM, D_MODEL), prompt_tile),
            pl.BlockSpec((PROJ_TM, D_MODEL), fixed),
            pl.BlockSpec((1, D_MODEL), fixed),
            pl.BlockSpec((D_MODEL, IN_DIM), fixed),
        ],
        out_specs=(pl.BlockSpec((PROJ_TM, IN_DIM), prompt_tile),
                   pl.BlockSpec((PROJ_TM, IN_DIM), fixed),
                   pl.BlockSpec((2 * KV_DIM, SAMPLE_ROWS), fixed)),
        compiler_params=_params("arbitrary"),
        name="inproj",
    )(xp, xs, g, w)


def _kv_head_of_lane(shape):
    return lax.broadcasted_iota(jnp.int32, shape, 1) // HEAD_DIM


def _pool_linear(d, gi, pw_ref, ps_ref):
    lanes = slice(gi * POOL_GROUP_DIM, (gi + 1) * POOL_GROUP_DIM)
    y = jnp.dot(d.astype(BF16), pw_ref[gi], preferred_element_type=F32)
    return y * ps_ref[:, lanes]


def _prompt_mix_body(sink_ref, u_ref, q_ref, kv_ref, kvh_ref, uh_ref, x_ref,
                     pw_ref, ps_ref, wo_ref, o_ref, mix_ref, p_ref):
    tiles_per_seq = SEQ // MIX_TQ
    tile = pl.program_id(0) % tiles_per_seq
    first = tile == 0

    lo = jnp.where(first, WINDOW, 0)
    keep_u = jnp.full((HALO_U, POOL_DIM), lo, jnp.int32) == 0
    keep_kv = jnp.full((WINDOW, 2 * KV_DIM), lo, jnp.int32) == 0
    uh = jnp.where(keep_u, uh_ref[...], 0.0)
    pos = tile * MIX_TQ + lax.broadcasted_iota(jnp.int32, (MIX_TQ, 1), 0)
    for gi, w in enumerate(POOL_WINDOWS):
        lanes = slice(gi * POOL_GROUP_DIM, (gi + 1) * POOL_GROUP_DIM)
        xg = jnp.concatenate([uh[:, lanes], u_ref[:, lanes]], axis=0)
        s = xg
        sh = 1
        while sh < w:
            s = s + pltpu.roll(s, sh, axis=0)
            sh *= 2
        cnt = jnp.minimum(w, pos + 1).astype(F32)
        d = s[HALO_U:] / cnt - xg[HALO_U:]
        mix_ref[:, lanes] = _pool_linear(d, gi, pw_ref, ps_ref).astype(BF16)

    kvh = jnp.where(keep_kv, kvh_ref[...], 0.0)
    k_all = jnp.concatenate([kvh[:, :KV_DIM], kv_ref[:, :KV_DIM]], axis=0)
    v_all = jnp.concatenate([kvh[:, KV_DIM:], kv_ref[:, KV_DIM:]], axis=0)
    head = _kv_head_of_lane((2 * WINDOW, KV_DIM))
    qi = lax.broadcasted_iota(jnp.int32, (ATT_CHUNK, 2 * WINDOW), 0)
    kj = lax.broadcasted_iota(jnp.int32, (ATT_CHUNK, 2 * WINDOW), 1)
    for n in range(MIX_TQ // WINDOW):
        kw = k_all[n * WINDOW:(n + 2) * WINDOW]
        vw = v_all[n * WINDOW:(n + 2) * WINDOW]
        qn = q_ref[n * WINDOW:(n + 1) * WINDOW, :].astype(BF16)
        qs = jnp.concatenate(
            [qn[:, g * KV_DIM:(g + 1) * KV_DIM] for g in range(GROUP)], axis=0)
        biases = []
        for c in range(WINDOW // ATT_CHUNK):
            q_pos = qi + c * ATT_CHUNK
            valid = (kj > q_pos) & (kj <= q_pos + WINDOW)
            if n == 0:
                valid = valid & (kj >= lo)
            biases.append(jnp.where(valid, 0.0, -jnp.inf))
        for k in range(N_KV_HEADS):
            kk = jnp.where(head == k, kw, 0.0).astype(BF16)
            s = lax.dot_general(qs, kk, (((1,), (1,)), ((), ())),
                                preferred_element_type=F32)
            for g in range(GROUP):
                sink = sink_ref[k * GROUP + g]
                for c in range(WINDOW // ATT_CHUNK):
                    r0 = g * WINDOW + c * ATT_CHUNK
                    sc = s[r0:r0 + ATT_CHUNK] + biases[c]
                    m = jnp.maximum(jnp.max(sc, axis=-1, keepdims=True), sink)
                    p = jnp.exp(sc - m)
                    den = jnp.sum(p, axis=-1, keepdims=True) + jnp.exp(sink - m)
                    p_ref[r0:r0 + ATT_CHUNK, k * 2 * WINDOW:(k + 1) * 2 * WINDOW] = (
                        p * (1.0 / den)).astype(BF16)
        vcat = jnp.concatenate(
            [jnp.where(head == k, vw, 0.0).astype(BF16)
             for k in range(N_KV_HEADS)], axis=0)
        o = jnp.dot(p_ref[...], vcat, preferred_element_type=F32)
        blk = slice(n * WINDOW, (n + 1) * WINDOW)
        for g in range(GROUP):
            mix_ref[blk, POOL_DIM + g * KV_DIM:POOL_DIM + (g + 1) * KV_DIM] = (
                o[g * WINDOW:(g + 1) * WINDOW].astype(BF16))
        if (n + 1) % OUT_BLOCKS == 0:
            done = slice((n + 1 - OUT_BLOCKS) * WINDOW, (n + 1) * WINDOW)
            o_ref[done, :] = x_ref[done, :] + jnp.dot(mix_ref[done, :], wo_ref[...],
                                                      preferred_element_type=F32)


def _prompt_mix(sinks, proj, x, pool_w, pool_scale, w_out):
    blocks_kv = MIX_TQ // WINDOW
    blocks_u = MIX_TQ // HALO_U
    return pl.pallas_call(
        _prompt_mix_body,
        out_shape=jax.ShapeDtypeStruct((PROMPT_ROWS, D_MODEL), F32),
        grid=(PROMPT_ROWS // MIX_TQ,),
        in_specs=[
            pl.BlockSpec(memory_space=pltpu.SMEM),
            pl.BlockSpec((MIX_TQ, POOL_DIM), lambda i: (i, 0)),
            pl.BlockSpec((MIX_TQ, Q_DIM), lambda i: (i, 1)),
            pl.BlockSpec((MIX_TQ, 2 * KV_DIM), lambda i: (i, 4)),
            pl.BlockSpec((WINDOW, 2 * KV_DIM),
                         lambda i: (jnp.maximum(i * blocks_kv - 1, 0), 4)),
            pl.BlockSpec((HALO_U, POOL_DIM),
                         lambda i: (jnp.maximum(i * blocks_u - 1, 0), 0)),
            pl.BlockSpec((MIX_TQ, D_MODEL), lambda i: (i, 0)),
            pl.BlockSpec((len(POOL_WINDOWS), POOL_GROUP_DIM, POOL_GROUP_DIM),
                         lambda i: (0, 0, 0)),
            pl.BlockSpec((1, POOL_DIM), lambda i: (0, 0)),
            pl.BlockSpec((D_MODEL, D_MODEL), lambda i: (0, 0)),
        ],
        out_specs=pl.BlockSpec((MIX_TQ, D_MODEL), lambda i: (i, 0)),
        scratch_shapes=[pltpu.VMEM((MIX_TQ, D_MODEL), BF16),
                        pltpu.VMEM((GROUP * WINDOW, N_KV_HEADS * 2 * WINDOW), BF16)],
        compiler_params=_params("parallel"),
        name="prompt_mix",
    )(sinks, proj, proj, proj, proj, proj, x, pool_w, pool_scale, w_out)


def _sample_pool_body(hist_ref, proj_ref, d_ref, newpool_ref):
    for gi, w in enumerate(POOL_WINDOWS):
        base = gi * POOL_GROUP_DIM
        seq = [hist_ref[j, :, base:base + POOL_GROUP_DIM] for j in range(POOL_HIST)]
        seq += [proj_ref[:, t * IN_DIM + base:t * IN_DIM + base + POOL_GROUP_DIM]
                for t in range(DEC_SEQ)]
        for t in range(DEC_SEQ):
            last = POOL_HIST + t
            s = seq[last - w + 1]
            for j in range(last - w + 2, last + 1):
                s = s + seq[j]
            cnt = float(min(w, N_HIST_SAMPLE + t + 1))
            d_ref[:, t * POOL_DIM + base:t * POOL_DIM + base + POOL_GROUP_DIM] = (
                s / cnt - seq[last])
    for j in range(POOL_HIST - DEC_SEQ):
        newpool_ref[j] = hist_ref[j + DEC_SEQ]
    for t in range(DEC_SEQ):
        newpool_ref[POOL_HIST - DEC_SEQ + t] = proj_ref[:, t * IN_DIM:t * IN_DIM + POOL_DIM]


def _sample_pool(layer, state_t, proj_flat):
    return pl.pallas_call(
        _sample_pool_body,
        out_shape=(jax.ShapeDtypeStruct((DEC_BATCH, DEC_SEQ * POOL_DIM), F32),
                   jax.ShapeDtypeStruct((POOL_HIST, DEC_BATCH, POOL_DIM), F32)),
        grid=(DEC_BATCH // POOL_SB,),
        in_specs=[
            pl.BlockSpec((None, POOL_HIST, POOL_SB, POOL_DIM), lambda i: (layer, 0, i, 0)),
            pl.BlockSpec((POOL_SB, DEC_SEQ * IN_DIM), lambda i: (i, 0)),
        ],
        out_specs=(pl.BlockSpec((POOL_SB, DEC_SEQ * POOL_DIM), lambda i: (i, 0)),
                   pl.BlockSpec((POOL_HIST, POOL_SB, POOL_DIM), lambda i: (0, i, 0))),
        compiler_params=_params("parallel"),
        name="sample_pool",
    )(state_t, proj_flat)


def _sample_attn_body(*refs, update_cache):
    if update_cache:
        (sink_ref, d_ref, q_ref, kvt_ref, ck_ref, cv_ref, x_ref, pw_ref, ps_ref, wo_ref,
         o_ref, nk_ref, nv_ref, mix_ref) = refs
    else:
        (sink_ref, d_ref, q_ref, kvt_ref, ck_ref, cv_ref, x_ref, pw_ref, ps_ref, wo_ref,
         o_ref, mix_ref) = refs

    lane = lax.broadcasted_iota(jnp.int32, (KV_DIM, N_BUF), 1)
    is_old = lane < N_BUF - DEC_SEQ

    def shift_in(b, kt, vt, kt_new, vt_new):
        shift_new = (N_BUF - DEC_SEQ - b * DEC_SEQ) % N_BUF
        nk_ref[b] = jnp.where(is_old, pltpu.roll(kt, N_BUF - DEC_SEQ, axis=1),
                              pltpu.roll(kt_new, shift_new, axis=1))
        nv_ref[b] = jnp.where(is_old, pltpu.roll(vt, N_BUF - DEC_SEQ, axis=1),
                              pltpu.roll(vt_new, shift_new, axis=1))

    def attend():
        _sample_attend(sink_ref, d_ref, q_ref, kvt_ref, ck_ref, cv_ref, x_ref, pw_ref,
                       ps_ref, wo_ref, o_ref, mix_ref,
                       shift_in if update_cache else None)

    if not update_cache:
        attend()
        return

    phase = pl.program_id(0)

    @pl.when(phase < DEPTH - 1)
    def _():
        kt_new = kvt_ref[:KV_DIM, :]
        vt_new = kvt_ref[KV_DIM:, :]
        for b in range(ATTN_SB):
            shift_in(b, ck_ref[b], cv_ref[b], kt_new, vt_new)

    @pl.when(phase == DEPTH - 1)
    def _():
        attend()


def _sample_attend(sink_ref, d_ref, q_ref, kvt_ref, ck_ref, cv_ref, x_ref, pw_ref, ps_ref,
                   wo_ref, o_ref, mix_ref, shift_in):
    for gi in range(len(POOL_WINDOWS)):
        lanes = slice(gi * POOL_GROUP_DIM, (gi + 1) * POOL_GROUP_DIM)
        mix_ref[:, lanes] = _pool_linear(d_ref[:, lanes], gi, pw_ref, ps_ref).astype(BF16)

    pair = 2 * DEC_SEQ
    grp = GROUP * pair
    rows = N_KV_HEADS * grp
    new_cols = ATTN_SB * DEC_SEQ
    keys = N_BUF + new_cols
    assert keys == KV_DIM and new_cols == N_BUF
    head = _kv_head_of_lane((rows, KV_DIM))
    row = lax.broadcasted_iota(jnp.int32, (rows, keys), 0)
    row_head = row // grp
    in_pair = row % pair
    low = in_pair < DEC_SEQ
    tok = in_pair % DEC_SEQ
    col = lax.broadcasted_iota(jnp.int32, (rows, keys), 1)
    new_col = col - N_BUF
    valid_cache = (col < N_BUF) & (col > tok)
    causal_new = (new_col >= 0) & ((new_col % DEC_SEQ) <= tok)
    seq_of_col = new_col // DEC_SEQ - in_pair // DEC_SEQ
    sink = jnp.concatenate(
        [jnp.full((pair, 1), sink_ref[k * GROUP + g], F32)
         for k in range(N_KV_HEADS) for g in range(GROUP)], axis=0)
    kt_new = kvt_ref[:KV_DIM, :]
    vt_new = kvt_ref[KV_DIM:, :]
    kt_new_bf = kt_new.astype(BF16)
    vt_new_bf = vt_new.astype(BF16)
    nt = (((1,), (1,)), ((), ()))

    for j in range(ATTN_SB // 2):
        r0 = j * pair
        q8 = q_ref[r0:r0 + pair, :]
        qs = jnp.concatenate(
            [q8[:, g * KV_DIM:(g + 1) * KV_DIM] for g in range(GROUP)], axis=0)
        qrep = jnp.concatenate([qs] * N_KV_HEADS, axis=0)
        lhs = jnp.where(head == row_head, qrep, 0.0).astype(BF16)
        scores, values = [], []
        for b in (2 * j, 2 * j + 1):
            kt = ck_ref[b]
            vt = cv_ref[b]
            scores.append(jnp.dot(lhs, kt.astype(BF16), preferred_element_type=F32))
            values.append(vt.astype(BF16))
            if shift_in is not None:
                shift_in(b, kt, vt, kt_new, vt_new)
        s_new = jnp.dot(lhs, kt_new_bf, preferred_element_type=F32)
        s = jnp.concatenate([jnp.where(low[:, :N_BUF], scores[0], scores[1]), s_new],
                            axis=1)
        valid = valid_cache | (causal_new & (seq_of_col == 2 * j))
        s = jnp.where(valid, s, -jnp.inf)
        m = jnp.maximum(jnp.max(s, axis=-1, keepdims=True), sink)
        p = jnp.exp(s - m)
        den = jnp.sum(p, axis=-1, keepdims=True) + jnp.exp(sink - m)
        p = (p / den).astype(BF16)
        p_old = p[:, :N_BUF]
        o = jnp.where(low,
                      lax.dot_general(p_old, values[0], nt, preferred_element_type=F32),
                      lax.dot_general(p_old, values[1], nt, preferred_element_type=F32))
        o = o + lax.dot_general(p[:, N_BUF:], vt_new_bf, nt, preferred_element_type=F32)
        o = jnp.where(head == row_head, o, 0.0)
        og = o[0:grp]
        for k in range(1, N_KV_HEADS):
            og = og + o[k * grp:(k + 1) * grp]
        for g in range(GROUP):
            mix_ref[r0:r0 + pair,
                    POOL_DIM + g * KV_DIM:POOL_DIM + (g + 1) * KV_DIM] = (
                        og[g * pair:(g + 1) * pair].astype(BF16))

    o_ref[...] = x_ref[...] + jnp.dot(mix_ref[...], wo_ref[...],
                                      preferred_element_type=F32)


def _sample_attn(layer, sinks, d_rows, proj, kvts, cache_kt, cache_vt, x, pool_w,
                 pool_scale, w_out):
    phases = kvts.shape[0]
    update_cache = phases > 1
    assert phases == 1 or (phases == DEPTH and layer == DEPTH - 1)
    layer_of = (lambda p: p) if update_cache else (lambda p: layer)
    row = lambda p, i: jnp.where(p == phases - 1, i, 0)
    cache_spec = pl.BlockSpec((None, ATTN_SB, KV_DIM, N_BUF),
                              lambda p, i: (layer_of(p), i, 0, 0))
    out_shape = [jax.ShapeDtypeStruct((SAMPLE_ROWS, D_MODEL), F32)]
    out_specs = [pl.BlockSpec((ATTN_ROWS, D_MODEL), lambda p, i: (row(p, i), 0))]
    if update_cache:
        out_shape += [jax.ShapeDtypeStruct((DEPTH, DEC_BATCH, KV_DIM, N_BUF), F32)] * 2
        out_specs += [pl.BlockSpec((None, ATTN_SB, KV_DIM, N_BUF),
                                   lambda p, i: (p, i, 0, 0))] * 2
    return pl.pallas_call(
        functools.partial(_sample_attn_body, update_cache=update_cache),
        out_shape=out_shape,
        grid=(phases, DEC_BATCH // ATTN_SB),
        in_specs=[
            pl.BlockSpec(memory_space=pltpu.SMEM),
            pl.BlockSpec((ATTN_ROWS, POOL_DIM), lambda p, i: (row(p, i), 0)),
            pl.BlockSpec((ATTN_ROWS, Q_DIM), lambda p, i: (row(p, i), 1)),
            pl.BlockSpec((None, 2 * KV_DIM, ATTN_ROWS), lambda p, i: (p, 0, i)),
            cache_spec,
            cache_spec,
            pl.BlockSpec((ATTN_ROWS, D_MODEL), lambda p, i: (row(p, i), 0)),
            pl.BlockSpec((len(POOL_WINDOWS), POOL_GROUP_DIM, POOL_GROUP_DIM),
                         lambda p, i: (0, 0, 0)),
            pl.BlockSpec((1, POOL_DIM), lambda p, i: (0, 0)),
            pl.BlockSpec((D_MODEL, D_MODEL), lambda p, i: (0, 0),
                         pipeline_mode=pl.Buffered(1)),
        ],
        out_specs=out_specs,
        scratch_shapes=[pltpu.VMEM((ATTN_ROWS, D_MODEL), BF16)],
        compiler_params=_params("arbitrary", "arbitrary"),
        name="sample_attn",
    )(sinks, d_rows, proj, kvts, cache_kt, cache_vt, x, pool_w, pool_scale, w_out)


def kernel(x_prompt, x_sample, cache_k, cache_v, state_pool, norm_ffn1, ffn1_gate,
           ffn1_up, ffn1_down, norm_mix, w_in, pool_w, pool_scale, attn_sinks,
           w_out, norm_ffn2, ffn2_gate, ffn2_up, ffn2_down, final_norm):
    xp = x_prompt.reshape(PROMPT_ROWS, D_MODEL)
    xs = x_sample.reshape(SAMPLE_ROWS, D_MODEL)
    gf = final_norm.reshape(1, D_MODEL)
    state_t = jnp.transpose(state_pool, (0, 2, 1, 3))

    def to_cache_t(c):
        return jnp.transpose(c, (0, 1, 3, 4, 2)).reshape(DEPTH, DEC_BATCH, KV_DIM, N_BUF)

    def from_cache_t(c):
        c = c.reshape(DEPTH, DEC_BATCH, N_KV_HEADS, HEAD_DIM, N_BUF)
        return jnp.transpose(c, (0, 1, 4, 2, 3))

    cache_kt = to_cache_t(cache_k)
    cache_vt = to_cache_t(cache_v)
    keep = min(WINDOW, SEQ)
    assert keep >= POOL_HIST
    k0 = POOL_DIM + Q_DIM

    ffn1_w = (ffn1_gate, ffn1_up, ffn1_down)
    ffn2_w = (ffn2_gate, ffn2_up, ffn2_down)
    w_bf = _cast_ffn(0, *ffn1_w)
    kp_l, vp_l, pp_l, ps_l, kvt_l = [], [], [], [], []
    for l in range(DEPTH):
        last = l == DEPTH - 1
        pool_w_l = pool_w[l].astype(BF16)
        pool_scale_l = pool_scale[l].reshape(1, POOL_DIM)
        sinks_l = attn_sinks[l].astype(F32)

        xp, xs, w_bf, (w_in_l, w_out_l) = _ffn(
            xp, xs, norm_ffn1[l].reshape(1, D_MODEL), *w_bf, gf,
            next_w=(l,) + ffn2_w, mix_w=(l, w_in, w_out))
        proj_p, proj_s, kvt = _inproj(xp, xs, norm_mix[l].reshape(1, D_MODEL), w_in_l)

        tails = [proj_p[(b + 1) * SEQ - keep:(b + 1) * SEQ] for b in range(BATCH)]
        kp_l.append(jnp.stack([t[:, k0:k0 + KV_DIM] for t in tails])
                    .reshape(BATCH, keep, N_KV_HEADS, HEAD_DIM))
        vp_l.append(jnp.stack([t[:, k0 + KV_DIM:] for t in tails])
                    .reshape(BATCH, keep, N_KV_HEADS, HEAD_DIM))
        pp_l.append(jnp.stack([t[keep - POOL_HIST:, :POOL_DIM] for t in tails]))

        d_flat, new_pool = _sample_pool(
            l, state_t, proj_s.reshape(DEC_BATCH, DEC_SEQ * IN_DIM))
        ps_l.append(new_pool)

        xp = _prompt_mix(sinks_l, proj_p, xp, pool_w_l, pool_scale_l, w_out_l)
        kvt_l.append(kvt)
        outs = _sample_attn(
            l, sinks_l, d_flat.reshape(SAMPLE_ROWS, POOL_DIM), proj_s,
            jnp.stack(kvt_l) if last else kvt[None],
            cache_kt, cache_vt, xs, pool_w_l, pool_scale_l, w_out_l)
        xs = outs[0]
        if last:
            new_kt, new_vt = outs[1:]

        xp, xs, w_bf, _ = _ffn(xp, xs, norm_ffn2[l].reshape(1, D_MODEL), *w_bf, gf,
                               next_w=None if last else (l + 1,) + ffn1_w,
                               final_norm=last)

    y_prompt = xp.reshape(BATCH, SEQ, D_MODEL)
    y_sample = xs.reshape(DEC_BATCH, DEC_SEQ, D_MODEL)
    new_pool_sample = jnp.transpose(jnp.stack(ps_l), (0, 2, 1, 3))
    return (y_prompt, y_sample, jnp.stack(kp_l), jnp.stack(vp_l), jnp.stack(pp_l),
            from_cache_t(new_kt), from_cache_t(new_vt), new_pool_sample)
```

```python
import functools

import jax
import jax.numpy as jnp
from jax import lax
from jax.experimental import pallas as pl
from jax.experimental.pallas import tpu as pltpu

F32 = jnp.float32
BF16 = jnp.bfloat16

D_MODEL = 2048
BATCH = 2
SEQ = 4096
DEPTH = 2
DEC_BATCH = 128
DEC_SEQ = 4
PAST_LEN = 8192

POOL_DIM = 1024
POOL_WINDOWS = (2, 4, 8, 16)
POOL_GROUP_DIM = POOL_DIM // len(POOL_WINDOWS)
POOL_HIST = max(POOL_WINDOWS) - 1
HEAD_DIM = 64
N_HEADS = 16
N_KV_HEADS = 4
GROUP = N_HEADS // N_KV_HEADS
Q_DIM = N_HEADS * HEAD_DIM
KV_DIM = N_KV_HEADS * HEAD_DIM
IN_DIM = POOL_DIM + Q_DIM + 2 * KV_DIM
WINDOW = 128
FFN_DIM = 5632
RMS_EPS = 1e-5
ATTN_SCALE = HEAD_DIM ** -0.5
N_BUF = min(WINDOW, PAST_LEN)
N_HIST_SAMPLE = min(POOL_HIST, PAST_LEN)

PROMPT_ROWS = BATCH * SEQ
SAMPLE_ROWS = DEC_BATCH * DEC_SEQ
M_ROWS = PROMPT_ROWS + SAMPLE_ROWS

VMEM_LIMIT_BYTES = 56 * 1024 * 1024

FFN_TM = 1088
FFN_TF = 512
FFN_TILES = M_ROWS // FFN_TM
FFN_CHUNKS = FFN_DIM // FFN_TF
FFN_FULL_TILES = PROMPT_ROWS // FFN_TM
FFN_SPLIT = PROMPT_ROWS - FFN_FULL_TILES * FFN_TM
FFN_TF_HEAD = 256
WIN_ROWS = 32
WIN_BLOCKS = D_MODEL // WIN_ROWS
WOUT_BLOCKS = D_MODEL // HEAD_DIM
PROJ_TM = 512
PROJ_PROMPT_TILES = PROMPT_ROWS // PROJ_TM
MIX_TQ = 512
HALO_U = 16
OUT_BLOCKS = 2
ATT_CHUNK = 64
POOL_SB = 32
ATTN_SB = 32
ATTN_ROWS = ATTN_SB * DEC_SEQ

assert FFN_TILES * FFN_TM == M_ROWS and FFN_CHUNKS * FFN_TF == FFN_DIM
assert FFN_FULL_TILES + 1 == FFN_TILES and FFN_SPLIT + SAMPLE_ROWS == FFN_TM
assert FFN_SPLIT % 8 == 0 and FFN_FULL_TILES >= 1
assert FFN_CHUNKS >= 2
assert FFN_DIM % FFN_TF_HEAD == 0
assert SAMPLE_ROWS == PROJ_TM and PROJ_PROMPT_TILES * PROJ_TM == PROMPT_ROWS


def _rms(x, g):
    return x * lax.rsqrt(jnp.mean(x * x, axis=-1, keepdims=True) + RMS_EPS) * g


def _params(*sem):
    return pltpu.CompilerParams(dimension_semantics=sem,
                                vmem_limit_bytes=VMEM_LIMIT_BYTES)


def _ffn_body(*refs, final_norm, cast_next, prep_mix, t0, n_tiles, head_in):
    xp_hbm, xs_hbm, g_ref, wg_ref, wu_ref, wd_ref, gf_ref = refs[:7]
    refs = refs[7:]
    if cast_next:
        ng_ref, nu_ref, nd_ref = refs[:3]
        refs = refs[3:]
    if prep_mix:
        win_ref, wout_ref = refs[:2]
        refs = refs[2:]
    if head_in:
        head_hbm = refs[0]
        refs = refs[1:]
    op_hbm, os_hbm = refs[:2]
    refs = refs[2:]
    if cast_next:
        cg_ref, cu_ref, cd_ref = refs[:3]
        refs = refs[3:]
    if prep_mix:
        cwin_ref, cwout_ref = refs[:2]
        refs = refs[2:]
    acc_ref, xn_ref, in_sem, out_sem = refs[:4]
    head_sem = refs[4] if head_in else None

    i = pl.program_id(0)
    f = pl.program_id(1)
    slot = i % 2
    tile = i + t0

    def head_copy():
        return pltpu.make_async_copy(head_hbm, op_hbm.at[pl.ds(0, t0 * FFN_TM)], head_sem)

    def copies(p_hbm, s_hbm, tile, s, sem, mixed, to_vmem):
        if mixed:
            pairs = [(p_hbm.at[pl.ds(FFN_FULL_TILES * FFN_TM, FFN_SPLIT)],
                      acc_ref.at[s, pl.ds(0, FFN_SPLIT)]),
                     (s_hbm, acc_ref.at[s, pl.ds(FFN_SPLIT, SAMPLE_ROWS)])]
        else:
            pairs = [(p_hbm.at[pl.ds(tile * FFN_TM, FFN_TM)], acc_ref.at[s])]
        return [pltpu.make_async_copy(h, v, sem.at[s]) if to_vmem
                else pltpu.make_async_copy(v, h, sem.at[s]) for h, v in pairs]

    def on_tile(tile, s, op, to_vmem):
        p_hbm, s_hbm, sem = ((xp_hbm, xs_hbm, in_sem) if to_vmem
                             else (op_hbm, os_hbm, out_sem))

        @pl.when(tile < FFN_FULL_TILES)
        def _():
            for c in copies(p_hbm, s_hbm, tile, s, sem, False, to_vmem):
                op(c)

        @pl.when(tile == FFN_FULL_TILES)
        def _():
            for c in copies(p_hbm, s_hbm, tile, s, sem, True, to_vmem):
                op(c)

    start = lambda c: c.start()
    wait = lambda c: c.wait()

    @pl.when((i == 0) & (f == 0))
    def _():
        on_tile(tile, slot, start, True)
        if head_in:
            head_copy().start()

    @pl.when(f == 0)
    def _():
        on_tile(tile, slot, wait, True)
        xn_ref[...] = _rms(acc_ref[slot], g_ref[...]).astype(BF16)

    a = xn_ref[...]
    gate = jnp.dot(a, wg_ref[...], preferred_element_type=F32)
    if cast_next:
        cg_ref[...] = ng_ref[...].astype(BF16)
        cu_ref[...] = nu_ref[...].astype(BF16)
    up = jnp.dot(a, wu_ref[...], preferred_element_type=F32)
    if cast_next:
        cd_ref[...] = nd_ref[...].astype(BF16)
    if prep_mix:
        _prep_w_in(win_ref, cwin_ref)
        cwout_ref[...] = wout_ref[...].astype(BF16)
    h = (gate * jax.nn.sigmoid(gate) * up).astype(BF16)
    acc_ref[slot] += 0.5 * jnp.dot(h, wd_ref[...], preferred_element_type=F32)

    @pl.when(f == 1)
    def _():
        @pl.when(i >= 1)
        def _():
            on_tile(tile - 1, 1 - slot, wait, False)

        @pl.when(i + 1 < n_tiles)
        def _():
            on_tile(tile + 1, 1 - slot, start, True)

    @pl.when(f == FFN_CHUNKS - 1)
    def _():
        if final_norm:
            acc_ref[slot] = _rms(acc_ref[slot], gf_ref[...])
        on_tile(tile, slot, start, False)

        @pl.when(i == n_tiles - 1)
        def _():
            on_tile(tile, slot, wait, False)
            if head_in:
                head_copy().wait()


def _prep_w_in(w_ref, o_ref):
    k0 = POOL_DIM + Q_DIM
    o_ref[:, :POOL_DIM] = w_ref[:, :POOL_DIM].astype(BF16)
    o_ref[:, k0:] = w_ref[:, k0:].astype(BF16)
    lane = lax.broadcasted_iota(jnp.int32, (WIN_ROWS, 2 * HEAD_DIM), 1)
    for g in range(GROUP):
        for kp in range(N_KV_HEADS // 2):
            src = [POOL_DIM + (2 * kp + j) * KV_DIM + (g // 2) * 2 * HEAD_DIM
                   for j in (0, 1)]
            lo, hi = [w_ref[:, c:c + 2 * HEAD_DIM] for c in src]
            if g % 2 == 0:
                hi = pltpu.roll(hi, HEAD_DIM, axis=1)
            else:
                lo = pltpu.roll(lo, HEAD_DIM, axis=1)
            dst = POOL_DIM + g * KV_DIM + kp * 2 * HEAD_DIM
            o_ref[:, dst:dst + 2 * HEAD_DIM] = (
                jnp.where(lane < HEAD_DIM, lo, hi) * ATTN_SCALE).astype(BF16)


def _w_out_src_block(out_blk):
    pool_blocks = POOL_DIM // HEAD_DIM
    h = out_blk - pool_blocks
    src = pool_blocks + (h % N_KV_HEADS) * GROUP + h // N_KV_HEADS
    return jnp.where(out_blk < pool_blocks, out_blk, src)


def _ffn(xp, xs, g, wg, wu, wd, gf, next_w=None, mix_w=None, *, final_norm=False,
         head=None):
    cast_next = next_w is not None
    prep_mix = mix_w is not None
    head_in = head is not None
    t0 = head.shape[0] // FFN_TM if head_in else 0
    n_tiles = FFN_TILES - t0
    cast_blocks = 1 << (n_tiles.bit_length() - 1)
    cast_rows = D_MODEL // cast_blocks
    win_steps = pl.cdiv(WIN_BLOCKS, n_tiles)
    wout_steps = pl.cdiv(WOUT_BLOCKS, n_tiles)
    assert max(win_steps, wout_steps) <= FFN_CHUNKS
    cast_blk = lambda i: jnp.minimum(i, cast_blocks - 1)
    cast_chunk = lambda i, f: jnp.where(i < cast_blocks, f, FFN_CHUNKS - 1)
    vec = pl.BlockSpec((1, D_MODEL), lambda i, f: (0, 0))
    in_specs = [
        pl.BlockSpec(memory_space=pl.ANY),
        pl.BlockSpec(memory_space=pl.ANY),
        vec,
        pl.BlockSpec((D_MODEL, FFN_TF), lambda i, f: (0, f)),
        pl.BlockSpec((D_MODEL, FFN_TF), lambda i, f: (0, f)),
        pl.BlockSpec((FFN_TF, D_MODEL), lambda i, f: (f, 0)),
        vec,
    ]
    out_shape = [jax.ShapeDtypeStruct((PROMPT_ROWS, D_MODEL), F32),
                 jax.ShapeDtypeStruct((SAMPLE_ROWS, D_MODEL), F32)]
    out_specs = [pl.BlockSpec(memory_space=pl.ANY), pl.BlockSpec(memory_space=pl.ANY)]
    args = [xp, xs, g, wg, wu, wd, gf]
    if cast_next:
        nl = next_w[0]
        in_specs += [
            pl.BlockSpec((None, cast_rows, FFN_TF),
                         lambda i, f: (nl, cast_blk(i), cast_chunk(i, f))),
            pl.BlockSpec((None, cast_rows, FFN_TF),
                         lambda i, f: (nl, cast_blk(i), cast_chunk(i, f))),
            pl.BlockSpec((None, FFN_TF, cast_rows),
                         lambda i, f: (nl, cast_chunk(i, f), cast_blk(i))),
        ]
        out_shape += [jax.ShapeDtypeStruct((D_MODEL, FFN_DIM), BF16),
                      jax.ShapeDtypeStruct((D_MODEL, FFN_DIM), BF16),
                      jax.ShapeDtypeStruct((FFN_DIM, D_MODEL), BF16)]
        out_specs += [
            pl.BlockSpec((cast_rows, FFN_TF), lambda i, f: (cast_blk(i), cast_chunk(i, f))),
            pl.BlockSpec((cast_rows, FFN_TF), lambda i, f: (cast_blk(i), cast_chunk(i, f))),
            pl.BlockSpec((FFN_TF, cast_rows), lambda i, f: (cast_chunk(i, f), cast_blk(i))),
        ]
        args += list(next_w[1:])
    if prep_mix:
        ml = mix_w[0]
        win_blk = lambda i, f: jnp.minimum(
            i * win_steps + jnp.minimum(f, win_steps - 1), WIN_BLOCKS - 1)
        wout_blk = lambda i, f: jnp.minimum(
            i * wout_steps + jnp.minimum(f, wout_steps - 1), WOUT_BLOCKS - 1)
        in_specs += [
            pl.BlockSpec((None, WIN_ROWS, IN_DIM), lambda i, f: (ml, win_blk(i, f), 0)),
            pl.BlockSpec((None, HEAD_DIM, D_MODEL),
                         lambda i, f: (ml, _w_out_src_block(wout_blk(i, f)), 0)),
        ]
        out_shape += [jax.ShapeDtypeStruct((D_MODEL, IN_DIM), BF16),
                      jax.ShapeDtypeStruct((D_MODEL, D_MODEL), BF16)]
        out_specs += [
            pl.BlockSpec((WIN_ROWS, IN_DIM), lambda i, f: (win_blk(i, f), 0)),
            pl.BlockSpec((HEAD_DIM, D_MODEL), lambda i, f: (wout_blk(i, f), 0)),
        ]
        args += list(mix_w[1:])
    scratch = [pltpu.VMEM((2, FFN_TM, D_MODEL), F32),
               pltpu.VMEM((FFN_TM, D_MODEL), BF16),
               pltpu.SemaphoreType.DMA((2,)),
               pltpu.SemaphoreType.DMA((2,))]
    if head_in:
        in_specs.append(pl.BlockSpec(memory_space=pl.ANY))
        args.append(head)
        scratch.append(pltpu.SemaphoreType.DMA(()))
    outs = pl.pallas_call(
        functools.partial(_ffn_body, final_norm=final_norm, cast_next=cast_next,
                          prep_mix=prep_mix, t0=t0, n_tiles=n_tiles, head_in=head_in),
        out_shape=out_shape,
        grid=(n_tiles, FFN_CHUNKS),
        in_specs=in_specs,
        out_specs=out_specs,
        scratch_shapes=scratch,
        compiler_params=_params("arbitrary", "arbitrary"),
        name="ffn_final" if final_norm else "ffn",
    )(*args)
    n_cast = 3 if cast_next else 0
    return outs[0], outs[1], tuple(outs[2:2 + n_cast]), tuple(outs[2 + n_cast:])


def _ffn_head_body(x_ref, g_ref, wg_ref, wu_ref, wd_ref, o_ref, og_ref, ou_ref, od_ref,
                   xn_ref, wgs_ref, wus_ref, wds_ref):
    step = pl.program_id(0)
    slot = step % 2

    @pl.when(step == 0)
    def _():
        x = x_ref[...]
        xn_ref[...] = _rms(x, g_ref[...]).astype(BF16)
        o_ref[...] = x
        wgs_ref[1] = jnp.zeros(wgs_ref.shape[1:], BF16)
        wus_ref[1] = jnp.zeros(wus_ref.shape[1:], BF16)
        wds_ref[1] = jnp.zeros(wds_ref.shape[1:], BF16)

    a = xn_ref[...]
    wg = wg_ref[...].astype(BF16)
    wgs_ref[slot] = wg
    og_ref[...] = wg
    gate = jnp.dot(a, wgs_ref[1 - slot], preferred_element_type=F32)
    wu = wu_ref[...].astype(BF16)
    wus_ref[slot] = wu
    ou_ref[...] = wu
    up = jnp.dot(a, wus_ref[1 - slot], preferred_element_type=F32)
    wd = wd_ref[...].astype(BF16)
    wds_ref[slot] = wd
    od_ref[...] = wd
    h = (gate * jax.nn.sigmoid(gate) * up).astype(BF16)
    o_ref[...] += 0.5 * jnp.dot(h, wds_ref[1 - slot], preferred_element_type=F32)


def _ffn_head(layer, xp, g, gate, up, down):
    tf = FFN_TF_HEAD
    n_chunks = FFN_DIM // tf
    fixed = lambda s: (0, 0)
    chunk = lambda s: jnp.minimum(s, n_chunks - 1)
    return pl.pallas_call(
        _ffn_head_body,
        out_shape=(jax.ShapeDtypeStruct((FFN_TM, D_MODEL), F32),
                   jax.ShapeDtypeStruct((D_MODEL, FFN_DIM), BF16),
                   jax.ShapeDtypeStruct((D_MODEL, FFN_DIM), BF16),
                   jax.ShapeDtypeStruct((FFN_DIM, D_MODEL), BF16)),
        grid=(n_chunks + 1,),
        in_specs=[
            pl.BlockSpec((FFN_TM, D_MODEL), fixed, pipeline_mode=pl.Buffered(1)),
            pl.BlockSpec((1, D_MODEL), fixed),
            pl.BlockSpec((None, D_MODEL, tf), lambda s: (layer, 0, chunk(s))),
            pl.BlockSpec((None, D_MODEL, tf), lambda s: (layer, 0, chunk(s))),
            pl.BlockSpec((None, tf, D_MODEL), lambda s: (layer, chunk(s), 0)),
        ],
        out_specs=(pl.BlockSpec((FFN_TM, D_MODEL), fixed),
                   pl.BlockSpec((D_MODEL, tf), lambda s: (0, chunk(s))),
                   pl.BlockSpec((D_MODEL, tf), lambda s: (0, chunk(s))),
                   pl.BlockSpec((tf, D_MODEL), lambda s: (chunk(s), 0))),
        scratch_shapes=[pltpu.VMEM((FFN_TM, D_MODEL), BF16),
                        pltpu.VMEM((2, D_MODEL, tf), BF16),
                        pltpu.VMEM((2, D_MODEL, tf), BF16),
                        pltpu.VMEM((2, tf, D_MODEL), BF16)],
        compiler_params=_params("arbitrary"),
        name="ffn_head",
    )(xp, g, gate, up, down)


def _inproj_body(xp_ref, xs_ref, g_ref, w_ref, pp_ref, ps_ref, kvt_ref):
    i = pl.program_id(0)

    @pl.when(i < PROJ_PROMPT_TILES)
    def _():
        xn = _rms(xp_ref[...], g_ref[...]).astype(BF16)
        pp_ref[...] = jnp.dot(xn, w_ref[...], preferred_element_type=F32)

    @pl.when(i == PROJ_PROMPT_TILES)
    def _():
        xn = _rms(xs_ref[...], g_ref[...]).astype(BF16)
        proj = jnp.dot(xn, w_ref[...], preferred_element_type=F32)
        ps_ref[...] = proj
        kvt_ref[...] = proj[:, POOL_DIM + Q_DIM:].T


def _inproj(xp, xs, g, w):
    prompt_tile = lambda i: (jnp.minimum(i, PROJ_PROMPT_TILES - 1), 0)
    fixed = lambda i: (0, 0)
    return pl.pallas_call(
        _inproj_body,
        out_shape=(jax.ShapeDtypeStruct((PROMPT_ROWS, IN_DIM), F32),
                   jax.ShapeDtypeStruct((SAMPLE_ROWS, IN_DIM), F32),
                   jax.ShapeDtypeStruct((2 * KV_DIM, SAMPLE_ROWS), F32)),
        grid=(PROJ_PROMPT_TILES + 1,),
        in_specs=[
            pl.BlockSpec((PROJ_TM, D_MODEL), prompt_tile),
            pl.BlockSpec((PROJ_TM, D_MODEL), fixed),
            pl.BlockSpec((1, D_MODEL), fixed),
            pl.BlockSpec((D_MODEL, IN_DIM), fixed),
        ],
        out_specs=(pl.BlockSpec((PROJ_TM, IN_DIM), prompt_tile),
                   pl.BlockSpec((PROJ_TM, IN_DIM), fixed),
                   pl.BlockSpec((2 * KV_DIM, SAMPLE_ROWS), fixed)),
        compiler_params=_params("arbitrary"),
        name="inproj",
    )(xp, xs, g, w)


def _kv_head_of_lane(shape):
    return lax.broadcasted_iota(jnp.int32, shape, 1) // HEAD_DIM


def _pool_linear(d, gi, pw_ref, ps_ref):
    lanes = slice(gi * POOL_GROUP_DIM, (gi + 1) * POOL_GROUP_DIM)
    y = jnp.dot(d.astype(BF16), pw_ref[gi], preferred_element_type=F32)
    return y * ps_ref[:, lanes]


def _prompt_mix_body(sink_ref, u_ref, q_ref, kv_ref, kvh_ref, uh_ref, x_ref,
                     pw_ref, ps_ref, wo_ref, o_ref, mix_ref, p_ref):
    tiles_per_seq = SEQ // MIX_TQ
    tile = pl.program_id(0) % tiles_per_seq
    first = tile == 0

    lo = jnp.where(first, WINDOW, 0)
    keep_u = jnp.full((HALO_U, POOL_DIM), lo, jnp.int32) == 0
    keep_kv = jnp.full((WINDOW, 2 * KV_DIM), lo, jnp.int32) == 0
    uh = jnp.where(keep_u, uh_ref[...], 0.0)
    pos = tile * MIX_TQ + lax.broadcasted_iota(jnp.int32, (MIX_TQ, 1), 0)
    for gi, w in enumerate(POOL_WINDOWS):
        lanes = slice(gi * POOL_GROUP_DIM, (gi + 1) * POOL_GROUP_DIM)
        xg = jnp.concatenate([uh[:, lanes], u_ref[:, lanes]], axis=0)
        s = xg
        sh = 1
        while sh < w:
            s = s + pltpu.roll(s, sh, axis=0)
            sh *= 2
        cnt = jnp.minimum(w, pos + 1).astype(F32)
        d = s[HALO_U:] / cnt - xg[HALO_U:]
        mix_ref[:, lanes] = _pool_linear(d, gi, pw_ref, ps_ref).astype(BF16)

    kvh = jnp.where(keep_kv, kvh_ref[...], 0.0)
    k_all = jnp.concatenate([kvh[:, :KV_DIM], kv_ref[:, :KV_DIM]], axis=0)
    v_all = jnp.concatenate([kvh[:, KV_DIM:], kv_ref[:, KV_DIM:]], axis=0)
    head = _kv_head_of_lane((2 * WINDOW, KV_DIM))
    qi = lax.broadcasted_iota(jnp.int32, (ATT_CHUNK, 2 * WINDOW), 0)
    kj = lax.broadcasted_iota(jnp.int32, (ATT_CHUNK, 2 * WINDOW), 1)
    for n in range(MIX_TQ // WINDOW):
        kw = k_all[n * WINDOW:(n + 2) * WINDOW]
        vw = v_all[n * WINDOW:(n + 2) * WINDOW]
        qn = q_ref[n * WINDOW:(n + 1) * WINDOW, :].astype(BF16)
        qs = jnp.concatenate(
            [qn[:, g * KV_DIM:(g + 1) * KV_DIM] for g in range(GROUP)], axis=0)
        biases = []
        for c in range(WINDOW // ATT_CHUNK):
            q_pos = qi + c * ATT_CHUNK
            valid = (kj > q_pos) & (kj <= q_pos + WINDOW)
            if n == 0:
                valid = valid & (kj >= lo)
            biases.append(jnp.where(valid, 0.0, -jnp.inf))
        for k in range(N_KV_HEADS):
            kk = jnp.where(head == k, kw, 0.0).astype(BF16)
            s = lax.dot_general(qs, kk, (((1,), (1,)), ((), ())),
                                preferred_element_type=F32)
            for g in range(GROUP):
                sink = sink_ref[k * GROUP + g]
                for c in range(WINDOW // ATT_CHUNK):
                    r0 = g * WINDOW + c * ATT_CHUNK
                    sc = s[r0:r0 + ATT_CHUNK] + biases[c]
                    m = jnp.maximum(jnp.max(sc, axis=-1, keepdims=True), sink)
                    p = jnp.exp(sc - m)
                    den = jnp.sum(p, axis=-1, keepdims=True) + jnp.exp(sink - m)
                    p_ref[r0:r0 + ATT_CHUNK, k * 2 * WINDOW:(k + 1) * 2 * WINDOW] = (
                        p * (1.0 / den)).astype(BF16)
        vcat = jnp.concatenate(
            [jnp.where(head == k, vw, 0.0).astype(BF16)
             for k in range(N_KV_HEADS)], axis=0)
        o = jnp.dot(p_ref[...], vcat, preferred_element_type=F32)
        blk = slice(n * WINDOW, (n + 1) * WINDOW)
        for g in range(GROUP):
            mix_ref[blk, POOL_DIM + g * KV_DIM:POOL_DIM + (g + 1) * KV_DIM] = (
                o[g * WINDOW:(g + 1) * WINDOW].astype(BF16))
        if (n + 1) % OUT_BLOCKS == 0:
            done = slice((n + 1 - OUT_BLOCKS) * WINDOW, (n + 1) * WINDOW)
            o_ref[done, :] = x_ref[done, :] + jnp.dot(mix_ref[done, :], wo_ref[...],
                                                      preferred_element_type=F32)


def _prompt_mix(sinks, proj, x, pool_w, pool_scale, w_out):
    blocks_kv = MIX_TQ // WINDOW
    blocks_u = MIX_TQ // HALO_U
    return pl.pallas_call(
        _prompt_mix_body,
        out_shape=jax.ShapeDtypeStruct((PROMPT_ROWS, D_MODEL), F32),
        grid=(PROMPT_ROWS // MIX_TQ,),
        in_specs=[
            pl.BlockSpec(memory_space=pltpu.SMEM),
            pl.BlockSpec((MIX_TQ, POOL_DIM), lambda i: (i, 0)),
            pl.BlockSpec((MIX_TQ, Q_DIM), lambda i: (i, 1)),
            pl.BlockSpec((MIX_TQ, 2 * KV_DIM), lambda i: (i, 4)),
            pl.BlockSpec((WINDOW, 2 * KV_DIM),
                         lambda i: (jnp.maximum(i * blocks_kv - 1, 0), 4)),
            pl.BlockSpec((HALO_U, POOL_DIM),
                         lambda i: (jnp.maximum(i * blocks_u - 1, 0), 0)),
            pl.BlockSpec((MIX_TQ, D_MODEL), lambda i: (i, 0)),
            pl.BlockSpec((len(POOL_WINDOWS), POOL_GROUP_DIM, POOL_GROUP_DIM),
                         lambda i: (0, 0, 0)),
            pl.BlockSpec((1, POOL_DIM), lambda i: (0, 0)),
            pl.BlockSpec((D_MODEL, D_MODEL), lambda i: (0, 0)),
        ],
        out_specs=pl.BlockSpec((MIX_TQ, D_MODEL), lambda i: (i, 0)),
        scratch_shapes=[pltpu.VMEM((MIX_TQ, D_MODEL), BF16),
                        pltpu.VMEM((GROUP * WINDOW, N_KV_HEADS * 2 * WINDOW), BF16)],
        compiler_params=_params("parallel"),
        name="prompt_mix",
    )(sinks, proj, proj, proj, proj, proj, x, pool_w, pool_scale, w_out)


def _sample_pool_body(hist_ref, proj_ref, d_ref, newpool_ref):
    for gi, w in enumerate(POOL_WINDOWS):
        base = gi * POOL_GROUP_DIM
        seq = [hist_ref[j, :, base:base + POOL_GROUP_DIM] for j in range(POOL_HIST)]
        seq += [proj_ref[:, t * IN_DIM + base:t * IN_DIM + base + POOL_GROUP_DIM]
                for t in range(DEC_SEQ)]
        for t in range(DEC_SEQ):
            last = POOL_HIST + t
            s = seq[last - w + 1]
            for j in range(last - w + 2, last + 1):
                s = s + seq[j]
            cnt = float(min(w, N_HIST_SAMPLE + t + 1))
            d_ref[:, t * POOL_DIM + base:t * POOL_DIM + base + POOL_GROUP_DIM] = (
                s / cnt - seq[last])
    for j in range(POOL_HIST - DEC_SEQ):
        newpool_ref[j] = hist_ref[j + DEC_SEQ]
    for t in range(DEC_SEQ):
        newpool_ref[POOL_HIST - DEC_SEQ + t] = proj_ref[:, t * IN_DIM:t * IN_DIM + POOL_DIM]


def _sample_pool(layer, state_t, proj_flat):
    return pl.pallas_call(
        _sample_pool_body,
        out_shape=(jax.ShapeDtypeStruct((DEC_BATCH, DEC_SEQ * POOL_DIM), F32),
                   jax.ShapeDtypeStruct((POOL_HIST, DEC_BATCH, POOL_DIM), F32)),
        grid=(DEC_BATCH // POOL_SB,),
        in_specs=[
            pl.BlockSpec((None, POOL_HIST, POOL_SB, POOL_DIM), lambda i: (layer, 0, i, 0)),
            pl.BlockSpec((POOL_SB, DEC_SEQ * IN_DIM), lambda i: (i, 0)),
        ],
        out_specs=(pl.BlockSpec((POOL_SB, DEC_SEQ * POOL_DIM), lambda i: (i, 0)),
                   pl.BlockSpec((POOL_HIST, POOL_SB, POOL_DIM), lambda i: (0, i, 0))),
        compiler_params=_params("parallel"),
        name="sample_pool",
    )(state_t, proj_flat)


def _sample_attn_body(*refs, update_cache):
    if update_cache:
        (sink_ref, d_ref, q_ref, kvt_ref, ck_ref, cv_ref, x_ref, pw_ref, ps_ref, wo_ref,
         o_ref, nk_ref, nv_ref, mix_ref) = refs
    else:
        (sink_ref, d_ref, q_ref, kvt_ref, ck_ref, cv_ref, x_ref, pw_ref, ps_ref, wo_ref,
         o_ref, mix_ref) = refs

    lane = lax.broadcasted_iota(jnp.int32, (KV_DIM, N_BUF), 1)
    is_old = lane < N_BUF - DEC_SEQ

    def shift_in(b, kt, vt, kt_new, vt_new):
        shift_new = (N_BUF - DEC_SEQ - b * DEC_SEQ) % N_BUF
        nk_ref[b] = jnp.where(is_old, pltpu.roll(kt, N_BUF - DEC_SEQ, axis=1),
                              pltpu.roll(kt_new, shift_new, axis=1))
        nv_ref[b] = jnp.where(is_old, pltpu.roll(vt, N_BUF - DEC_SEQ, axis=1),
                              pltpu.roll(vt_new, shift_new, axis=1))

    def attend():
        _sample_attend(sink_ref, d_ref, q_ref, kvt_ref, ck_ref, cv_ref, x_ref, pw_ref,
                       ps_ref, wo_ref, o_ref, mix_ref,
                       shift_in if update_cache else None)

    if not update_cache:
        attend()
        return

    phase = pl.program_id(0)

    @pl.when(phase < DEPTH - 1)
    def _():
        kt_new = kvt_ref[:KV_DIM, :]
        vt_new = kvt_ref[KV_DIM:, :]
        for b in range(ATTN_SB):
            shift_in(b, ck_ref[b], cv_ref[b], kt_new, vt_new)

    @pl.when(phase == DEPTH - 1)
    def _():
        attend()


def _sample_attend(sink_ref, d_ref, q_ref, kvt_ref, ck_ref, cv_ref, x_ref, pw_ref, ps_ref,
                   wo_ref, o_ref, mix_ref, shift_in):
    for gi in range(len(POOL_WINDOWS)):
        lanes = slice(gi * POOL_GROUP_DIM, (gi + 1) * POOL_GROUP_DIM)
        mix_ref[:, lanes] = _pool_linear(d_ref[:, lanes], gi, pw_ref, ps_ref).astype(BF16)

    pair = 2 * DEC_SEQ
    grp = GROUP * pair
    rows = N_KV_HEADS * grp
    new_cols = ATTN_SB * DEC_SEQ
    keys = N_BUF + new_cols
    assert keys == KV_DIM and new_cols == N_BUF
    head = _kv_head_of_lane((rows, KV_DIM))
    row = lax.broadcasted_iota(jnp.int32, (rows, keys), 0)
    row_head = row // grp
    in_pair = row % pair
    low = in_pair < DEC_SEQ
    tok = in_pair % DEC_SEQ
    col = lax.broadcasted_iota(jnp.int32, (rows, keys), 1)
    new_col = col - N_BUF
    valid_cache = (col < N_BUF) & (col > tok)
    causal_new = (new_col >= 0) & ((new_col % DEC_SEQ) <= tok)
    seq_of_col = new_col // DEC_SEQ - in_pair // DEC_SEQ
    sink = jnp.concatenate(
        [jnp.full((pair, 1), sink_ref[k * GROUP + g], F32)
         for k in range(N_KV_HEADS) for g in range(GROUP)], axis=0)
    kt_new = kvt_ref[:KV_DIM, :]
    vt_new = kvt_ref[KV_DIM:, :]
    kt_new_bf = kt_new.astype(BF16)
    vt_new_bf = vt_new.astype(BF16)
    nt = (((1,), (1,)), ((), ()))

    for j in range(ATTN_SB // 2):
        r0 = j * pair
        q8 = q_ref[r0:r0 + pair, :]
        qs = jnp.concatenate(
            [q8[:, g * KV_DIM:(g + 1) * KV_DIM] for g in range(GROUP)], axis=0)
        qrep = jnp.concatenate([qs] * N_KV_HEADS, axis=0)
        lhs = jnp.where(head == row_head, qrep, 0.0).astype(BF16)
        scores, values = [], []
        for b in (2 * j, 2 * j + 1):
            kt = ck_ref[b]
            vt = cv_ref[b]
            scores.append(jnp.dot(lhs, kt.astype(BF16), preferred_element_type=F32))
            values.append(vt.astype(BF16))
            if shift_in is not None:
                shift_in(b, kt, vt, kt_new, vt_new)
        s_new = jnp.dot(lhs, kt_new_bf, preferred_element_type=F32)
        s = jnp.concatenate([jnp.where(low[:, :N_BUF], scores[0], scores[1]), s_new],
                            axis=1)
        valid = valid_cache | (causal_new & (seq_of_col == 2 * j))
        s = jnp.where(valid, s, -jnp.inf)
        m = jnp.maximum(jnp.max(s, axis=-1, keepdims=True), sink)
        p = jnp.exp(s - m)
        den = jnp.sum(p, axis=-1, keepdims=True) + jnp.exp(sink - m)
        p = (p / den).astype(BF16)
        p_old = p[:, :N_BUF]
        o = jnp.where(low,
                      lax.dot_general(p_old, values[0], nt, preferred_element_type=F32),
                      lax.dot_general(p_old, values[1], nt, preferred_element_type=F32))
        o = o + lax.dot_general(p[:, N_BUF:], vt_new_bf, nt, preferred_element_type=F32)
        o = jnp.where(head == row_head, o, 0.0)
        og = o[0:grp]
        for k in range(1, N_KV_HEADS):
            og = og + o[k * grp:(k + 1) * grp]
        for g in range(GROUP):
            mix_ref[r0:r0 + pair,
                    POOL_DIM + g * KV_DIM:POOL_DIM + (g + 1) * KV_DIM] = (
                        og[g * pair:(g + 1) * pair].astype(BF16))

    o_ref[...] = x_ref[...] + jnp.dot(mix_ref[...], wo_ref[...],
                                      preferred_element_type=F32)


def _sample_attn(layer, sinks, d_rows, proj, kvts, cache_kt, cache_vt, x, pool_w,
                 pool_scale, w_out):
    phases = kvts.shape[0]
    update_cache = phases > 1
    assert phases == 1 or (phases == DEPTH and layer == DEPTH - 1)
    layer_of = (lambda p: p) if update_cache else (lambda p: layer)
    row = lambda p, i: jnp.where(p == phases - 1, i, 0)
    cache_spec = pl.BlockSpec((None, ATTN_SB, KV_DIM, N_BUF),
                              lambda p, i: (layer_of(p), i, 0, 0))
    out_shape = [jax.ShapeDtypeStruct((SAMPLE_ROWS, D_MODEL), F32)]
    out_specs = [pl.BlockSpec((ATTN_ROWS, D_MODEL), lambda p, i: (row(p, i), 0))]
    if update_cache:
        out_shape += [jax.ShapeDtypeStruct((DEPTH, DEC_BATCH, KV_DIM, N_BUF), F32)] * 2
        out_specs += [pl.BlockSpec((None, ATTN_SB, KV_DIM, N_BUF),
                                   lambda p, i: (p, i, 0, 0))] * 2
    return pl.pallas_call(
        functools.partial(_sample_attn_body, update_cache=update_cache),
        out_shape=out_shape,
        grid=(phases, DEC_BATCH // ATTN_SB),
        in_specs=[
            pl.BlockSpec(memory_space=pltpu.SMEM),
            pl.BlockSpec((ATTN_ROWS, POOL_DIM), lambda p, i: (row(p, i), 0)),
            pl.BlockSpec((ATTN_ROWS, Q_DIM), lambda p, i: (row(p, i), 1)),
            pl.BlockSpec((None, 2 * KV_DIM, ATTN_ROWS), lambda p, i: (p, 0, i)),
            cache_spec,
            cache_spec,
            pl.BlockSpec((ATTN_ROWS, D_MODEL), lambda p, i: (row(p, i), 0)),
            pl.BlockSpec((len(POOL_WINDOWS), POOL_GROUP_DIM, POOL_GROUP_DIM),
                         lambda p, i: (0, 0, 0)),
            pl.BlockSpec((1, POOL_DIM), lambda p, i: (0, 0)),
            pl.BlockSpec((D_MODEL, D_MODEL), lambda p, i: (0, 0),
                         pipeline_mode=pl.Buffered(1)),
        ],
        out_specs=out_specs,
        scratch_shapes=[pltpu.VMEM((ATTN_ROWS, D_MODEL), BF16)],
        compiler_params=_params("arbitrary", "arbitrary"),
        name="sample_attn",
    )(sinks, d_rows, proj, kvts, cache_kt, cache_vt, x, pool_w, pool_scale, w_out)


def kernel(x_prompt, x_sample, cache_k, cache_v, state_pool, norm_ffn1, ffn1_gate,
           ffn1_up, ffn1_down, norm_mix, w_in, pool_w, pool_scale, attn_sinks,
           w_out, norm_ffn2, ffn2_gate, ffn2_up, ffn2_down, final_norm):
    xp = x_prompt.reshape(PROMPT_ROWS, D_MODEL)
    xs = x_sample.reshape(SAMPLE_ROWS, D_MODEL)
    gf = final_norm.reshape(1, D_MODEL)
    state_t = jnp.transpose(state_pool, (0, 2, 1, 3))

    def to_cache_t(c):
        return jnp.transpose(c, (0, 1, 3, 4, 2)).reshape(DEPTH, DEC_BATCH, KV_DIM, N_BUF)

    def from_cache_t(c):
        c = c.reshape(DEPTH, DEC_BATCH, N_KV_HEADS, HEAD_DIM, N_BUF)
        return jnp.transpose(c, (0, 1, 4, 2, 3))

    cache_kt = to_cache_t(cache_k)
    cache_vt = to_cache_t(cache_v)
    keep = min(WINDOW, SEQ)
    assert keep >= POOL_HIST
    k0 = POOL_DIM + Q_DIM

    ffn1_w = (ffn1_gate, ffn1_up, ffn1_down)
    ffn2_w = (ffn2_gate, ffn2_up, ffn2_down)
    head, *w_bf = _ffn_head(0, xp, norm_ffn1[0].reshape(1, D_MODEL), *ffn1_w)
    kp_l, vp_l, pp_l, ps_l, kvt_l = [], [], [], [], []
    for l in range(DEPTH):
        last = l == DEPTH - 1
        pool_w_l = pool_w[l].astype(BF16)
        pool_scale_l = pool_scale[l].reshape(1, POOL_DIM)
        sinks_l = attn_sinks[l].astype(F32)

        xp, xs, w_bf, (w_in_l, w_out_l) = _ffn(
            xp, xs, norm_ffn1[l].reshape(1, D_MODEL), *w_bf, gf,
            next_w=(l,) + ffn2_w, mix_w=(l, w_in, w_out),
            head=head if l == 0 else None)
        proj_p, proj_s, kvt = _inproj(xp, xs, norm_mix[l].reshape(1, D_MODEL), w_in_l)

        tails = [proj_p[(b + 1) * SEQ - keep:(b + 1) * SEQ] for b in range(BATCH)]
        kp_l.append(jnp.stack([t[:, k0:k0 + KV_DIM] for t in tails])
                    .reshape(BATCH, keep, N_KV_HEADS, HEAD_DIM))
        vp_l.append(jnp.stack([t[:, k0 + KV_DIM:] for t in tails])
                    .reshape(BATCH, keep, N_KV_HEADS, HEAD_DIM))
        pp_l.append(jnp.stack([t[keep - POOL_HIST:, :POOL_DIM] for t in tails]))

        d_flat, new_pool = _sample_pool(
            l, state_t, proj_s.reshape(DEC_BATCH, DEC_SEQ * IN_DIM))
        ps_l.append(new_pool)

        xp = _prompt_mix(sinks_l, proj_p, xp, pool_w_l, pool_scale_l, w_out_l)
        kvt_l.append(kvt)
        outs = _sample_attn(
            l, sinks_l, d_flat.reshape(SAMPLE_ROWS, POOL_DIM), proj_s,
            jnp.stack(kvt_l) if last else kvt[None],
            cache_kt, cache_vt, xs, pool_w_l, pool_scale_l, w_out_l)
        xs = outs[0]
        if last:
            new_kt, new_vt = outs[1:]

        xp, xs, w_bf, _ = _ffn(xp, xs, norm_ffn2[l].reshape(1, D_MODEL), *w_bf, gf,
                               next_w=None if last else (l + 1,) + ffn1_w,
                               final_norm=last)

    y_prompt = xp.reshape(BATCH, SEQ, D_MODEL)
    y_sample = xs.reshape(DEC_BATCH, DEC_SEQ, D_MODEL)
    new_pool_sample = jnp.transpose(jnp.stack(ps_l), (0, 2, 1, 3))
    return (y_prompt, y_sample, jnp.stack(kp_l), jnp.stack(vp_l), jnp.stack(pp_l),
            from_cache_t(new_kt), from_cache_t(new_vt), new_pool_sample)
```

```python
import functools

import jax
import jax.numpy as jnp
from jax import lax
from jax.experimental import pallas as pl
from jax.experimental.pallas import tpu as pltpu

F32 = jnp.float32
BF16 = jnp.bfloat16

D_MODEL = 2048
BATCH = 2
SEQ = 4096
DEPTH = 2
DEC_BATCH = 128
DEC_SEQ = 4
PAST_LEN = 8192

POOL_DIM = 1024
POOL_WINDOWS = (2, 4, 8, 16)
POOL_GROUP_DIM = POOL_DIM // len(POOL_WINDOWS)
POOL_HIST = max(POOL_WINDOWS) - 1
HEAD_DIM = 64
N_HEADS = 16
N_KV_HEADS = 4
GROUP = N_HEADS // N_KV_HEADS
Q_DIM = N_HEADS * HEAD_DIM
KV_DIM = N_KV_HEADS * HEAD_DIM
IN_DIM = POOL_DIM + Q_DIM + 2 * KV_DIM
WINDOW = 128
FFN_DIM = 5632
RMS_EPS = 1e-5
ATTN_SCALE = HEAD_DIM ** -0.5
N_BUF = min(WINDOW, PAST_LEN)
N_HIST_SAMPLE = min(POOL_HIST, PAST_LEN)

PROMPT_ROWS = BATCH * SEQ
SAMPLE_ROWS = DEC_BATCH * DEC_SEQ
M_ROWS = PROMPT_ROWS + SAMPLE_ROWS

VMEM_LIMIT_BYTES = 56 * 1024 * 1024

FFN_TM = 1088
FFN_TF = 512
FFN_TILES = M_ROWS // FFN_TM
FFN_CHUNKS = FFN_DIM // FFN_TF
FFN_FULL_TILES = PROMPT_ROWS // FFN_TM
FFN_SPLIT = PROMPT_ROWS - FFN_FULL_TILES * FFN_TM
FFN_TF_HEAD = 256
WIN_ROWS = 32
WIN_BLOCKS = D_MODEL // WIN_ROWS
WOUT_BLOCKS = D_MODEL // HEAD_DIM
PROJ_TM = 512
PROJ_PROMPT_TILES = PROMPT_ROWS // PROJ_TM
MIX_TQ = 512
HALO_U = 16
OUT_BLOCKS = 2
ATT_CHUNK = 64
POOL_SB = 32
ATTN_SB = 32
ATTN_ROWS = ATTN_SB * DEC_SEQ

assert FFN_TILES * FFN_TM == M_ROWS and FFN_CHUNKS * FFN_TF == FFN_DIM
assert FFN_FULL_TILES + 1 == FFN_TILES and FFN_SPLIT + SAMPLE_ROWS == FFN_TM
assert FFN_SPLIT % 8 == 0 and FFN_FULL_TILES >= 1
assert FFN_CHUNKS >= 2
assert FFN_DIM % FFN_TF_HEAD == 0
assert SAMPLE_ROWS == PROJ_TM and PROJ_PROMPT_TILES * PROJ_TM == PROMPT_ROWS


def _rms(x, g):
    return x * lax.rsqrt(jnp.mean(x * x, axis=-1, keepdims=True) + RMS_EPS) * g


def _params(*sem):
    return pltpu.CompilerParams(dimension_semantics=sem,
                                vmem_limit_bytes=VMEM_LIMIT_BYTES)


def _ffn_body(*refs, final_norm, cast_next, prep_mix, t0, n_tiles, head_in):
    xp_hbm, xs_hbm, g_ref, wg_ref, wu_ref, wd_ref, gf_ref = refs[:7]
    refs = refs[7:]
    if cast_next:
        ng_ref, nu_ref, nd_ref = refs[:3]
        refs = refs[3:]
    if prep_mix:
        win_ref, wout_ref = refs[:2]
        refs = refs[2:]
    if head_in:
        head_hbm = refs[0]
        refs = refs[1:]
    op_hbm, os_hbm = refs[:2]
    refs = refs[2:]
    if cast_next:
        cg_ref, cu_ref, cd_ref = refs[:3]
        refs = refs[3:]
    if prep_mix:
        cwin_ref, cwout_ref = refs[:2]
        refs = refs[2:]
    acc_ref, xn_ref, in_sem, out_sem = refs[:4]
    head_sem = refs[4] if head_in else None

    i = pl.program_id(0)
    f = pl.program_id(1)
    slot = i % 2
    tile = i + t0

    def head_copy():
        return pltpu.make_async_copy(head_hbm, op_hbm.at[pl.ds(0, t0 * FFN_TM)], head_sem)

    def copies(p_hbm, s_hbm, tile, s, sem, mixed, to_vmem):
        if mixed:
            pairs = [(p_hbm.at[pl.ds(FFN_FULL_TILES * FFN_TM, FFN_SPLIT)],
                      acc_ref.at[s, pl.ds(0, FFN_SPLIT)]),
                     (s_hbm, acc_ref.at[s, pl.ds(FFN_SPLIT, SAMPLE_ROWS)])]
        else:
            pairs = [(p_hbm.at[pl.ds(tile * FFN_TM, FFN_TM)], acc_ref.at[s])]
        return [pltpu.make_async_copy(h, v, sem.at[s]) if to_vmem
                else pltpu.make_async_copy(v, h, sem.at[s]) for h, v in pairs]

    def on_tile(tile, s, op, to_vmem):
        p_hbm, s_hbm, sem = ((xp_hbm, xs_hbm, in_sem) if to_vmem
                             else (op_hbm, os_hbm, out_sem))

        @pl.when(tile < FFN_FULL_TILES)
        def _():
            for c in copies(p_hbm, s_hbm, tile, s, sem, False, to_vmem):
                op(c)

        @pl.when(tile == FFN_FULL_TILES)
        def _():
            for c in copies(p_hbm, s_hbm, tile, s, sem, True, to_vmem):
                op(c)

    start = lambda c: c.start()
    wait = lambda c: c.wait()

    @pl.when((i == 0) & (f == 0))
    def _():
        on_tile(tile, slot, start, True)
        if head_in:
            head_copy().start()

    @pl.when(f == 0)
    def _():
        on_tile(tile, slot, wait, True)
        xn_ref[...] = _rms(acc_ref[slot], g_ref[...]).astype(BF16)

    a = xn_ref[...]
    gate = jnp.dot(a, wg_ref[...], preferred_element_type=F32)
    if cast_next:
        cg_ref[...] = ng_ref[...].astype(BF16)
        cu_ref[...] = nu_ref[...].astype(BF16)
    up = jnp.dot(a, wu_ref[...], preferred_element_type=F32)
    if cast_next:
        cd_ref[...] = nd_ref[...].astype(BF16)
    if prep_mix:
        _prep_w_in(win_ref, cwin_ref)
        cwout_ref[...] = wout_ref[...].astype(BF16)
    h = (gate * jax.nn.sigmoid(gate) * up).astype(BF16)
    acc_ref[slot] += 0.5 * jnp.dot(h, wd_ref[...], preferred_element_type=F32)

    @pl.when(f == 1)
    def _():
        @pl.when(i >= 1)
        def _():
            on_tile(tile - 1, 1 - slot, wait, False)

        @pl.when(i + 1 < n_tiles)
        def _():
            on_tile(tile + 1, 1 - slot, start, True)

    @pl.when(f == FFN_CHUNKS - 1)
    def _():
        if final_norm:
            acc_ref[slot] = _rms(acc_ref[slot], gf_ref[...])
        on_tile(tile, slot, start, False)

        @pl.when(i == n_tiles - 1)
        def _():
            on_tile(tile, slot, wait, False)
            if head_in:
                head_copy().wait()


def _prep_w_in(w_ref, o_ref):
    k0 = POOL_DIM + Q_DIM
    o_ref[:, :POOL_DIM] = w_ref[:, :POOL_DIM].astype(BF16)
    o_ref[:, k0:] = w_ref[:, k0:].astype(BF16)
    lane = lax.broadcasted_iota(jnp.int32, (WIN_ROWS, 2 * HEAD_DIM), 1)
    for g in range(GROUP):
        for kp in range(N_KV_HEADS // 2):
            src = [POOL_DIM + (2 * kp + j) * KV_DIM + (g // 2) * 2 * HEAD_DIM
                   for j in (0, 1)]
            lo, hi = [w_ref[:, c:c + 2 * HEAD_DIM] for c in src]
            if g % 2 == 0:
                hi = pltpu.roll(hi, HEAD_DIM, axis=1)
            else:
                lo = pltpu.roll(lo, HEAD_DIM, axis=1)
            dst = POOL_DIM + g * KV_DIM + kp * 2 * HEAD_DIM
            o_ref[:, dst:dst + 2 * HEAD_DIM] = (
                jnp.where(lane < HEAD_DIM, lo, hi) * ATTN_SCALE).astype(BF16)


def _w_out_src_block(out_blk):
    pool_blocks = POOL_DIM // HEAD_DIM
    h = out_blk - pool_blocks
    src = pool_blocks + (h % N_KV_HEADS) * GROUP + h // N_KV_HEADS
    return jnp.where(out_blk < pool_blocks, out_blk, src)


def _ffn(xp, xs, g, wg, wu, wd, gf, next_w=None, mix_w=None, *, final_norm=False,
         head=None):
    cast_next = next_w is not None
    prep_mix = mix_w is not None
    head_in = head is not None
    t0 = head.shape[0] // FFN_TM if head_in else 0
    n_tiles = FFN_TILES - t0
    cast_blocks = 1 << (n_tiles.bit_length() - 1)
    cast_rows = D_MODEL // cast_blocks
    win_steps = pl.cdiv(WIN_BLOCKS, n_tiles)
    wout_steps = pl.cdiv(WOUT_BLOCKS, n_tiles)
    assert max(win_steps, wout_steps) <= FFN_CHUNKS
    cast_blk = lambda i: jnp.minimum(i, cast_blocks - 1)
    cast_chunk = lambda i, f: jnp.where(i < cast_blocks, f, FFN_CHUNKS - 1)
    vec = pl.BlockSpec((1, D_MODEL), lambda i, f: (0, 0))
    in_specs = [
        pl.BlockSpec(memory_space=pl.ANY),
        pl.BlockSpec(memory_space=pl.ANY),
        vec,
        pl.BlockSpec((D_MODEL, FFN_TF), lambda i, f: (0, f)),
        pl.BlockSpec((D_MODEL, FFN_TF), lambda i, f: (0, f)),
        pl.BlockSpec((FFN_TF, D_MODEL), lambda i, f: (f, 0)),
        vec,
    ]
    out_shape = [jax.ShapeDtypeStruct((PROMPT_ROWS, D_MODEL), F32),
                 jax.ShapeDtypeStruct((SAMPLE_ROWS, D_MODEL), F32)]
    out_specs = [pl.BlockSpec(memory_space=pl.ANY), pl.BlockSpec(memory_space=pl.ANY)]
    args = [xp, xs, g, wg, wu, wd, gf]
    if cast_next:
        nl = next_w[0]
        in_specs += [
            pl.BlockSpec((None, cast_rows, FFN_TF),
                         lambda i, f: (nl, cast_blk(i), cast_chunk(i, f))),
            pl.BlockSpec((None, cast_rows, FFN_TF),
                         lambda i, f: (nl, cast_blk(i), cast_chunk(i, f))),
            pl.BlockSpec((None, FFN_TF, cast_rows),
                         lambda i, f: (nl, cast_chunk(i, f), cast_blk(i))),
        ]
        out_shape += [jax.ShapeDtypeStruct((D_MODEL, FFN_DIM), BF16),
                      jax.ShapeDtypeStruct((D_MODEL, FFN_DIM), BF16),
                      jax.ShapeDtypeStruct((FFN_DIM, D_MODEL), BF16)]
        out_specs += [
            pl.BlockSpec((cast_rows, FFN_TF), lambda i, f: (cast_blk(i), cast_chunk(i, f))),
            pl.BlockSpec((cast_rows, FFN_TF), lambda i, f: (cast_blk(i), cast_chunk(i, f))),
            pl.BlockSpec((FFN_TF, cast_rows), lambda i, f: (cast_chunk(i, f), cast_blk(i))),
        ]
        args += list(next_w[1:])
    if prep_mix:
        ml = mix_w[0]
        win_blk = lambda i, f: jnp.minimum(
            i * win_steps + jnp.minimum(f, win_steps - 1), WIN_BLOCKS - 1)
        wout_blk = lambda i, f: jnp.minimum(
            i * wout_steps + jnp.minimum(f, wout_steps - 1), WOUT_BLOCKS - 1)
        in_specs += [
            pl.BlockSpec((None, WIN_ROWS, IN_DIM), lambda i, f: (ml, win_blk(i, f), 0)),
            pl.BlockSpec((None, HEAD_DIM, D_MODEL),
                         lambda i, f: (ml, _w_out_src_block(wout_blk(i, f)), 0)),
        ]
        out_shape += [jax.ShapeDtypeStruct((D_MODEL, IN_DIM), BF16),
                      jax.ShapeDtypeStruct((D_MODEL, D_MODEL), BF16)]
        out_specs += [
            pl.BlockSpec((WIN_ROWS, IN_DIM), lambda i, f: (win_blk(i, f), 0)),
            pl.BlockSpec((HEAD_DIM, D_MODEL), lambda i, f: (wout_blk(i, f), 0)),
        ]
        args += list(mix_w[1:])
    scratch = [pltpu.VMEM((2, FFN_TM, D_MODEL), F32),
               pltpu.VMEM((FFN_TM, D_MODEL), BF16),
               pltpu.SemaphoreType.DMA((2,)),
               pltpu.SemaphoreType.DMA((2,))]
    if head_in:
        in_specs.append(pl.BlockSpec(memory_space=pl.ANY))
        args.append(head)
        scratch.append(pltpu.SemaphoreType.DMA(()))
    outs = pl.pallas_call(
        functools.partial(_ffn_body, final_norm=final_norm, cast_next=cast_next,
                          prep_mix=prep_mix, t0=t0, n_tiles=n_tiles, head_in=head_in),
        out_shape=out_shape,
        grid=(n_tiles, FFN_CHUNKS),
        in_specs=in_specs,
        out_specs=out_specs,
        scratch_shapes=scratch,
        compiler_params=_params("arbitrary", "arbitrary"),
        name="ffn_final" if final_norm else "ffn",
    )(*args)
    n_cast = 3 if cast_next else 0
    return outs[0], outs[1], tuple(outs[2:2 + n_cast]), tuple(outs[2 + n_cast:])


def _ffn_head_body(x_ref, g_ref, wg_ref, wu_ref, wd_ref, o_ref, og_ref, ou_ref, od_ref,
                   xn_ref, wgs_ref, wus_ref, wds_ref):
    step = pl.program_id(0)
    slot = step % 2

    @pl.when(step == 0)
    def _():
        x = x_ref[...]
        xn_ref[...] = _rms(x, g_ref[...]).astype(BF16)
        o_ref[...] = x
        wgs_ref[1] = jnp.zeros(wgs_ref.shape[1:], BF16)
        wus_ref[1] = jnp.zeros(wus_ref.shape[1:], BF16)
        wds_ref[1] = jnp.zeros(wds_ref.shape[1:], BF16)

    a = xn_ref[...]
    wg = wg_ref[...].astype(BF16)
    wgs_ref[slot] = wg
    og_ref[...] = wg
    gate = jnp.dot(a, wgs_ref[1 - slot], preferred_element_type=F32)
    wu = wu_ref[...].astype(BF16)
    wus_ref[slot] = wu
    ou_ref[...] = wu
    up = jnp.dot(a, wus_ref[1 - slot], preferred_element_type=F32)
    wd = wd_ref[...].astype(BF16)
    wds_ref[slot] = wd
    od_ref[...] = wd
    h = (gate * jax.nn.sigmoid(gate) * up).astype(BF16)
    o_ref[...] += 0.5 * jnp.dot(h, wds_ref[1 - slot], preferred_element_type=F32)


def _ffn_head(layer, xp, g, gate, up, down):
    tf = FFN_TF_HEAD
    n_chunks = FFN_DIM // tf
    fixed = lambda s: (0, 0)
    chunk = lambda s: jnp.minimum(s, n_chunks - 1)
    return pl.pallas_call(
        _ffn_head_body,
        out_shape=(jax.ShapeDtypeStruct((FFN_TM, D_MODEL), F32),
                   jax.ShapeDtypeStruct((D_MODEL, FFN_DIM), BF16),
                   jax.ShapeDtypeStruct((D_MODEL, FFN_DIM), BF16),
                   jax.ShapeDtypeStruct((FFN_DIM, D_MODEL), BF16)),
        grid=(n_chunks + 1,),
        in_specs=[
            pl.BlockSpec((FFN_TM, D_MODEL), fixed, pipeline_mode=pl.Buffered(1)),
            pl.BlockSpec((1, D_MODEL), fixed),
            pl.BlockSpec((None, D_MODEL, tf), lambda s: (layer, 0, chunk(s))),
            pl.BlockSpec((None, D_MODEL, tf), lambda s: (layer, 0, chunk(s))),
            pl.BlockSpec((None, tf, D_MODEL), lambda s: (layer, chunk(s), 0)),
        ],
        out_specs=(pl.BlockSpec((FFN_TM, D_MODEL), fixed),
                   pl.BlockSpec((D_MODEL, tf), lambda s: (0, chunk(s))),
                   pl.BlockSpec((D_MODEL, tf), lambda s: (0, chunk(s))),
                   pl.BlockSpec((tf, D_MODEL), lambda s: (chunk(s), 0))),
        scratch_shapes=[pltpu.VMEM((FFN_TM, D_MODEL), BF16),
                        pltpu.VMEM((2, D_MODEL, tf), BF16),
                        pltpu.VMEM((2, D_MODEL, tf), BF16),
                        pltpu.VMEM((2, tf, D_MODEL), BF16)],
        compiler_params=_params("arbitrary"),
        name="ffn_head",
    )(xp, g, gate, up, down)


def _inproj_body(xp_ref, xs_ref, g_ref, w_ref, pp_ref, ps_ref, kvt_ref):
    i = pl.program_id(0)

    @pl.when(i < PROJ_PROMPT_TILES)
    def _():
        xn = _rms(xp_ref[...], g_ref[...]).astype(BF16)
        pp_ref[...] = jnp.dot(xn, w_ref[...], preferred_element_type=F32)

    @pl.when(i == PROJ_PROMPT_TILES)
    def _():
        xn = _rms(xs_ref[...], g_ref[...]).astype(BF16)
        proj = jnp.dot(xn, w_ref[...], preferred_element_type=F32)
        ps_ref[...] = proj
        kvt_ref[...] = proj[:, POOL_DIM + Q_DIM:].T


def _inproj(xp, xs, g, w):
    prompt_tile = lambda i: (jnp.minimum(i, PROJ_PROMPT_TILES - 1), 0)
    fixed = lambda i: (0, 0)
    return pl.pallas_call(
        _inproj_body,
        out_shape=(jax.ShapeDtypeStruct((PROMPT_ROWS, IN_DIM), F32),
                   jax.ShapeDtypeStruct((SAMPLE_ROWS, IN_DIM), F32),
                   jax.ShapeDtypeStruct((2 * KV_DIM, SAMPLE_ROWS), F32)),
        grid=(PROJ_PROMPT_TILES + 1,),
        in_specs=[
            pl.BlockSpec((PROJ_TM, D_MODEL), prompt_tile),
            pl.BlockSpec((PROJ_TM, D_MODEL), fixed),
            pl.BlockSpec((1, D_MODEL), fixed),
            pl.BlockSpec((D_MODEL, IN_DIM), fixed),
        ],
        out_specs=(pl.BlockSpec((PROJ_TM, IN_DIM), prompt_tile),
                   pl.BlockSpec((PROJ_TM, IN_DIM), fixed),
                   pl.BlockSpec((2 * KV_DIM, SAMPLE_ROWS), fixed)),
        compiler_params=_params("arbitrary"),
        name="inproj",
    )(xp, xs, g, w)


def _kv_head_of_lane(shape):
    return lax.broadcasted_iota(jnp.int32, shape, 1) // HEAD_DIM


def _pool_linear(d, gi, pw_ref, ps_ref):
    lanes = slice(gi * POOL_GROUP_DIM, (gi + 1) * POOL_GROUP_DIM)
    y = jnp.dot(d.astype(BF16), pw_ref[gi], preferred_element_type=F32)
    return y * ps_ref[:, lanes]


def _prompt_mix_body(sink_ref, u_ref, q_ref, kv_ref, kvh_ref, uh_ref, x_ref,
                     pw_ref, ps_ref, wo_ref, o_ref, mix_ref, p_ref):
    tiles_per_seq = SEQ // MIX_TQ
    tile = pl.program_id(0) % tiles_per_seq
    first = tile == 0

    lo = jnp.where(first, WINDOW, 0)
    keep_u = jnp.full((HALO_U, POOL_DIM), lo, jnp.int32) == 0
    keep_kv = jnp.full((WINDOW, 2 * KV_DIM), lo, jnp.int32) == 0
    uh = jnp.where(keep_u, uh_ref[...], 0.0)
    pos = tile * MIX_TQ + lax.broadcasted_iota(jnp.int32, (MIX_TQ, 1), 0)
    for gi, w in enumerate(POOL_WINDOWS):
        lanes = slice(gi * POOL_GROUP_DIM, (gi + 1) * POOL_GROUP_DIM)
        xg = jnp.concatenate([uh[:, lanes], u_ref[:, lanes]], axis=0)
        s = xg
        sh = 1
        while sh < w:
            s = s + pltpu.roll(s, sh, axis=0)
            sh *= 2
        cnt = jnp.minimum(w, pos + 1).astype(F32)
        d = s[HALO_U:] / cnt - xg[HALO_U:]
        mix_ref[:, lanes] = _pool_linear(d, gi, pw_ref, ps_ref).astype(BF16)

    kvh = jnp.where(keep_kv, kvh_ref[...], 0.0)
    k_all = jnp.concatenate([kvh[:, :KV_DIM], kv_ref[:, :KV_DIM]], axis=0)
    v_all = jnp.concatenate([kvh[:, KV_DIM:], kv_ref[:, KV_DIM:]], axis=0)
    head = _kv_head_of_lane((2 * WINDOW, KV_DIM))
    qi = lax.broadcasted_iota(jnp.int32, (ATT_CHUNK, 2 * WINDOW), 0)
    kj = lax.broadcasted_iota(jnp.int32, (ATT_CHUNK, 2 * WINDOW), 1)
    for n in range(MIX_TQ // WINDOW):
        kw = k_all[n * WINDOW:(n + 2) * WINDOW]
        vw = v_all[n * WINDOW:(n + 2) * WINDOW]
        qn = q_ref[n * WINDOW:(n + 1) * WINDOW, :].astype(BF16)
        qs = jnp.concatenate(
            [qn[:, g * KV_DIM:(g + 1) * KV_DIM] for g in range(GROUP)], axis=0)
        biases = []
        for c in range(WINDOW // ATT_CHUNK):
            q_pos = qi + c * ATT_CHUNK
            valid = (kj > q_pos) & (kj <= q_pos + WINDOW)
            if n == 0:
                valid = valid & (kj >= lo)
            biases.append(jnp.where(valid, 0.0, -jnp.inf))
        for k in range(N_KV_HEADS):
            kk = jnp.where(head == k, kw, 0.0).astype(BF16)
            s = lax.dot_general(qs, kk, (((1,), (1,)), ((), ())),
                                preferred_element_type=F32)
            for g in range(GROUP):
                sink = sink_ref[k * GROUP + g]
                for c in range(WINDOW // ATT_CHUNK):
                    r0 = g * WINDOW + c * ATT_CHUNK
                    sc = s[r0:r0 + ATT_CHUNK] + biases[c]
                    m = jnp.maximum(jnp.max(sc, axis=-1, keepdims=True), sink)
                    p = jnp.exp(sc - m)
                    den = jnp.sum(p, axis=-1, keepdims=True) + jnp.exp(sink - m)
                    p_ref[r0:r0 + ATT_CHUNK, k * 2 * WINDOW:(k + 1) * 2 * WINDOW] = (
                        p * (1.0 / den)).astype(BF16)
        vcat = jnp.concatenate(
            [jnp.where(head == k, vw, 0.0).astype(BF16)
             for k in range(N_KV_HEADS)], axis=0)
        o = jnp.dot(p_ref[...], vcat, preferred_element_type=F32)
        blk = slice(n * WINDOW, (n + 1) * WINDOW)
        for g in range(GROUP):
            mix_ref[blk, POOL_DIM + g * KV_DIM:POOL_DIM + (g + 1) * KV_DIM] = (
                o[g * WINDOW:(g + 1) * WINDOW].astype(BF16))
        if (n + 1) % OUT_BLOCKS == 0:
            done = slice((n + 1 - OUT_BLOCKS) * WINDOW, (n + 1) * WINDOW)
            o_ref[done, :] = x_ref[done, :] + jnp.dot(mix_ref[done, :], wo_ref[...],
                                                      preferred_element_type=F32)


def _prompt_mix(sinks, proj, x, pool_w, pool_scale, w_out):
    blocks_kv = MIX_TQ // WINDOW
    blocks_u = MIX_TQ // HALO_U
    return pl.pallas_call(
        _prompt_mix_body,
        out_shape=jax.ShapeDtypeStruct((PROMPT_ROWS, D_MODEL), F32),
        grid=(PROMPT_ROWS // MIX_TQ,),
        in_specs=[
            pl.BlockSpec(memory_space=pltpu.SMEM),
            pl.BlockSpec((MIX_TQ, POOL_DIM), lambda i: (i, 0)),
            pl.BlockSpec((MIX_TQ, Q_DIM), lambda i: (i, 1)),
            pl.BlockSpec((MIX_TQ, 2 * KV_DIM), lambda i: (i, 4)),
            pl.BlockSpec((WINDOW, 2 * KV_DIM),
                         lambda i: (jnp.maximum(i * blocks_kv - 1, 0), 4)),
            pl.BlockSpec((HALO_U, POOL_DIM),
                         lambda i: (jnp.maximum(i * blocks_u - 1, 0), 0)),
            pl.BlockSpec((MIX_TQ, D_MODEL), lambda i: (i, 0)),
            pl.BlockSpec((len(POOL_WINDOWS), POOL_GROUP_DIM, POOL_GROUP_DIM),
                         lambda i: (0, 0, 0)),
            pl.BlockSpec((1, POOL_DIM), lambda i: (0, 0)),
            pl.BlockSpec((D_MODEL, D_MODEL), lambda i: (0, 0)),
        ],
        out_specs=pl.BlockSpec((MIX_TQ, D_MODEL), lambda i: (i, 0)),
        scratch_shapes=[pltpu.VMEM((MIX_TQ, D_MODEL), BF16),
                        pltpu.VMEM((GROUP * WINDOW, N_KV_HEADS * 2 * WINDOW), BF16)],
        compiler_params=_params("parallel"),
        name="prompt_mix",
    )(sinks, proj, proj, proj, proj, proj, x, pool_w, pool_scale, w_out)


def _sample_pool_body(hist_ref, proj_ref, d_ref, newpool_ref):
    for gi, w in enumerate(POOL_WINDOWS):
        base = gi * POOL_GROUP_DIM
        seq = [hist_ref[j, :, base:base + POOL_GROUP_DIM] for j in range(POOL_HIST)]
        seq += [proj_ref[:, t * IN_DIM + base:t * IN_DIM + base + POOL_GROUP_DIM]
                for t in range(DEC_SEQ)]
        for t in range(DEC_SEQ):
            last = POOL_HIST + t
            s = seq[last - w + 1]
            for j in range(last - w + 2, last + 1):
                s = s + seq[j]
            cnt = float(min(w, N_HIST_SAMPLE + t + 1))
            d_ref[:, t * POOL_DIM + base:t * POOL_DIM + base + POOL_GROUP_DIM] = (
                s / cnt - seq[last])
    for j in range(POOL_HIST - DEC_SEQ):
        newpool_ref[j] = hist_ref[j + DEC_SEQ]
    for t in range(DEC_SEQ):
        newpool_ref[POOL_HIST - DEC_SEQ + t] = proj_ref[:, t * IN_DIM:t * IN_DIM + POOL_DIM]


def _sample_pool(layer, state_t, proj_flat):
    return pl.pallas_call(
        _sample_pool_body,
        out_shape=(jax.ShapeDtypeStruct((DEC_BATCH, DEC_SEQ * POOL_DIM), F32),
                   jax.ShapeDtypeStruct((POOL_HIST, DEC_BATCH, POOL_DIM), F32)),
        grid=(DEC_BATCH // POOL_SB,),
        in_specs=[
            pl.BlockSpec((None, POOL_HIST, POOL_SB, POOL_DIM), lambda i: (layer, 0, i, 0)),
            pl.BlockSpec((POOL_SB, DEC_SEQ * IN_DIM), lambda i: (i, 0)),
        ],
        out_specs=(pl.BlockSpec((POOL_SB, DEC_SEQ * POOL_DIM), lambda i: (i, 0)),
                   pl.BlockSpec((POOL_HIST, POOL_SB, POOL_DIM), lambda i: (0, i, 0))),
        compiler_params=_params("parallel"),
        name="sample_pool",
    )(state_t, proj_flat)


def _sample_attn_body(*refs, update_cache):
    if update_cache:
        (sink_ref, d_ref, q_ref, kvt_ref, ck_ref, cv_ref, x_ref, pw_ref, ps_ref, wo_ref,
         o_ref, nk_ref, nv_ref, mix_ref, s_ref, p_ref) = refs
    else:
        (sink_ref, d_ref, q_ref, kvt_ref, ck_ref, cv_ref, x_ref, pw_ref, ps_ref, wo_ref,
         o_ref, mix_ref, s_ref, p_ref) = refs

    lane = lax.broadcasted_iota(jnp.int32, (KV_DIM, N_BUF), 1)
    is_old = lane < N_BUF - DEC_SEQ

    def shift_in(b, kt, vt, kt_new, vt_new):
        shift_new = (N_BUF - DEC_SEQ - b * DEC_SEQ) % N_BUF
        nk_ref[b] = jnp.where(is_old, pltpu.roll(kt, N_BUF - DEC_SEQ, axis=1),
                              pltpu.roll(kt_new, shift_new, axis=1))
        nv_ref[b] = jnp.where(is_old, pltpu.roll(vt, N_BUF - DEC_SEQ, axis=1),
                              pltpu.roll(vt_new, shift_new, axis=1))

    def attend():
        _sample_attend(sink_ref, d_ref, q_ref, kvt_ref, ck_ref, cv_ref, x_ref, pw_ref,
                       ps_ref, wo_ref, o_ref, mix_ref, s_ref, p_ref,
                       shift_in if update_cache else None)

    if not update_cache:
        attend()
        return

    phase = pl.program_id(0)

    @pl.when(phase < DEPTH - 1)
    def _():
        kt_new = kvt_ref[:KV_DIM, :]
        vt_new = kvt_ref[KV_DIM:, :]
        for b in range(ATTN_SB):
            shift_in(b, ck_ref[b], cv_ref[b], kt_new, vt_new)

    @pl.when(phase == DEPTH - 1)
    def _():
        attend()


def _sample_attend(sink_ref, d_ref, q_ref, kvt_ref, ck_ref, cv_ref, x_ref, pw_ref, ps_ref,
                   wo_ref, o_ref, mix_ref, s_ref, p_ref, shift_in):
    for gi in range(len(POOL_WINDOWS)):
        lanes = slice(gi * POOL_GROUP_DIM, (gi + 1) * POOL_GROUP_DIM)
        mix_ref[:, lanes] = _pool_linear(d_ref[:, lanes], gi, pw_ref, ps_ref).astype(BF16)

    pair = 2 * DEC_SEQ
    grp = GROUP * pair
    rows = N_KV_HEADS * grp
    new_cols = ATTN_SB * DEC_SEQ
    keys = N_BUF + new_cols
    assert keys == KV_DIM and new_cols == N_BUF
    head = _kv_head_of_lane((rows, KV_DIM))
    row = lax.broadcasted_iota(jnp.int32, (rows, keys), 0)
    row_head = row // grp
    in_pair = row % pair
    low = in_pair < DEC_SEQ
    tok = in_pair % DEC_SEQ
    col = lax.broadcasted_iota(jnp.int32, (rows, keys), 1)
    new_col = col - N_BUF
    valid_cache = (col < N_BUF) & (col > tok)
    causal_new = (new_col >= 0) & ((new_col % DEC_SEQ) <= tok)
    seq_of_col = new_col // DEC_SEQ - in_pair // DEC_SEQ
    sink = jnp.concatenate(
        [jnp.full((pair, 1), sink_ref[k * GROUP + g], F32)
         for k in range(N_KV_HEADS) for g in range(GROUP)], axis=0)
    kt_new = kvt_ref[:KV_DIM, :]
    vt_new = kvt_ref[KV_DIM:, :]
    kt_new_bf = kt_new.astype(BF16)
    vt_new_bf = vt_new.astype(BF16)
    nt = (((1,), (1,)), ((), ()))

    n_pairs = ATTN_SB // 2
    for j in range(n_pairs):
        r0 = j * pair
        q8 = q_ref[r0:r0 + pair, :]
        qs = jnp.concatenate(
            [q8[:, g * KV_DIM:(g + 1) * KV_DIM] for g in range(GROUP)], axis=0)
        qrep = jnp.concatenate([qs] * N_KV_HEADS, axis=0)
        lhs = jnp.where(head == row_head, qrep, 0.0).astype(BF16)
        scores = [jnp.dot(lhs, ck_ref[b].astype(BF16), preferred_element_type=F32)
                  for b in (2 * j, 2 * j + 1)]
        s_new = jnp.dot(lhs, kt_new_bf, preferred_element_type=F32)
        s = jnp.concatenate([jnp.where(low[:, :N_BUF], scores[0], scores[1]), s_new],
                            axis=1)
        valid = valid_cache | (causal_new & (seq_of_col == 2 * j))
        s_ref[j * rows:(j + 1) * rows, :] = jnp.where(valid, s, -jnp.inf)

    for j in range(n_pairs):
        s = s_ref[j * rows:(j + 1) * rows, :]
        m = jnp.maximum(jnp.max(s, axis=-1, keepdims=True), sink)
        p = jnp.exp(s - m)
        den = jnp.sum(p, axis=-1, keepdims=True) + jnp.exp(sink - m)
        p_ref[j * rows:(j + 1) * rows, :] = (p * (1.0 / den)).astype(BF16)

    for j in range(n_pairs):
        r0 = j * pair
        p = p_ref[j * rows:(j + 1) * rows, :]
        p_old = p[:, :N_BUF]
        outs = []
        for b in (2 * j, 2 * j + 1):
            kt = ck_ref[b]
            vt = cv_ref[b]
            outs.append(lax.dot_general(p_old, vt.astype(BF16), nt,
                                        preferred_element_type=F32))
            if shift_in is not None:
                shift_in(b, kt, vt, kt_new, vt_new)
        o = jnp.where(low, outs[0], outs[1])
        o = o + lax.dot_general(p[:, N_BUF:], vt_new_bf, nt, preferred_element_type=F32)
        o = jnp.where(head == row_head, o, 0.0)
        og = o[0:grp]
        for k in range(1, N_KV_HEADS):
            og = og + o[k * grp:(k + 1) * grp]
        for g in range(GROUP):
            mix_ref[r0:r0 + pair,
                    POOL_DIM + g * KV_DIM:POOL_DIM + (g + 1) * KV_DIM] = (
                        og[g * pair:(g + 1) * pair].astype(BF16))

    o_ref[...] = x_ref[...] + jnp.dot(mix_ref[...], wo_ref[...],
                                      preferred_element_type=F32)


def _sample_attn(layer, sinks, d_rows, proj, kvts, cache_kt, cache_vt, x, pool_w,
                 pool_scale, w_out):
    phases = kvts.shape[0]
    update_cache = phases > 1
    score_shape = (ATTN_SB // 2 * N_HEADS * 2 * DEC_SEQ, N_BUF + ATTN_ROWS)
    assert phases == 1 or (phases == DEPTH and layer == DEPTH - 1)
    layer_of = (lambda p: p) if update_cache else (lambda p: layer)
    row = lambda p, i: jnp.where(p == phases - 1, i, 0)
    cache_spec = pl.BlockSpec((None, ATTN_SB, KV_DIM, N_BUF),
                              lambda p, i: (layer_of(p), i, 0, 0))
    out_shape = [jax.ShapeDtypeStruct((SAMPLE_ROWS, D_MODEL), F32)]
    out_specs = [pl.BlockSpec((ATTN_ROWS, D_MODEL), lambda p, i: (row(p, i), 0))]
    if update_cache:
        out_shape += [jax.ShapeDtypeStruct((DEPTH, DEC_BATCH, KV_DIM, N_BUF), F32)] * 2
        out_specs += [pl.BlockSpec((None, ATTN_SB, KV_DIM, N_BUF),
                                   lambda p, i: (p, i, 0, 0))] * 2
    return pl.pallas_call(
        functools.partial(_sample_attn_body, update_cache=update_cache),
        out_shape=out_shape,
        grid=(phases, DEC_BATCH // ATTN_SB),
        in_specs=[
            pl.BlockSpec(memory_space=pltpu.SMEM),
            pl.BlockSpec((ATTN_ROWS, POOL_DIM), lambda p, i: (row(p, i), 0)),
            pl.BlockSpec((ATTN_ROWS, Q_DIM), lambda p, i: (row(p, i), 1)),
            pl.BlockSpec((None, 2 * KV_DIM, ATTN_ROWS), lambda p, i: (p, 0, i)),
            cache_spec,
            cache_spec,
            pl.BlockSpec((ATTN_ROWS, D_MODEL), lambda p, i: (row(p, i), 0)),
            pl.BlockSpec((len(POOL_WINDOWS), POOL_GROUP_DIM, POOL_GROUP_DIM),
                         lambda p, i: (0, 0, 0)),
            pl.BlockSpec((1, POOL_DIM), lambda p, i: (0, 0)),
            pl.BlockSpec((D_MODEL, D_MODEL), lambda p, i: (0, 0),
                         pipeline_mode=pl.Buffered(1)),
        ],
        out_specs=out_specs,
        scratch_shapes=[pltpu.VMEM((ATTN_ROWS, D_MODEL), BF16),
                        pltpu.VMEM(score_shape, F32),
                        pltpu.VMEM(score_shape, BF16)],
        compiler_params=_params("arbitrary", "arbitrary"),
        name="sample_attn",
    )(sinks, d_rows, proj, kvts, cache_kt, cache_vt, x, pool_w, pool_scale, w_out)


def kernel(x_prompt, x_sample, cache_k, cache_v, state_pool, norm_ffn1, ffn1_gate,
           ffn1_up, ffn1_down, norm_mix, w_in, pool_w, pool_scale, attn_sinks,
           w_out, norm_ffn2, ffn2_gate, ffn2_up, ffn2_down, final_norm):
    xp = x_prompt.reshape(PROMPT_ROWS, D_MODEL)
    xs = x_sample.reshape(SAMPLE_ROWS, D_MODEL)
    gf = final_norm.reshape(1, D_MODEL)
    state_t = jnp.transpose(state_pool, (0, 2, 1, 3))

    def to_cache_t(c):
        return jnp.transpose(c, (0, 1, 3, 4, 2)).reshape(DEPTH, DEC_BATCH, KV_DIM, N_BUF)

    def from_cache_t(c):
        c = c.reshape(DEPTH, DEC_BATCH, N_KV_HEADS, HEAD_DIM, N_BUF)
        return jnp.transpose(c, (0, 1, 4, 2, 3))

    cache_kt = to_cache_t(cache_k)
    cache_vt = to_cache_t(cache_v)
    keep = min(WINDOW, SEQ)
    assert keep >= POOL_HIST
    k0 = POOL_DIM + Q_DIM

    ffn1_w = (ffn1_gate, ffn1_up, ffn1_down)
    ffn2_w = (ffn2_gate, ffn2_up, ffn2_down)
    head, *w_bf = _ffn_head(0, xp, norm_ffn1[0].reshape(1, D_MODEL), *ffn1_w)
    kp_l, vp_l, pp_l, ps_l, kvt_l = [], [], [], [], []
    for l in range(DEPTH):
        last = l == DEPTH - 1
        pool_w_l = pool_w[l].astype(BF16)
        pool_scale_l = pool_scale[l].reshape(1, POOL_DIM)
        sinks_l = attn_sinks[l].astype(F32)

        xp, xs, w_bf, (w_in_l, w_out_l) = _ffn(
            xp, xs, norm_ffn1[l].reshape(1, D_MODEL), *w_bf, gf,
            next_w=(l,) + ffn2_w, mix_w=(l, w_in, w_out),
            head=head if l == 0 else None)
        proj_p, proj_s, kvt = _inproj(xp, xs, norm_mix[l].reshape(1, D_MODEL), w_in_l)

        tails = [proj_p[(b + 1) * SEQ - keep:(b + 1) * SEQ] for b in range(BATCH)]
        kp_l.append(jnp.stack([t[:, k0:k0 + KV_DIM] for t in tails])
                    .reshape(BATCH, keep, N_KV_HEADS, HEAD_DIM))
        vp_l.append(jnp.stack([t[:, k0 + KV_DIM:] for t in tails])
                    .reshape(BATCH, keep, N_KV_HEADS, HEAD_DIM))
        pp_l.append(jnp.stack([t[keep - POOL_HIST:, :POOL_DIM] for t in tails]))

        d_flat, new_pool = _sample_pool(
            l, state_t, proj_s.reshape(DEC_BATCH, DEC_SEQ * IN_DIM))
        ps_l.append(new_pool)

        xp = _prompt_mix(sinks_l, proj_p, xp, pool_w_l, pool_scale_l, w_out_l)
        kvt_l.append(kvt)
        outs = _sample_attn(
            l, sinks_l, d_flat.reshape(SAMPLE_ROWS, POOL_DIM), proj_s,
            jnp.stack(kvt_l) if last else kvt[None],
            cache_kt, cache_vt, xs, pool_w_l, pool_scale_l, w_out_l)
        xs = outs[0]
        if last:
            new_kt, new_vt = outs[1:]

        xp, xs, w_bf, _ = _ffn(xp, xs, norm_ffn2[l].reshape(1, D_MODEL), *w_bf, gf,
                               next_w=None if last else (l + 1,) + ffn1_w,
                               final_norm=last)

    y_prompt = xp.reshape(BATCH, SEQ, D_MODEL)
    y_sample = xs.reshape(DEC_BATCH, DEC_SEQ, D_MODEL)
    new_pool_sample = jnp.transpose(jnp.stack(ps_l), (0, 2, 1, 3))
    return (y_prompt, y_sample, jnp.stack(kp_l), jnp.stack(vp_l), jnp.stack(pp_l),
            from_cache_t(new_kt), from_cache_t(new_vt), new_pool_sample)
```

```python
import functools

import jax
import jax.numpy as jnp
from jax import lax
from jax.experimental import pallas as pl
from jax.experimental.pallas import tpu as pltpu

F32 = jnp.float32
BF16 = jnp.bfloat16

D_MODEL = 2048
BATCH = 2
SEQ = 4096
DEPTH = 2
DEC_BATCH = 128
DEC_SEQ = 4
PAST_LEN = 8192

POOL_DIM = 1024
POOL_WINDOWS = (2, 4, 8, 16)
POOL_GROUP_DIM = POOL_DIM // len(POOL_WINDOWS)
POOL_HIST = max(POOL_WINDOWS) - 1
HEAD_DIM = 64
N_HEADS = 16
N_KV_HEADS = 4
GROUP = N_HEADS // N_KV_HEADS
Q_DIM = N_HEADS * HEAD_DIM
KV_DIM = N_KV_HEADS * HEAD_DIM
IN_DIM = POOL_DIM + Q_DIM + 2 * KV_DIM
WINDOW = 128
FFN_DIM = 5632
RMS_EPS = 1e-5
ATTN_SCALE = HEAD_DIM ** -0.5
N_BUF = min(WINDOW, PAST_LEN)
N_HIST_SAMPLE = min(POOL_HIST, PAST_LEN)

PROMPT_ROWS = BATCH * SEQ
SAMPLE_ROWS = DEC_BATCH * DEC_SEQ
M_ROWS = PROMPT_ROWS + SAMPLE_ROWS

VMEM_LIMIT_BYTES = 56 * 1024 * 1024

FFN_TM = 1088
FFN_TF = 512
FFN_TILES = M_ROWS // FFN_TM
FFN_CHUNKS = FFN_DIM // FFN_TF
FFN_FULL_TILES = PROMPT_ROWS // FFN_TM
FFN_SPLIT = PROMPT_ROWS - FFN_FULL_TILES * FFN_TM
FFN_TF_HEAD = 256
WIN_ROWS = 32
WIN_BLOCKS = D_MODEL // WIN_ROWS
WOUT_BLOCKS = D_MODEL // HEAD_DIM
PROJ_TM = 512
PROJ_PROMPT_TILES = PROMPT_ROWS // PROJ_TM
MIX_TQ = 512
HALO_U = 16
ATT_CHUNK = 64
POOL_SB = 32
ATTN_SB = 32
ATTN_ROWS = ATTN_SB * DEC_SEQ

assert FFN_TILES * FFN_TM == M_ROWS and FFN_CHUNKS * FFN_TF == FFN_DIM
assert FFN_FULL_TILES + 1 == FFN_TILES and FFN_SPLIT + SAMPLE_ROWS == FFN_TM
assert FFN_SPLIT % 8 == 0 and FFN_FULL_TILES >= 1
assert FFN_CHUNKS >= 2
assert FFN_DIM % FFN_TF_HEAD == 0
assert SAMPLE_ROWS == PROJ_TM and PROJ_PROMPT_TILES * PROJ_TM == PROMPT_ROWS


def _rms(x, g):
    return x * lax.rsqrt(jnp.mean(x * x, axis=-1, keepdims=True) + RMS_EPS) * g


def _params(*sem):
    return pltpu.CompilerParams(dimension_semantics=sem,
                                vmem_limit_bytes=VMEM_LIMIT_BYTES)


def _ffn_body(*refs, final_norm, cast_next, prep_mix, t0, n_tiles, head_in):
    xp_hbm, xs_hbm, g_ref, wg_ref, wu_ref, wd_ref, gf_ref = refs[:7]
    refs = refs[7:]
    if cast_next:
        ng_ref, nu_ref, nd_ref = refs[:3]
        refs = refs[3:]
    if prep_mix:
        win_ref, wout_ref = refs[:2]
        refs = refs[2:]
    if head_in:
        head_hbm = refs[0]
        refs = refs[1:]
    op_hbm, os_hbm = refs[:2]
    refs = refs[2:]
    if cast_next:
        cg_ref, cu_ref, cd_ref = refs[:3]
        refs = refs[3:]
    if prep_mix:
        cwin_ref, cwout_ref = refs[:2]
        refs = refs[2:]
    acc_ref, xn_ref, in_sem, out_sem = refs[:4]
    head_sem = refs[4] if head_in else None

    i = pl.program_id(0)
    f = pl.program_id(1)
    slot = i % 2
    tile = i + t0

    def head_copy():
        return pltpu.make_async_copy(head_hbm, op_hbm.at[pl.ds(0, t0 * FFN_TM)], head_sem)

    def copies(p_hbm, s_hbm, tile, s, sem, mixed, to_vmem):
        if mixed:
            pairs = [(p_hbm.at[pl.ds(FFN_FULL_TILES * FFN_TM, FFN_SPLIT)],
                      acc_ref.at[s, pl.ds(0, FFN_SPLIT)]),
                     (s_hbm, acc_ref.at[s, pl.ds(FFN_SPLIT, SAMPLE_ROWS)])]
        else:
            pairs = [(p_hbm.at[pl.ds(tile * FFN_TM, FFN_TM)], acc_ref.at[s])]
        return [pltpu.make_async_copy(h, v, sem.at[s]) if to_vmem
                else pltpu.make_async_copy(v, h, sem.at[s]) for h, v in pairs]

    def on_tile(tile, s, op, to_vmem):
        p_hbm, s_hbm, sem = ((xp_hbm, xs_hbm, in_sem) if to_vmem
                             else (op_hbm, os_hbm, out_sem))

        @pl.when(tile < FFN_FULL_TILES)
        def _():
            for c in copies(p_hbm, s_hbm, tile, s, sem, False, to_vmem):
                op(c)

        @pl.when(tile == FFN_FULL_TILES)
        def _():
            for c in copies(p_hbm, s_hbm, tile, s, sem, True, to_vmem):
                op(c)

    start = lambda c: c.start()
    wait = lambda c: c.wait()

    @pl.when((i == 0) & (f == 0))
    def _():
        on_tile(tile, slot, start, True)
        if head_in:
            head_copy().start()

    @pl.when(f == 0)
    def _():
        on_tile(tile, slot, wait, True)
        xn_ref[...] = _rms(acc_ref[slot], g_ref[...]).astype(BF16)

    a = xn_ref[...]
    gate = jnp.dot(a, wg_ref[...], preferred_element_type=F32)
    if cast_next:
        cg_ref[...] = ng_ref[...].astype(BF16)
        cu_ref[...] = nu_ref[...].astype(BF16)
    up = jnp.dot(a, wu_ref[...], preferred_element_type=F32)
    if cast_next:
        cd_ref[...] = nd_ref[...].astype(BF16)
    if prep_mix:
        _prep_w_in(win_ref, cwin_ref)
        cwout_ref[...] = wout_ref[...].astype(BF16)
    h = (gate * jax.nn.sigmoid(gate) * up).astype(BF16)
    acc_ref[slot] += 0.5 * jnp.dot(h, wd_ref[...], preferred_element_type=F32)

    @pl.when(f == 1)
    def _():
        @pl.when(i >= 1)
        def _():
            on_tile(tile - 1, 1 - slot, wait, False)

        @pl.when(i + 1 < n_tiles)
        def _():
            on_tile(tile + 1, 1 - slot, start, True)

    @pl.when(f == FFN_CHUNKS - 1)
    def _():
        if final_norm:
            acc_ref[slot] = _rms(acc_ref[slot], gf_ref[...])
        on_tile(tile, slot, start, False)

        @pl.when(i == n_tiles - 1)
        def _():
            on_tile(tile, slot, wait, False)
            if head_in:
                head_copy().wait()


def _prep_w_in(w_ref, o_ref):
    k0 = POOL_DIM + Q_DIM
    o_ref[:, :POOL_DIM] = w_ref[:, :POOL_DIM].astype(BF16)
    o_ref[:, k0:] = w_ref[:, k0:].astype(BF16)
    lane = lax.broadcasted_iota(jnp.int32, (WIN_ROWS, 2 * HEAD_DIM), 1)
    for g in range(GROUP):
        for kp in range(N_KV_HEADS // 2):
            src = [POOL_DIM + (2 * kp + j) * KV_DIM + (g // 2) * 2 * HEAD_DIM
                   for j in (0, 1)]
            lo, hi = [w_ref[:, c:c + 2 * HEAD_DIM] for c in src]
            if g % 2 == 0:
                hi = pltpu.roll(hi, HEAD_DIM, axis=1)
            else:
                lo = pltpu.roll(lo, HEAD_DIM, axis=1)
            dst = POOL_DIM + g * KV_DIM + kp * 2 * HEAD_DIM
            o_ref[:, dst:dst + 2 * HEAD_DIM] = (
                jnp.where(lane < HEAD_DIM, lo, hi) * ATTN_SCALE).astype(BF16)


def _w_out_src_block(out_blk):
    pool_blocks = POOL_DIM // HEAD_DIM
    h = out_blk - pool_blocks
    src = pool_blocks + (h % N_KV_HEADS) * GROUP + h // N_KV_HEADS
    return jnp.where(out_blk < pool_blocks, out_blk, src)


def _ffn(xp, xs, g, wg, wu, wd, gf, next_w=None, mix_w=None, *, final_norm=False,
         head=None):
    cast_next = next_w is not None
    prep_mix = mix_w is not None
    head_in = head is not None
    t0 = head.shape[0] // FFN_TM if head_in else 0
    n_tiles = FFN_TILES - t0
    cast_blocks = 1 << (n_tiles.bit_length() - 1)
    cast_rows = D_MODEL // cast_blocks
    win_steps = pl.cdiv(WIN_BLOCKS, n_tiles)
    wout_steps = pl.cdiv(WOUT_BLOCKS, n_tiles)
    assert max(win_steps, wout_steps) <= FFN_CHUNKS
    cast_blk = lambda i: jnp.minimum(i, cast_blocks - 1)
    cast_chunk = lambda i, f: jnp.where(i < cast_blocks, f, FFN_CHUNKS - 1)
    vec = pl.BlockSpec((1, D_MODEL), lambda i, f: (0, 0))
    in_specs = [
        pl.BlockSpec(memory_space=pl.ANY),
        pl.BlockSpec(memory_space=pl.ANY),
        vec,
        pl.BlockSpec((D_MODEL, FFN_TF), lambda i, f: (0, f)),
        pl.BlockSpec((D_MODEL, FFN_TF), lambda i, f: (0, f)),
        pl.BlockSpec((FFN_TF, D_MODEL), lambda i, f: (f, 0)),
        vec,
    ]
    out_shape = [jax.ShapeDtypeStruct((PROMPT_ROWS, D_MODEL), F32),
                 jax.ShapeDtypeStruct((SAMPLE_ROWS, D_MODEL), F32)]
    out_specs = [pl.BlockSpec(memory_space=pl.ANY), pl.BlockSpec(memory_space=pl.ANY)]
    args = [xp, xs, g, wg, wu, wd, gf]
    if cast_next:
        nl = next_w[0]
        in_specs += [
            pl.BlockSpec((None, cast_rows, FFN_TF),
                         lambda i, f: (nl, cast_blk(i), cast_chunk(i, f))),
            pl.BlockSpec((None, cast_rows, FFN_TF),
                         lambda i, f: (nl, cast_blk(i), cast_chunk(i, f))),
            pl.BlockSpec((None, FFN_TF, cast_rows),
                         lambda i, f: (nl, cast_chunk(i, f), cast_blk(i))),
        ]
        out_shape += [jax.ShapeDtypeStruct((D_MODEL, FFN_DIM), BF16),
                      jax.ShapeDtypeStruct((D_MODEL, FFN_DIM), BF16),
                      jax.ShapeDtypeStruct((FFN_DIM, D_MODEL), BF16)]
        out_specs += [
            pl.BlockSpec((cast_rows, FFN_TF), lambda i, f: (cast_blk(i), cast_chunk(i, f))),
            pl.BlockSpec((cast_rows, FFN_TF), lambda i, f: (cast_blk(i), cast_chunk(i, f))),
            pl.BlockSpec((FFN_TF, cast_rows), lambda i, f: (cast_chunk(i, f), cast_blk(i))),
        ]
        args += list(next_w[1:])
    if prep_mix:
        ml = mix_w[0]
        win_blk = lambda i, f: jnp.minimum(
            i * win_steps + jnp.minimum(f, win_steps - 1), WIN_BLOCKS - 1)
        wout_blk = lambda i, f: jnp.minimum(
            i * wout_steps + jnp.minimum(f, wout_steps - 1), WOUT_BLOCKS - 1)
        in_specs += [
            pl.BlockSpec((None, WIN_ROWS, IN_DIM), lambda i, f: (ml, win_blk(i, f), 0)),
            pl.BlockSpec((None, HEAD_DIM, D_MODEL),
                         lambda i, f: (ml, _w_out_src_block(wout_blk(i, f)), 0)),
        ]
        out_shape += [jax.ShapeDtypeStruct((D_MODEL, IN_DIM), BF16),
                      jax.ShapeDtypeStruct((D_MODEL, D_MODEL), BF16)]
        out_specs += [
            pl.BlockSpec((WIN_ROWS, IN_DIM), lambda i, f: (win_blk(i, f), 0)),
            pl.BlockSpec((HEAD_DIM, D_MODEL), lambda i, f: (wout_blk(i, f), 0)),
        ]
        args += list(mix_w[1:])
    scratch = [pltpu.VMEM((2, FFN_TM, D_MODEL), F32),
               pltpu.VMEM((FFN_TM, D_MODEL), BF16),
               pltpu.SemaphoreType.DMA((2,)),
               pltpu.SemaphoreType.DMA((2,))]
    if head_in:
        in_specs.append(pl.BlockSpec(memory_space=pl.ANY))
        args.append(head)
        scratch.append(pltpu.SemaphoreType.DMA(()))
    outs = pl.pallas_call(
        functools.partial(_ffn_body, final_norm=final_norm, cast_next=cast_next,
                          prep_mix=prep_mix, t0=t0, n_tiles=n_tiles, head_in=head_in),
        out_shape=out_shape,
        grid=(n_tiles, FFN_CHUNKS),
        in_specs=in_specs,
        out_specs=out_specs,
        scratch_shapes=scratch,
        compiler_params=_params("arbitrary", "arbitrary"),
        name="ffn_final" if final_norm else "ffn",
    )(*args)
    n_cast = 3 if cast_next else 0
    return outs[0], outs[1], tuple(outs[2:2 + n_cast]), tuple(outs[2 + n_cast:])


def _ffn_head_body(x_ref, g_ref, wg_ref, wu_ref, wd_ref, o_ref, og_ref, ou_ref, od_ref,
                   xn_ref, wgs_ref, wus_ref, wds_ref):
    step = pl.program_id(0)
    slot = step % 2

    @pl.when(step == 0)
    def _():
        x = x_ref[...]
        xn_ref[...] = _rms(x, g_ref[...]).astype(BF16)
        o_ref[...] = x
        wgs_ref[1] = jnp.zeros(wgs_ref.shape[1:], BF16)
        wus_ref[1] = jnp.zeros(wus_ref.shape[1:], BF16)
        wds_ref[1] = jnp.zeros(wds_ref.shape[1:], BF16)

    a = xn_ref[...]
    wg = wg_ref[...].astype(BF16)
    wgs_ref[slot] = wg
    og_ref[...] = wg
    gate = jnp.dot(a, wgs_ref[1 - slot], preferred_element_type=F32)
    wu = wu_ref[...].astype(BF16)
    wus_ref[slot] = wu
    ou_ref[...] = wu
    up = jnp.dot(a, wus_ref[1 - slot], preferred_element_type=F32)
    wd = wd_ref[...].astype(BF16)
    wds_ref[slot] = wd
    od_ref[...] = wd
    h = (gate * jax.nn.sigmoid(gate) * up).astype(BF16)
    o_ref[...] += 0.5 * jnp.dot(h, wds_ref[1 - slot], preferred_element_type=F32)


def _ffn_head(layer, xp, g, gate, up, down):
    tf = FFN_TF_HEAD
    n_chunks = FFN_DIM // tf
    fixed = lambda s: (0, 0)
    chunk = lambda s: jnp.minimum(s, n_chunks - 1)
    return pl.pallas_call(
        _ffn_head_body,
        out_shape=(jax.ShapeDtypeStruct((FFN_TM, D_MODEL), F32),
                   jax.ShapeDtypeStruct((D_MODEL, FFN_DIM), BF16),
                   jax.ShapeDtypeStruct((D_MODEL, FFN_DIM), BF16),
                   jax.ShapeDtypeStruct((FFN_DIM, D_MODEL), BF16)),
        grid=(n_chunks + 1,),
        in_specs=[
            pl.BlockSpec((FFN_TM, D_MODEL), fixed, pipeline_mode=pl.Buffered(1)),
            pl.BlockSpec((1, D_MODEL), fixed),
            pl.BlockSpec((None, D_MODEL, tf), lambda s: (layer, 0, chunk(s))),
            pl.BlockSpec((None, D_MODEL, tf), lambda s: (layer, 0, chunk(s))),
            pl.BlockSpec((None, tf, D_MODEL), lambda s: (layer, chunk(s), 0)),
        ],
        out_specs=(pl.BlockSpec((FFN_TM, D_MODEL), fixed),
                   pl.BlockSpec((D_MODEL, tf), lambda s: (0, chunk(s))),
                   pl.BlockSpec((D_MODEL, tf), lambda s: (0, chunk(s))),
                   pl.BlockSpec((tf, D_MODEL), lambda s: (chunk(s), 0))),
        scratch_shapes=[pltpu.VMEM((FFN_TM, D_MODEL), BF16),
                        pltpu.VMEM((2, D_MODEL, tf), BF16),
                        pltpu.VMEM((2, D_MODEL, tf), BF16),
                        pltpu.VMEM((2, tf, D_MODEL), BF16)],
        compiler_params=_params("arbitrary"),
        name="ffn_head",
    )(xp, g, gate, up, down)


def _inproj_body(xp_ref, xs_ref, g_ref, w_ref, pp_ref, ps_ref, kvt_ref):
    i = pl.program_id(0)

    @pl.when(i < PROJ_PROMPT_TILES)
    def _():
        xn = _rms(xp_ref[...], g_ref[...]).astype(BF16)
        pp_ref[...] = jnp.dot(xn, w_ref[...], preferred_element_type=F32)

    @pl.when(i == PROJ_PROMPT_TILES)
    def _():
        xn = _rms(xs_ref[...], g_ref[...]).astype(BF16)
        proj = jnp.dot(xn, w_ref[...], preferred_element_type=F32)
        ps_ref[...] = proj
        kvt_ref[...] = proj[:, POOL_DIM + Q_DIM:].T


def _inproj(xp, xs, g, w):
    prompt_tile = lambda i: (jnp.minimum(i, PROJ_PROMPT_TILES - 1), 0)
    fixed = lambda i: (0, 0)
    return pl.pallas_call(
        _inproj_body,
        out_shape=(jax.ShapeDtypeStruct((PROMPT_ROWS, IN_DIM), F32),
                   jax.ShapeDtypeStruct((SAMPLE_ROWS, IN_DIM), F32),
                   jax.ShapeDtypeStruct((2 * KV_DIM, SAMPLE_ROWS), F32)),
        grid=(PROJ_PROMPT_TILES + 1,),
        in_specs=[
            pl.BlockSpec((PROJ_TM, D_MODEL), prompt_tile),
            pl.BlockSpec((PROJ_TM, D_MODEL), fixed),
            pl.BlockSpec((1, D_MODEL), fixed),
            pl.BlockSpec((D_MODEL, IN_DIM), fixed),
        ],
        out_specs=(pl.BlockSpec((PROJ_TM, IN_DIM), prompt_tile),
                   pl.BlockSpec((PROJ_TM, IN_DIM), fixed),
                   pl.BlockSpec((2 * KV_DIM, SAMPLE_ROWS), fixed)),
        compiler_params=_params("arbitrary"),
        name="inproj",
    )(xp, xs, g, w)


def _kv_head_of_lane(shape):
    return lax.broadcasted_iota(jnp.int32, shape, 1) // HEAD_DIM


def _pool_linear(d, gi, pw_ref, ps_ref):
    lanes = slice(gi * POOL_GROUP_DIM, (gi + 1) * POOL_GROUP_DIM)
    y = jnp.dot(d.astype(BF16), pw_ref[gi], preferred_element_type=F32)
    return y * ps_ref[:, lanes]


def _prompt_mix_body(sink_ref, u_ref, q_ref, kv_ref, kvh_ref, uh_ref, x_ref,
                     pw_ref, ps_ref, wo_ref, o_ref, mix_ref, s_ref, p_ref):
    tiles_per_seq = SEQ // MIX_TQ
    tile = pl.program_id(0) % tiles_per_seq
    first = tile == 0

    lo = jnp.where(first, WINDOW, 0)
    keep_u = jnp.full((HALO_U, POOL_DIM), lo, jnp.int32) == 0
    keep_kv = jnp.full((WINDOW, 2 * KV_DIM), lo, jnp.int32) == 0

    def pool_mixer():
        uh = jnp.where(keep_u, uh_ref[...], 0.0)
        pos = tile * MIX_TQ + lax.broadcasted_iota(jnp.int32, (MIX_TQ, 1), 0)
        for gi, w in enumerate(POOL_WINDOWS):
            lanes = slice(gi * POOL_GROUP_DIM, (gi + 1) * POOL_GROUP_DIM)
            xg = jnp.concatenate([uh[:, lanes], u_ref[:, lanes]], axis=0)
            s = xg
            sh = 1
            while sh < w:
                s = s + pltpu.roll(s, sh, axis=0)
                sh *= 2
            cnt = jnp.minimum(w, pos + 1).astype(F32)
            d = s[HALO_U:] / cnt - xg[HALO_U:]
            mix_ref[:, lanes] = _pool_linear(d, gi, pw_ref, ps_ref).astype(BF16)

    pool_mixer()

    kvh = jnp.where(keep_kv, kvh_ref[...], 0.0)
    k_all = jnp.concatenate([kvh[:, :KV_DIM], kv_ref[:, :KV_DIM]], axis=0)
    v_all = jnp.concatenate([kvh[:, KV_DIM:], kv_ref[:, KV_DIM:]], axis=0)
    head = _kv_head_of_lane((2 * WINDOW, KV_DIM))
    qi = lax.broadcasted_iota(jnp.int32, (ATT_CHUNK, 2 * WINDOW), 0)
    kj = lax.broadcasted_iota(jnp.int32, (ATT_CHUNK, 2 * WINDOW), 1)
    n_blocks = MIX_TQ // WINDOW
    rows = GROUP * WINDOW
    for n in range(n_blocks):
        kw = k_all[n * WINDOW:(n + 2) * WINDOW]
        qn = q_ref[n * WINDOW:(n + 1) * WINDOW, :].astype(BF16)
        qs = jnp.concatenate(
            [qn[:, g * KV_DIM:(g + 1) * KV_DIM] for g in range(GROUP)], axis=0)
        for k in range(N_KV_HEADS):
            kk = jnp.where(head == k, kw, 0.0).astype(BF16)
            s_ref[n * rows:(n + 1) * rows, k * 2 * WINDOW:(k + 1) * 2 * WINDOW] = (
                lax.dot_general(qs, kk, (((1,), (1,)), ((), ())),
                                preferred_element_type=F32))

    for n in range(n_blocks):
        biases = []
        for c in range(WINDOW // ATT_CHUNK):
            q_pos = qi + c * ATT_CHUNK
            valid = (kj > q_pos) & (kj <= q_pos + WINDOW)
            if n == 0:
                valid = valid & (kj >= lo)
            biases.append(jnp.where(valid, 0.0, -jnp.inf))
        for k in range(N_KV_HEADS):
            cols = slice(k * 2 * WINDOW, (k + 1) * 2 * WINDOW)
            for g in range(GROUP):
                sink = sink_ref[k * GROUP + g]
                for c in range(WINDOW // ATT_CHUNK):
                    r0 = n * rows + g * WINDOW + c * ATT_CHUNK
                    sc = s_ref[r0:r0 + ATT_CHUNK, cols] + biases[c]
                    m = jnp.maximum(jnp.max(sc, axis=-1, keepdims=True), sink)
                    p = jnp.exp(sc - m)
                    den = jnp.sum(p, axis=-1, keepdims=True) + jnp.exp(sink - m)
                    p_ref[r0:r0 + ATT_CHUNK, cols] = (p * (1.0 / den)).astype(BF16)

    for n in range(n_blocks):
        vw = v_all[n * WINDOW:(n + 2) * WINDOW]
        vcat = jnp.concatenate(
            [jnp.where(head == k, vw, 0.0).astype(BF16)
             for k in range(N_KV_HEADS)], axis=0)
        o = jnp.dot(p_ref[n * rows:(n + 1) * rows, :], vcat, preferred_element_type=F32)
        blk = slice(n * WINDOW, (n + 1) * WINDOW)
        for g in range(GROUP):
            mix_ref[blk, POOL_DIM + g * KV_DIM:POOL_DIM + (g + 1) * KV_DIM] = (
                o[g * WINDOW:(g + 1) * WINDOW].astype(BF16))

    o_ref[...] = x_ref[...] + jnp.dot(mix_ref[...], wo_ref[...],
                                      preferred_element_type=F32)


def _prompt_mix(sinks, proj, x, pool_w, pool_scale, w_out):
    blocks_kv = MIX_TQ // WINDOW
    blocks_u = MIX_TQ // HALO_U
    return pl.pallas_call(
        _prompt_mix_body,
        out_shape=jax.ShapeDtypeStruct((PROMPT_ROWS, D_MODEL), F32),
        grid=(PROMPT_ROWS // MIX_TQ,),
        in_specs=[
            pl.BlockSpec(memory_space=pltpu.SMEM),
            pl.BlockSpec((MIX_TQ, POOL_DIM), lambda i: (i, 0)),
            pl.BlockSpec((MIX_TQ, Q_DIM), lambda i: (i, 1)),
            pl.BlockSpec((MIX_TQ, 2 * KV_DIM), lambda i: (i, 4)),
            pl.BlockSpec((WINDOW, 2 * KV_DIM),
                         lambda i: (jnp.maximum(i * blocks_kv - 1, 0), 4)),
            pl.BlockSpec((HALO_U, POOL_DIM),
                         lambda i: (jnp.maximum(i * blocks_u - 1, 0), 0)),
            pl.BlockSpec((MIX_TQ, D_MODEL), lambda i: (i, 0)),
            pl.BlockSpec((len(POOL_WINDOWS), POOL_GROUP_DIM, POOL_GROUP_DIM),
                         lambda i: (0, 0, 0)),
            pl.BlockSpec((1, POOL_DIM), lambda i: (0, 0)),
            pl.BlockSpec((D_MODEL, D_MODEL), lambda i: (0, 0)),
        ],
        out_specs=pl.BlockSpec((MIX_TQ, D_MODEL), lambda i: (i, 0)),
        scratch_shapes=[pltpu.VMEM((MIX_TQ, D_MODEL), BF16),
                        pltpu.VMEM((GROUP * MIX_TQ, N_KV_HEADS * 2 * WINDOW), F32),
                        pltpu.VMEM((GROUP * MIX_TQ, N_KV_HEADS * 2 * WINDOW), BF16)],
        compiler_params=_params("parallel"),
        name="prompt_mix",
    )(sinks, proj, proj, proj, proj, proj, x, pool_w, pool_scale, w_out)


def _sample_pool_body(hist_ref, proj_ref, d_ref, newpool_ref):
    for gi, w in enumerate(POOL_WINDOWS):
        base = gi * POOL_GROUP_DIM
        seq = [hist_ref[j, :, base:base + POOL_GROUP_DIM] for j in range(POOL_HIST)]
        seq += [proj_ref[:, t * IN_DIM + base:t * IN_DIM + base + POOL_GROUP_DIM]
                for t in range(DEC_SEQ)]
        for t in range(DEC_SEQ):
            last = POOL_HIST + t
            s = seq[last - w + 1]
            for j in range(last - w + 2, last + 1):
                s = s + seq[j]
            cnt = float(min(w, N_HIST_SAMPLE + t + 1))
            d_ref[:, t * POOL_DIM + base:t * POOL_DIM + base + POOL_GROUP_DIM] = (
                s / cnt - seq[last])
    for j in range(POOL_HIST - DEC_SEQ):
        newpool_ref[j] = hist_ref[j + DEC_SEQ]
    for t in range(DEC_SEQ):
        newpool_ref[POOL_HIST - DEC_SEQ + t] = proj_ref[:, t * IN_DIM:t * IN_DIM + POOL_DIM]


def _sample_pool(layer, state_t, proj_flat):
    return pl.pallas_call(
        _sample_pool_body,
        out_shape=(jax.ShapeDtypeStruct((DEC_BATCH, DEC_SEQ * POOL_DIM), F32),
                   jax.ShapeDtypeStruct((POOL_HIST, DEC_BATCH, POOL_DIM), F32)),
        grid=(DEC_BATCH // POOL_SB,),
        in_specs=[
            pl.BlockSpec((None, POOL_HIST, POOL_SB, POOL_DIM), lambda i: (layer, 0, i, 0)),
            pl.BlockSpec((POOL_SB, DEC_SEQ * IN_DIM), lambda i: (i, 0)),
        ],
        out_specs=(pl.BlockSpec((POOL_SB, DEC_SEQ * POOL_DIM), lambda i: (i, 0)),
                   pl.BlockSpec((POOL_HIST, POOL_SB, POOL_DIM), lambda i: (0, i, 0))),
        compiler_params=_params("parallel"),
        name="sample_pool",
    )(state_t, proj_flat)


def _sample_attn_body(*refs, update_cache):
    if update_cache:
        (sink_ref, d_ref, q_ref, kvt_ref, ck_ref, cv_ref, x_ref, pw_ref, ps_ref, wo_ref,
         o_ref, nk_ref, nv_ref, mix_ref, s_ref, p_ref) = refs
    else:
        (sink_ref, d_ref, q_ref, kvt_ref, ck_ref, cv_ref, x_ref, pw_ref, ps_ref, wo_ref,
         o_ref, mix_ref, s_ref, p_ref) = refs

    lane = lax.broadcasted_iota(jnp.int32, (KV_DIM, N_BUF), 1)
    is_old = lane < N_BUF - DEC_SEQ

    def shift_in(b, kt, vt, kt_new, vt_new):
        shift_new = (N_BUF - DEC_SEQ - b * DEC_SEQ) % N_BUF
        nk_ref[b] = jnp.where(is_old, pltpu.roll(kt, N_BUF - DEC_SEQ, axis=1),
                              pltpu.roll(kt_new, shift_new, axis=1))
        nv_ref[b] = jnp.where(is_old, pltpu.roll(vt, N_BUF - DEC_SEQ, axis=1),
                              pltpu.roll(vt_new, shift_new, axis=1))

    def attend():
        _sample_attend(sink_ref, d_ref, q_ref, kvt_ref, ck_ref, cv_ref, x_ref, pw_ref,
                       ps_ref, wo_ref, o_ref, mix_ref, s_ref, p_ref,
                       shift_in if update_cache else None)

    if not update_cache:
        attend()
        return

    phase = pl.program_id(0)

    @pl.when(phase < DEPTH - 1)
    def _():
        kt_new = kvt_ref[:KV_DIM, :]
        vt_new = kvt_ref[KV_DIM:, :]
        for b in range(ATTN_SB):
            shift_in(b, ck_ref[b], cv_ref[b], kt_new, vt_new)

    @pl.when(phase == DEPTH - 1)
    def _():
        attend()


def _sample_attend(sink_ref, d_ref, q_ref, kvt_ref, ck_ref, cv_ref, x_ref, pw_ref, ps_ref,
                   wo_ref, o_ref, mix_ref, s_ref, p_ref, shift_in):
    for gi in range(len(POOL_WINDOWS)):
        lanes = slice(gi * POOL_GROUP_DIM, (gi + 1) * POOL_GROUP_DIM)
        mix_ref[:, lanes] = _pool_linear(d_ref[:, lanes], gi, pw_ref, ps_ref).astype(BF16)

    pair = 2 * DEC_SEQ
    grp = GROUP * pair
    rows = N_KV_HEADS * grp
    new_cols = ATTN_SB * DEC_SEQ
    keys = N_BUF + new_cols
    assert keys == KV_DIM and new_cols == N_BUF
    head = _kv_head_of_lane((rows, KV_DIM))
    row = lax.broadcasted_iota(jnp.int32, (rows, keys), 0)
    row_head = row // grp
    in_pair = row % pair
    low = in_pair < DEC_SEQ
    tok = in_pair % DEC_SEQ
    col = lax.broadcasted_iota(jnp.int32, (rows, keys), 1)
    new_col = col - N_BUF
    valid_cache = (col < N_BUF) & (col > tok)
    causal_new = (new_col >= 0) & ((new_col % DEC_SEQ) <= tok)
    seq_of_col = new_col // DEC_SEQ - in_pair // DEC_SEQ
    sink = jnp.concatenate(
        [jnp.full((pair, 1), sink_ref[k * GROUP + g], F32)
         for k in range(N_KV_HEADS) for g in range(GROUP)], axis=0)
    kt_new = kvt_ref[:KV_DIM, :]
    vt_new = kvt_ref[KV_DIM:, :]
    kt_new_bf = kt_new.astype(BF16)
    vt_new_bf = vt_new.astype(BF16)
    nt = (((1,), (1,)), ((), ()))

    n_pairs = ATTN_SB // 2
    for j in range(n_pairs):
        r0 = j * pair
        q8 = q_ref[r0:r0 + pair, :]
        qs = jnp.concatenate(
            [q8[:, g * KV_DIM:(g + 1) * KV_DIM] for g in range(GROUP)], axis=0)
        qrep = jnp.concatenate([qs] * N_KV_HEADS, axis=0)
        lhs = jnp.where(head == row_head, qrep, 0.0).astype(BF16)
        scores = [jnp.dot(lhs, ck_ref[b].astype(BF16), preferred_element_type=F32)
                  for b in (2 * j, 2 * j + 1)]
        s_new = jnp.dot(lhs, kt_new_bf, preferred_element_type=F32)
        s = jnp.concatenate([jnp.where(low[:, :N_BUF], scores[0], scores[1]), s_new],
                            axis=1)
        valid = valid_cache | (causal_new & (seq_of_col == 2 * j))
        s_ref[j * rows:(j + 1) * rows, :] = jnp.where(valid, s, -jnp.inf)

    for j in range(n_pairs):
        s = s_ref[j * rows:(j + 1) * rows, :]
        m = jnp.maximum(jnp.max(s, axis=-1, keepdims=True), sink)
        p = jnp.exp(s - m)
        den = jnp.sum(p, axis=-1, keepdims=True) + jnp.exp(sink - m)
        p_ref[j * rows:(j + 1) * rows, :] = (p * (1.0 / den)).astype(BF16)

    for j in range(n_pairs):
        r0 = j * pair
        p = p_ref[j * rows:(j + 1) * rows, :]
        p_old = p[:, :N_BUF]
        outs = []
        for b in (2 * j, 2 * j + 1):
            kt = ck_ref[b]
            vt = cv_ref[b]
            outs.append(lax.dot_general(p_old, vt.astype(BF16), nt,
                                        preferred_element_type=F32))
            if shift_in is not None:
                shift_in(b, kt, vt, kt_new, vt_new)
        o = jnp.where(low, outs[0], outs[1])
        o = o + lax.dot_general(p[:, N_BUF:], vt_new_bf, nt, preferred_element_type=F32)
        o = jnp.where(head == row_head, o, 0.0)
        og = o[0:grp]
        for k in range(1, N_KV_HEADS):
            og = og + o[k * grp:(k + 1) * grp]
        for g in range(GROUP):
            mix_ref[r0:r0 + pair,
                    POOL_DIM + g * KV_DIM:POOL_DIM + (g + 1) * KV_DIM] = (
                        og[g * pair:(g + 1) * pair].astype(BF16))

    o_ref[...] = x_ref[...] + jnp.dot(mix_ref[...], wo_ref[...],
                                      preferred_element_type=F32)


def _sample_attn(layer, sinks, d_rows, proj, kvts, cache_kt, cache_vt, x, pool_w,
                 pool_scale, w_out):
    phases = kvts.shape[0]
    update_cache = phases > 1
    score_shape = (ATTN_SB // 2 * N_HEADS * 2 * DEC_SEQ, N_BUF + ATTN_ROWS)
    assert phases == 1 or (phases == DEPTH and layer == DEPTH - 1)
    layer_of = (lambda p: p) if update_cache else (lambda p: layer)
    row = lambda p, i: jnp.where(p == phases - 1, i, 0)
    cache_spec = pl.BlockSpec((None, ATTN_SB, KV_DIM, N_BUF),
                              lambda p, i: (layer_of(p), i, 0, 0))
    out_shape = [jax.ShapeDtypeStruct((SAMPLE_ROWS, D_MODEL), F32)]
    out_specs = [pl.BlockSpec((ATTN_ROWS, D_MODEL), lambda p, i: (row(p, i), 0))]
    if update_cache:
        out_shape += [jax.ShapeDtypeStruct((DEPTH, DEC_BATCH, KV_DIM, N_BUF), F32)] * 2
        out_specs += [pl.BlockSpec((None, ATTN_SB, KV_DIM, N_BUF),
                                   lambda p, i: (p, i, 0, 0))] * 2
    return pl.pallas_call(
        functools.partial(_sample_attn_body, update_cache=update_cache),
        out_shape=out_shape,
        grid=(phases, DEC_BATCH // ATTN_SB),
        in_specs=[
            pl.BlockSpec(memory_space=pltpu.SMEM),
            pl.BlockSpec((ATTN_ROWS, POOL_DIM), lambda p, i: (row(p, i), 0)),
            pl.BlockSpec((ATTN_ROWS, Q_DIM), lambda p, i: (row(p, i), 1)),
            pl.BlockSpec((None, 2 * KV_DIM, ATTN_ROWS), lambda p, i: (p, 0, i)),
            cache_spec,
            cache_spec,
            pl.BlockSpec((ATTN_ROWS, D_MODEL), lambda p, i: (row(p, i), 0)),
            pl.BlockSpec((len(POOL_WINDOWS), POOL_GROUP_DIM, POOL_GROUP_DIM),
                         lambda p, i: (0, 0, 0)),
            pl.BlockSpec((1, POOL_DIM), lambda p, i: (0, 0)),
            pl.BlockSpec((D_MODEL, D_MODEL), lambda p, i: (0, 0),
                         pipeline_mode=pl.Buffered(1)),
        ],
        out_specs=out_specs,
        scratch_shapes=[pltpu.VMEM((ATTN_ROWS, D_MODEL), BF16),
                        pltpu.VMEM(score_shape, F32),
                        pltpu.VMEM(score_shape, BF16)],
        compiler_params=_params("arbitrary", "arbitrary"),
        name="sample_attn",
    )(sinks, d_rows, proj, kvts, cache_kt, cache_vt, x, pool_w, pool_scale, w_out)


def kernel(x_prompt, x_sample, cache_k, cache_v, state_pool, norm_ffn1, ffn1_gate,
           ffn1_up, ffn1_down, norm_mix, w_in, pool_w, pool_scale, attn_sinks,
           w_out, norm_ffn2, ffn2_gate, ffn2_up, ffn2_down, final_norm):
    xp = x_prompt.reshape(PROMPT_ROWS, D_MODEL)
    xs = x_sample.reshape(SAMPLE_ROWS, D_MODEL)
    gf = final_norm.reshape(1, D_MODEL)
    state_t = jnp.transpose(state_pool, (0, 2, 1, 3))

    def to_cache_t(c):
        return jnp.transpose(c, (0, 1, 3, 4, 2)).reshape(DEPTH, DEC_BATCH, KV_DIM, N_BUF)

    def from_cache_t(c):
        c = c.reshape(DEPTH, DEC_BATCH, N_KV_HEADS, HEAD_DIM, N_BUF)
        return jnp.transpose(c, (0, 1, 4, 2, 3))

    cache_kt = to_cache_t(cache_k)
    cache_vt = to_cache_t(cache_v)
    keep = min(WINDOW, SEQ)
    assert keep >= POOL_HIST
    k0 = POOL_DIM + Q_DIM

    ffn1_w = (ffn1_gate, ffn1_up, ffn1_down)
    ffn2_w = (ffn2_gate, ffn2_up, ffn2_down)
    head, *w_bf = _ffn_head(0, xp, norm_ffn1[0].reshape(1, D_MODEL), *ffn1_w)
    kp_l, vp_l, pp_l, ps_l, kvt_l = [], [], [], [], []
    for l in range(DEPTH):
        last = l == DEPTH - 1
        pool_w_l = pool_w[l].astype(BF16)
        pool_scale_l = pool_scale[l].reshape(1, POOL_DIM)
        sinks_l = attn_sinks[l].astype(F32)

        xp, xs, w_bf, (w_in_l, w_out_l) = _ffn(
            xp, xs, norm_ffn1[l].reshape(1, D_MODEL), *w_bf, gf,
            next_w=(l,) + ffn2_w, mix_w=(l, w_in, w_out),
            head=head if l == 0 else None)
        proj_p, proj_s, kvt = _inproj(xp, xs, norm_mix[l].reshape(1, D_MODEL), w_in_l)

        tails = [proj_p[(b + 1) * SEQ - keep:(b + 1) * SEQ] for b in range(BATCH)]
        kp_l.append(jnp.stack([t[:, k0:k0 + KV_DIM] for t in tails])
                    .reshape(BATCH, keep, N_KV_HEADS, HEAD_DIM))
        vp_l.append(jnp.stack([t[:, k0 + KV_DIM:] for t in tails])
                    .reshape(BATCH, keep, N_KV_HEADS, HEAD_DIM))
        pp_l.append(jnp.stack([t[keep - POOL_HIST:, :POOL_DIM] for t in tails]))

        d_flat, new_pool = _sample_pool(
            l, state_t, proj_s.reshape(DEC_BATCH, DEC_SEQ * IN_DIM))
        ps_l.append(new_pool)

        xp = _prompt_mix(sinks_l, proj_p, xp, pool_w_l, pool_scale_l, w_out_l)
        kvt_l.append(kvt)
        outs = _sample_attn(
            l, sinks_l, d_flat.reshape(SAMPLE_ROWS, POOL_DIM), proj_s,
            jnp.stack(kvt_l) if last else kvt[None],
            cache_kt, cache_vt, xs, pool_w_l, pool_scale_l, w_out_l)
        xs = outs[0]
        if last:
            new_kt, new_vt = outs[1:]

        xp, xs, w_bf, _ = _ffn(xp, xs, norm_ffn2[l].reshape(1, D_MODEL), *w_bf, gf,
                               next_w=None if last else (l + 1,) + ffn1_w,
                               final_norm=last)

    y_prompt = xp.reshape(BATCH, SEQ, D_MODEL)
    y_sample = xs.reshape(DEC_BATCH, DEC_SEQ, D_MODEL)
    new_pool_sample = jnp.transpose(jnp.stack(ps_l), (0, 2, 1, 3))
    return (y_prompt, y_sample, jnp.stack(kp_l), jnp.stack(vp_l), jnp.stack(pp_l),
            from_cache_t(new_kt), from_cache_t(new_vt), new_pool_sample)
```

```python
import functools

import jax
import jax.numpy as jnp
from jax import lax
from jax.experimental import pallas as pl
from jax.experimental.pallas import tpu as pltpu

F32 = jnp.float32
BF16 = jnp.bfloat16

D_MODEL = 2048
BATCH = 2
SEQ = 4096
DEPTH = 2
DEC_BATCH = 128
DEC_SEQ = 4
PAST_LEN = 8192

POOL_DIM = 1024
POOL_WINDOWS = (2, 4, 8, 16)
POOL_GROUP_DIM = POOL_DIM // len(POOL_WINDOWS)
POOL_HIST = max(POOL_WINDOWS) - 1
HEAD_DIM = 64
N_HEADS = 16
N_KV_HEADS = 4
GROUP = N_HEADS // N_KV_HEADS
Q_DIM = N_HEADS * HEAD_DIM
KV_DIM = N_KV_HEADS * HEAD_DIM
IN_DIM = POOL_DIM + Q_DIM + 2 * KV_DIM
WINDOW = 128
FFN_DIM = 5632
RMS_EPS = 1e-5
ATTN_SCALE = HEAD_DIM ** -0.5
N_BUF = min(WINDOW, PAST_LEN)
N_HIST_SAMPLE = min(POOL_HIST, PAST_LEN)

PROMPT_ROWS = BATCH * SEQ
SAMPLE_ROWS = DEC_BATCH * DEC_SEQ
M_ROWS = PROMPT_ROWS + SAMPLE_ROWS

VMEM_LIMIT_BYTES = 56 * 1024 * 1024

FFN_TM = 1088
FFN_TF = 512
FFN_TILES = M_ROWS // FFN_TM
FFN_CHUNKS = FFN_DIM // FFN_TF
FFN_FULL_TILES = PROMPT_ROWS // FFN_TM
FFN_SPLIT = PROMPT_ROWS - FFN_FULL_TILES * FFN_TM
FFN_TF_HEAD = 256
WIN_ROWS = 32
WIN_BLOCKS = D_MODEL // WIN_ROWS
WOUT_BLOCKS = D_MODEL // HEAD_DIM
PROJ_TM = 512
PROJ_PROMPT_TILES = PROMPT_ROWS // PROJ_TM
MIX_TQ = 512
HALO_U = 16
ATT_CHUNK = 64
POOL_SB = 32
ATTN_SB = 32
ATTN_ROWS = ATTN_SB * DEC_SEQ

assert FFN_TILES * FFN_TM == M_ROWS and FFN_CHUNKS * FFN_TF == FFN_DIM
assert FFN_FULL_TILES + 1 == FFN_TILES and FFN_SPLIT + SAMPLE_ROWS == FFN_TM
assert FFN_SPLIT % 8 == 0 and FFN_FULL_TILES >= 1
assert FFN_CHUNKS >= 2
assert FFN_DIM % FFN_TF_HEAD == 0
assert SAMPLE_ROWS == PROJ_TM and PROJ_PROMPT_TILES * PROJ_TM == PROMPT_ROWS


def _rms(x, g):
    return x * lax.rsqrt(jnp.mean(x * x, axis=-1, keepdims=True) + RMS_EPS) * g


def _params(*sem):
    return pltpu.CompilerParams(dimension_semantics=sem,
                                vmem_limit_bytes=VMEM_LIMIT_BYTES)


def _ffn_body(*refs, final_norm, cast_next, prep_mix, t0, n_tiles, head_in):
    xp_hbm, xs_hbm, g_ref, wg_ref, wu_ref, wd_ref, gf_ref = refs[:7]
    refs = refs[7:]
    if cast_next:
        ng_ref, nu_ref, nd_ref = refs[:3]
        refs = refs[3:]
    if prep_mix:
        win_ref, wout_ref = refs[:2]
        refs = refs[2:]
    if head_in:
        head_hbm = refs[0]
        refs = refs[1:]
    op_hbm, os_hbm = refs[:2]
    refs = refs[2:]
    if cast_next:
        cg_ref, cu_ref, cd_ref = refs[:3]
        refs = refs[3:]
    if prep_mix:
        cwin_ref, cwout_ref = refs[:2]
        refs = refs[2:]
    acc_ref, xn_ref, in_sem, out_sem = refs[:4]
    head_sem = refs[4] if head_in else None

    i = pl.program_id(0)
    f = pl.program_id(1)
    slot = i % 2
    tile = i + t0

    def head_copy():
        return pltpu.make_async_copy(head_hbm, op_hbm.at[pl.ds(0, t0 * FFN_TM)], head_sem)

    def copies(p_hbm, s_hbm, tile, s, sem, mixed, to_vmem):
        if mixed:
            pairs = [(p_hbm.at[pl.ds(FFN_FULL_TILES * FFN_TM, FFN_SPLIT)],
                      acc_ref.at[s, pl.ds(0, FFN_SPLIT)]),
                     (s_hbm, acc_ref.at[s, pl.ds(FFN_SPLIT, SAMPLE_ROWS)])]
        else:
            pairs = [(p_hbm.at[pl.ds(tile * FFN_TM, FFN_TM)], acc_ref.at[s])]
        return [pltpu.make_async_copy(h, v, sem.at[s]) if to_vmem
                else pltpu.make_async_copy(v, h, sem.at[s]) for h, v in pairs]

    def on_tile(tile, s, op, to_vmem):
        p_hbm, s_hbm, sem = ((xp_hbm, xs_hbm, in_sem) if to_vmem
                             else (op_hbm, os_hbm, out_sem))

        @pl.when(tile < FFN_FULL_TILES)
        def _():
            for c in copies(p_hbm, s_hbm, tile, s, sem, False, to_vmem):
                op(c)

        @pl.when(tile == FFN_FULL_TILES)
        def _():
            for c in copies(p_hbm, s_hbm, tile, s, sem, True, to_vmem):
                op(c)

    start = lambda c: c.start()
    wait = lambda c: c.wait()

    @pl.when((i == 0) & (f == 0))
    def _():
        on_tile(tile, slot, start, True)
        if head_in:
            head_copy().start()

    @pl.when(f == 0)
    def _():
        on_tile(tile, slot, wait, True)
        xn_ref[...] = _rms(acc_ref[slot], g_ref[...]).astype(BF16)

    a = xn_ref[...]
    gate = jnp.dot(a, wg_ref[...], preferred_element_type=F32)
    if cast_next:
        cg_ref[...] = ng_ref[...].astype(BF16)
        cu_ref[...] = nu_ref[...].astype(BF16)
    up = jnp.dot(a, wu_ref[...], preferred_element_type=F32)
    if cast_next:
        cd_ref[...] = nd_ref[...].astype(BF16)
    if prep_mix:
        _prep_w_in(win_ref, cwin_ref)
        cwout_ref[...] = wout_ref[...].astype(BF16)
    h = (gate * jax.nn.sigmoid(gate) * up).astype(BF16)
    acc_ref[slot] += 0.5 * jnp.dot(h, wd_ref[...], preferred_element_type=F32)

    @pl.when(f == 1)
    def _():
        @pl.when(i >= 1)
        def _():
            on_tile(tile - 1, 1 - slot, wait, False)

        @pl.when(i + 1 < n_tiles)
        def _():
            on_tile(tile + 1, 1 - slot, start, True)

    @pl.when(f == FFN_CHUNKS - 1)
    def _():
        if final_norm:
            acc_ref[slot] = _rms(acc_ref[slot], gf_ref[...])
        on_tile(tile, slot, start, False)

        @pl.when(i == n_tiles - 1)
        def _():
            on_tile(tile, slot, wait, False)
            if head_in:
                head_copy().wait()


def _prep_w_in(w_ref, o_ref):
    k0 = POOL_DIM + Q_DIM
    o_ref[:, :POOL_DIM] = w_ref[:, :POOL_DIM].astype(BF16)
    o_ref[:, k0:] = w_ref[:, k0:].astype(BF16)
    lane = lax.broadcasted_iota(jnp.int32, (WIN_ROWS, 2 * HEAD_DIM), 1)
    for g in range(GROUP):
        for kp in range(N_KV_HEADS // 2):
            src = [POOL_DIM + (2 * kp + j) * KV_DIM + (g // 2) * 2 * HEAD_DIM
                   for j in (0, 1)]
            lo, hi = [w_ref[:, c:c + 2 * HEAD_DIM] for c in src]
            if g % 2 == 0:
                hi = pltpu.roll(hi, HEAD_DIM, axis=1)
            else:
                lo = pltpu.roll(lo, HEAD_DIM, axis=1)
            dst = POOL_DIM + g * KV_DIM + kp * 2 * HEAD_DIM
            o_ref[:, dst:dst + 2 * HEAD_DIM] = (
                jnp.where(lane < HEAD_DIM, lo, hi) * ATTN_SCALE).astype(BF16)


def _w_out_src_block(out_blk):
    pool_blocks = POOL_DIM // HEAD_DIM
    h = out_blk - pool_blocks
    src = pool_blocks + (h % N_KV_HEADS) * GROUP + h // N_KV_HEADS
    return jnp.where(out_blk < pool_blocks, out_blk, src)


def _ffn(xp, xs, g, wg, wu, wd, gf, next_w=None, mix_w=None, *, final_norm=False,
         head=None):
    cast_next = next_w is not None
    prep_mix = mix_w is not None
    head_in = head is not None
    t0 = head.shape[0] // FFN_TM if head_in else 0
    n_tiles = FFN_TILES - t0
    cast_blocks = 1 << (n_tiles.bit_length() - 1)
    cast_rows = D_MODEL // cast_blocks
    win_steps = pl.cdiv(WIN_BLOCKS, n_tiles)
    wout_steps = pl.cdiv(WOUT_BLOCKS, n_tiles)
    assert max(win_steps, wout_steps) <= FFN_CHUNKS
    cast_blk = lambda i: jnp.minimum(i, cast_blocks - 1)
    cast_chunk = lambda i, f: jnp.where(i < cast_blocks, f, FFN_CHUNKS - 1)
    vec = pl.BlockSpec((1, D_MODEL), lambda i, f: (0, 0))
    in_specs = [
        pl.BlockSpec(memory_space=pl.ANY),
        pl.BlockSpec(memory_space=pl.ANY),
        vec,
        pl.BlockSpec((D_MODEL, FFN_TF), lambda i, f: (0, f)),
        pl.BlockSpec((D_MODEL, FFN_TF), lambda i, f: (0, f)),
        pl.BlockSpec((FFN_TF, D_MODEL), lambda i, f: (f, 0)),
        vec,
    ]
    out_shape = [jax.ShapeDtypeStruct((PROMPT_ROWS, D_MODEL), F32),
                 jax.ShapeDtypeStruct((SAMPLE_ROWS, D_MODEL), F32)]
    out_specs = [pl.BlockSpec(memory_space=pl.ANY), pl.BlockSpec(memory_space=pl.ANY)]
    args = [xp, xs, g, wg, wu, wd, gf]
    if cast_next:
        nl = next_w[0]
        in_specs += [
            pl.BlockSpec((None, cast_rows, FFN_TF),
                         lambda i, f: (nl, cast_blk(i), cast_chunk(i, f))),
            pl.BlockSpec((None, cast_rows, FFN_TF),
                         lambda i, f: (nl, cast_blk(i), cast_chunk(i, f))),
            pl.BlockSpec((None, FFN_TF, cast_rows),
                         lambda i, f: (nl, cast_chunk(i, f), cast_blk(i))),
        ]
        out_shape += [jax.ShapeDtypeStruct((D_MODEL, FFN_DIM), BF16),
                      jax.ShapeDtypeStruct((D_MODEL, FFN_DIM), BF16),
                      jax.ShapeDtypeStruct((FFN_DIM, D_MODEL), BF16)]
        out_specs += [
            pl.BlockSpec((cast_rows, FFN_TF), lambda i, f: (cast_blk(i), cast_chunk(i, f))),
            pl.BlockSpec((cast_rows, FFN_TF), lambda i, f: (cast_blk(i), cast_chunk(i, f))),
            pl.BlockSpec((FFN_TF, cast_rows), lambda i, f: (cast_chunk(i, f), cast_blk(i))),
        ]
        args += list(next_w[1:])
    if prep_mix:
        ml = mix_w[0]
        win_blk = lambda i, f: jnp.minimum(
            i * win_steps + jnp.minimum(f, win_steps - 1), WIN_BLOCKS - 1)
        wout_blk = lambda i, f: jnp.minimum(
            i * wout_steps + jnp.minimum(f, wout_steps - 1), WOUT_BLOCKS - 1)
        in_specs += [
            pl.BlockSpec((None, WIN_ROWS, IN_DIM), lambda i, f: (ml, win_blk(i, f), 0)),
            pl.BlockSpec((None, HEAD_DIM, D_MODEL),
                         lambda i, f: (ml, _w_out_src_block(wout_blk(i, f)), 0)),
        ]
        out_shape += [jax.ShapeDtypeStruct((D_MODEL, IN_DIM), BF16),
                      jax.ShapeDtypeStruct((D_MODEL, D_MODEL), BF16)]
        out_specs += [
            pl.BlockSpec((WIN_ROWS, IN_DIM), lambda i, f: (win_blk(i, f), 0)),
            pl.BlockSpec((HEAD_DIM, D_MODEL), lambda i, f: (wout_blk(i, f), 0)),
        ]
        args += list(mix_w[1:])
    scratch = [pltpu.VMEM((2, FFN_TM, D_MODEL), F32),
               pltpu.VMEM((FFN_TM, D_MODEL), BF16),
               pltpu.SemaphoreType.DMA((2,)),
               pltpu.SemaphoreType.DMA((2,))]
    if head_in:
        in_specs.append(pl.BlockSpec(memory_space=pl.ANY))
        args.append(head)
        scratch.append(pltpu.SemaphoreType.DMA(()))
    outs = pl.pallas_call(
        functools.partial(_ffn_body, final_norm=final_norm, cast_next=cast_next,
                          prep_mix=prep_mix, t0=t0, n_tiles=n_tiles, head_in=head_in),
        out_shape=out_shape,
        grid=(n_tiles, FFN_CHUNKS),
        in_specs=in_specs,
        out_specs=out_specs,
        scratch_shapes=scratch,
        compiler_params=_params("arbitrary", "arbitrary"),
        name="ffn_final" if final_norm else "ffn",
    )(*args)
    n_cast = 3 if cast_next else 0
    return outs[0], outs[1], tuple(outs[2:2 + n_cast]), tuple(outs[2 + n_cast:])


def _ffn_head_body(x_ref, g_ref, wg_ref, wu_ref, wd_ref, o_ref, og_ref, ou_ref, od_ref,
                   xn_ref, wgs_ref, wus_ref, wds_ref):
    step = pl.program_id(0)
    slot = step % 2

    @pl.when(step == 0)
    def _():
        x = x_ref[...]
        xn_ref[...] = _rms(x, g_ref[...]).astype(BF16)
        o_ref[...] = x
        wgs_ref[1] = jnp.zeros(wgs_ref.shape[1:], BF16)
        wus_ref[1] = jnp.zeros(wus_ref.shape[1:], BF16)
        wds_ref[1] = jnp.zeros(wds_ref.shape[1:], BF16)

    a = xn_ref[...]
    wg = wg_ref[...].astype(BF16)
    wgs_ref[slot] = wg
    og_ref[...] = wg
    gate = jnp.dot(a, wgs_ref[1 - slot], preferred_element_type=F32)
    wu = wu_ref[...].astype(BF16)
    wus_ref[slot] = wu
    ou_ref[...] = wu
    up = jnp.dot(a, wus_ref[1 - slot], preferred_element_type=F32)
    wd = wd_ref[...].astype(BF16)
    wds_ref[slot] = wd
    od_ref[...] = wd
    h = (gate * jax.nn.sigmoid(gate) * up).astype(BF16)
    o_ref[...] += 0.5 * jnp.dot(h, wds_ref[1 - slot], preferred_element_type=F32)


def _ffn_head(layer, xp, g, gate, up, down):
    tf = FFN_TF_HEAD
    n_chunks = FFN_DIM // tf
    fixed = lambda s: (0, 0)
    chunk = lambda s: jnp.minimum(s, n_chunks - 1)
    return pl.pallas_call(
        _ffn_head_body,
        out_shape=(jax.ShapeDtypeStruct((FFN_TM, D_MODEL), F32),
                   jax.ShapeDtypeStruct((D_MODEL, FFN_DIM), BF16),
                   jax.ShapeDtypeStruct((D_MODEL, FFN_DIM), BF16),
                   jax.ShapeDtypeStruct((FFN_DIM, D_MODEL), BF16)),
        grid=(n_chunks + 1,),
        in_specs=[
            pl.BlockSpec((FFN_TM, D_MODEL), fixed, pipeline_mode=pl.Buffered(1)),
            pl.BlockSpec((1, D_MODEL), fixed),
            pl.BlockSpec((None, D_MODEL, tf), lambda s: (layer, 0, chunk(s))),
            pl.BlockSpec((None, D_MODEL, tf), lambda s: (layer, 0, chunk(s))),
            pl.BlockSpec((None, tf, D_MODEL), lambda s: (layer, chunk(s), 0)),
        ],
        out_specs=(pl.BlockSpec((FFN_TM, D_MODEL), fixed),
                   pl.BlockSpec((D_MODEL, tf), lambda s: (0, chunk(s))),
                   pl.BlockSpec((D_MODEL, tf), lambda s: (0, chunk(s))),
                   pl.BlockSpec((tf, D_MODEL), lambda s: (chunk(s), 0))),
        scratch_shapes=[pltpu.VMEM((FFN_TM, D_MODEL), BF16),
                        pltpu.VMEM((2, D_MODEL, tf), BF16),
                        pltpu.VMEM((2, D_MODEL, tf), BF16),
                        pltpu.VMEM((2, tf, D_MODEL), BF16)],
        compiler_params=_params("arbitrary"),
        name="ffn_head",
    )(xp, g, gate, up, down)


def _inproj_body(xp_ref, xs_ref, g_ref, w_ref, pp_ref, ps_ref, kvt_ref):
    i = pl.program_id(0)

    @pl.when(i < PROJ_PROMPT_TILES)
    def _():
        xn = _rms(xp_ref[...], g_ref[...]).astype(BF16)
        pp_ref[...] = jnp.dot(xn, w_ref[...], preferred_element_type=F32)

    @pl.when(i == PROJ_PROMPT_TILES)
    def _():
        xn = _rms(xs_ref[...], g_ref[...]).astype(BF16)
        proj = jnp.dot(xn, w_ref[...], preferred_element_type=F32)
        ps_ref[...] = proj
        kvt_ref[...] = proj[:, POOL_DIM + Q_DIM:].T


def _inproj(xp, xs, g, w):
    prompt_tile = lambda i: (jnp.minimum(i, PROJ_PROMPT_TILES - 1), 0)
    fixed = lambda i: (0, 0)
    return pl.pallas_call(
        _inproj_body,
        out_shape=(jax.ShapeDtypeStruct((PROMPT_ROWS, IN_DIM), F32),
                   jax.ShapeDtypeStruct((SAMPLE_ROWS, IN_DIM), F32),
                   jax.ShapeDtypeStruct((2 * KV_DIM, SAMPLE_ROWS), F32)),
        grid=(PROJ_PROMPT_TILES + 1,),
        in_specs=[
            pl.BlockSpec((PROJ_TM, D_MODEL), prompt_tile),
            pl.BlockSpec((PROJ_TM, D_MODEL), fixed),
            pl.BlockSpec((1, D_MODEL), fixed),
            pl.BlockSpec((D_MODEL, IN_DIM), fixed),
        ],
        out_specs=(pl.BlockSpec((PROJ_TM, IN_DIM), prompt_tile),
                   pl.BlockSpec((PROJ_TM, IN_DIM), fixed),
                   pl.BlockSpec((2 * KV_DIM, SAMPLE_ROWS), fixed)),
        compiler_params=_params("arbitrary"),
        name="inproj",
    )(xp, xs, g, w)


def _kv_head_of_lane(shape):
    return lax.broadcasted_iota(jnp.int32, shape, 1) // HEAD_DIM


def _pool_linear(d, gi, pw_ref, ps_ref):
    lanes = slice(gi * POOL_GROUP_DIM, (gi + 1) * POOL_GROUP_DIM)
    y = jnp.dot(d.astype(BF16), pw_ref[gi], preferred_element_type=F32)
    return y * ps_ref[:, lanes]


def _prompt_mix_body(sink_ref, u_ref, q_ref, kv_ref, kvh_ref, uh_ref, x_ref,
                     pw_ref, ps_ref, wo_ref, o_ref, mix_ref, s_ref, p_ref):
    tiles_per_seq = SEQ // MIX_TQ
    tile = pl.program_id(0) % tiles_per_seq
    first = tile == 0

    lo = jnp.where(first, WINDOW, 0)
    keep_u = jnp.full((HALO_U, POOL_DIM), lo, jnp.int32) == 0
    keep_kv = jnp.full((WINDOW, 2 * KV_DIM), lo, jnp.int32) == 0

    def pool_mixer():
        uh = jnp.where(keep_u, uh_ref[...], 0.0)
        pos = tile * MIX_TQ + lax.broadcasted_iota(jnp.int32, (MIX_TQ, 1), 0)
        for gi, w in enumerate(POOL_WINDOWS):
            lanes = slice(gi * POOL_GROUP_DIM, (gi + 1) * POOL_GROUP_DIM)
            xg = jnp.concatenate([uh[:, lanes], u_ref[:, lanes]], axis=0)
            s = xg
            sh = 1
            while sh < w:
                s = s + pltpu.roll(s, sh, axis=0)
                sh *= 2
            cnt = jnp.minimum(w, pos + 1).astype(F32)
            d = s[HALO_U:] / cnt - xg[HALO_U:]
            mix_ref[:, lanes] = _pool_linear(d, gi, pw_ref, ps_ref).astype(BF16)

    pool_mixer()

    kvh = jnp.where(keep_kv, kvh_ref[...], 0.0)
    k_all = jnp.concatenate([kvh[:, :KV_DIM], kv_ref[:, :KV_DIM]], axis=0)
    v_all = jnp.concatenate([kvh[:, KV_DIM:], kv_ref[:, KV_DIM:]], axis=0)
    head = _kv_head_of_lane((2 * WINDOW, KV_DIM))
    n_blocks = MIX_TQ // WINDOW
    rows = GROUP * WINDOW
    for n in range(n_blocks):
        kw = k_all[n * WINDOW:(n + 2) * WINDOW]
        qn = q_ref[n * WINDOW:(n + 1) * WINDOW, :].astype(BF16)
        qs = jnp.concatenate(
            [qn[:, g * KV_DIM:(g + 1) * KV_DIM] for g in range(GROUP)], axis=0)
        for k in range(N_KV_HEADS):
            kk = jnp.where(head == k, kw, 0.0).astype(BF16)
            s_ref[n * rows:(n + 1) * rows, k * 2 * WINDOW:(k + 1) * 2 * WINDOW] = (
                lax.dot_general(qs, kk, (((1,), (1,)), ((), ())),
                                preferred_element_type=F32))

    fi = lax.broadcasted_iota(jnp.int32, (ATT_CHUNK, WINDOW), 0)
    fj = lax.broadcasted_iota(jnp.int32, (ATT_CHUNK, WINDOW), 1)
    from_prev = [fj > fi + c * ATT_CHUNK for c in range(WINDOW // ATT_CHUNK)]
    no_prev = [jnp.where(fp & (fj < lo), -jnp.inf, 0.0) for fp in from_prev]
    for n in range(n_blocks):
        for k in range(N_KV_HEADS):
            c0 = k * 2 * WINDOW
            for g in range(GROUP):
                sink = sink_ref[k * GROUP + g]
                for c in range(WINDOW // ATT_CHUNK):
                    r0 = n * rows + g * WINDOW + c * ATT_CHUNK
                    sc = jnp.where(from_prev[c],
                                   s_ref[r0:r0 + ATT_CHUNK, c0:c0 + WINDOW],
                                   s_ref[r0:r0 + ATT_CHUNK, c0 + WINDOW:c0 + 2 * WINDOW])
                    if n == 0:
                        sc = sc + no_prev[c]
                    m = jnp.maximum(jnp.max(sc, axis=-1, keepdims=True), sink)
                    p = jnp.exp(sc - m)
                    den = jnp.sum(p, axis=-1, keepdims=True) + jnp.exp(sink - m)
                    p = p * (1.0 / den)
                    p_ref[r0:r0 + ATT_CHUNK, c0:c0 + WINDOW] = (
                        jnp.where(from_prev[c], p, 0.0).astype(BF16))
                    p_ref[r0:r0 + ATT_CHUNK, c0 + WINDOW:c0 + 2 * WINDOW] = (
                        jnp.where(from_prev[c], 0.0, p).astype(BF16))

    for n in range(n_blocks):
        vw = v_all[n * WINDOW:(n + 2) * WINDOW]
        vcat = jnp.concatenate(
            [jnp.where(head == k, vw, 0.0).astype(BF16)
             for k in range(N_KV_HEADS)], axis=0)
        o = jnp.dot(p_ref[n * rows:(n + 1) * rows, :], vcat, preferred_element_type=F32)
        blk = slice(n * WINDOW, (n + 1) * WINDOW)
        for g in range(GROUP):
            mix_ref[blk, POOL_DIM + g * KV_DIM:POOL_DIM + (g + 1) * KV_DIM] = (
                o[g * WINDOW:(g + 1) * WINDOW].astype(BF16))

    o_ref[...] = x_ref[...] + jnp.dot(mix_ref[...], wo_ref[...],
                                      preferred_element_type=F32)


def _prompt_mix(sinks, proj, x, pool_w, pool_scale, w_out):
    blocks_kv = MIX_TQ // WINDOW
    blocks_u = MIX_TQ // HALO_U
    return pl.pallas_call(
        _prompt_mix_body,
        out_shape=jax.ShapeDtypeStruct((PROMPT_ROWS, D_MODEL), F32),
        grid=(PROMPT_ROWS // MIX_TQ,),
        in_specs=[
            pl.BlockSpec(memory_space=pltpu.SMEM),
            pl.BlockSpec((MIX_TQ, POOL_DIM), lambda i: (i, 0)),
            pl.BlockSpec((MIX_TQ, Q_DIM), lambda i: (i, 1)),
            pl.BlockSpec((MIX_TQ, 2 * KV_DIM), lambda i: (i, 4)),
            pl.BlockSpec((WINDOW, 2 * KV_DIM),
                         lambda i: (jnp.maximum(i * blocks_kv - 1, 0), 4)),
            pl.BlockSpec((HALO_U, POOL_DIM),
                         lambda i: (jnp.maximum(i * blocks_u - 1, 0), 0)),
            pl.BlockSpec((MIX_TQ, D_MODEL), lambda i: (i, 0)),
            pl.BlockSpec((len(POOL_WINDOWS), POOL_GROUP_DIM, POOL_GROUP_DIM),
                         lambda i: (0, 0, 0)),
            pl.BlockSpec((1, POOL_DIM), lambda i: (0, 0)),
            pl.BlockSpec((D_MODEL, D_MODEL), lambda i: (0, 0)),
        ],
        out_specs=pl.BlockSpec((MIX_TQ, D_MODEL), lambda i: (i, 0)),
        scratch_shapes=[pltpu.VMEM((MIX_TQ, D_MODEL), BF16),
                        pltpu.VMEM((GROUP * MIX_TQ, N_KV_HEADS * 2 * WINDOW), F32),
                        pltpu.VMEM((GROUP * MIX_TQ, N_KV_HEADS * 2 * WINDOW), BF16)],
        compiler_params=_params("parallel"),
        name="prompt_mix",
    )(sinks, proj, proj, proj, proj, proj, x, pool_w, pool_scale, w_out)


def _sample_pool_body(hist_ref, proj_ref, d_ref, newpool_ref):
    for gi, w in enumerate(POOL_WINDOWS):
        base = gi * POOL_GROUP_DIM
        seq = [hist_ref[j, :, base:base + POOL_GROUP_DIM] for j in range(POOL_HIST)]
        seq += [proj_ref[:, t * IN_DIM + base:t * IN_DIM + base + POOL_GROUP_DIM]
                for t in range(DEC_SEQ)]
        for t in range(DEC_SEQ):
            last = POOL_HIST + t
            s = seq[last - w + 1]
            for j in range(last - w + 2, last + 1):
                s = s + seq[j]
            cnt = float(min(w, N_HIST_SAMPLE + t + 1))
            d_ref[:, t * POOL_DIM + base:t * POOL_DIM + base + POOL_GROUP_DIM] = (
                s / cnt - seq[last])
    for j in range(POOL_HIST - DEC_SEQ):
        newpool_ref[j] = hist_ref[j + DEC_SEQ]
    for t in range(DEC_SEQ):
        newpool_ref[POOL_HIST - DEC_SEQ + t] = proj_ref[:, t * IN_DIM:t * IN_DIM + POOL_DIM]


def _sample_pool(layer, state_t, proj_flat):
    return pl.pallas_call(
        _sample_pool_body,
        out_shape=(jax.ShapeDtypeStruct((DEC_BATCH, DEC_SEQ * POOL_DIM), F32),
                   jax.ShapeDtypeStruct((POOL_HIST, DEC_BATCH, POOL_DIM), F32)),
        grid=(DEC_BATCH // POOL_SB,),
        in_specs=[
            pl.BlockSpec((None, POOL_HIST, POOL_SB, POOL_DIM), lambda i: (layer, 0, i, 0)),
            pl.BlockSpec((POOL_SB, DEC_SEQ * IN_DIM), lambda i: (i, 0)),
        ],
        out_specs=(pl.BlockSpec((POOL_SB, DEC_SEQ * POOL_DIM), lambda i: (i, 0)),
                   pl.BlockSpec((POOL_HIST, POOL_SB, POOL_DIM), lambda i: (0, i, 0))),
        compiler_params=_params("parallel"),
        name="sample_pool",
    )(state_t, proj_flat)


def _sample_attn_body(*refs, update_cache):
    if update_cache:
        (sink_ref, d_ref, q_ref, kvt_ref, ck_ref, cv_ref, x_ref, pw_ref, ps_ref, wo_ref,
         o_ref, nk_ref, nv_ref, mix_ref, s_ref, p_ref) = refs
    else:
        (sink_ref, d_ref, q_ref, kvt_ref, ck_ref, cv_ref, x_ref, pw_ref, ps_ref, wo_ref,
         o_ref, mix_ref, s_ref, p_ref) = refs

    lane = lax.broadcasted_iota(jnp.int32, (KV_DIM, N_BUF), 1)
    is_old = lane < N_BUF - DEC_SEQ

    def shift_in(b, kt, vt, kt_new, vt_new):
        shift_new = (N_BUF - DEC_SEQ - b * DEC_SEQ) % N_BUF
        nk_ref[b] = jnp.where(is_old, pltpu.roll(kt, N_BUF - DEC_SEQ, axis=1),
                              pltpu.roll(kt_new, shift_new, axis=1))
        nv_ref[b] = jnp.where(is_old, pltpu.roll(vt, N_BUF - DEC_SEQ, axis=1),
                              pltpu.roll(vt_new, shift_new, axis=1))

    def attend():
        _sample_attend(sink_ref, d_ref, q_ref, kvt_ref, ck_ref, cv_ref, x_ref, pw_ref,
                       ps_ref, wo_ref, o_ref, mix_ref, s_ref, p_ref,
                       shift_in if update_cache else None)

    if not update_cache:
        attend()
        return

    phase = pl.program_id(0)

    @pl.when(phase < DEPTH - 1)
    def _():
        kt_new = kvt_ref[:KV_DIM, :]
        vt_new = kvt_ref[KV_DIM:, :]
        for b in range(ATTN_SB):
            shift_in(b, ck_ref[b], cv_ref[b], kt_new, vt_new)

    @pl.when(phase == DEPTH - 1)
    def _():
        attend()


def _sample_attend(sink_ref, d_ref, q_ref, kvt_ref, ck_ref, cv_ref, x_ref, pw_ref, ps_ref,
                   wo_ref, o_ref, mix_ref, s_ref, p_ref, shift_in):
    for gi in range(len(POOL_WINDOWS)):
        lanes = slice(gi * POOL_GROUP_DIM, (gi + 1) * POOL_GROUP_DIM)
        mix_ref[:, lanes] = _pool_linear(d_ref[:, lanes], gi, pw_ref, ps_ref).astype(BF16)

    pair = 2 * DEC_SEQ
    grp = GROUP * pair
    rows = N_KV_HEADS * grp
    new_cols = ATTN_SB * DEC_SEQ
    keys = N_BUF + new_cols
    assert keys == KV_DIM and new_cols == N_BUF
    head = _kv_head_of_lane((rows, KV_DIM))
    row = lax.broadcasted_iota(jnp.int32, (rows, keys), 0)
    row_head = row // grp
    in_pair = row % pair
    low = in_pair < DEC_SEQ
    tok = in_pair % DEC_SEQ
    col = lax.broadcasted_iota(jnp.int32, (rows, keys), 1)
    new_col = col - N_BUF
    valid_cache = (col < N_BUF) & (col > tok)
    causal_new = (new_col >= 0) & ((new_col % DEC_SEQ) <= tok)
    seq_of_col = new_col // DEC_SEQ - in_pair // DEC_SEQ
    sink = jnp.concatenate(
        [jnp.full((pair, 1), sink_ref[k * GROUP + g], F32)
         for k in range(N_KV_HEADS) for g in range(GROUP)], axis=0)
    kt_new = kvt_ref[:KV_DIM, :]
    vt_new = kvt_ref[KV_DIM:, :]
    kt_new_bf = kt_new.astype(BF16)
    vt_new_bf = vt_new.astype(BF16)
    nt = (((1,), (1,)), ((), ()))

    n_pairs = ATTN_SB // 2
    for j in range(n_pairs):
        r0 = j * pair
        q8 = q_ref[r0:r0 + pair, :]
        qs = jnp.concatenate(
            [q8[:, g * KV_DIM:(g + 1) * KV_DIM] for g in range(GROUP)], axis=0)
        qrep = jnp.concatenate([qs] * N_KV_HEADS, axis=0)
        lhs = jnp.where(head == row_head, qrep, 0.0).astype(BF16)
        scores = [jnp.dot(lhs, ck_ref[b].astype(BF16), preferred_element_type=F32)
                  for b in (2 * j, 2 * j + 1)]
        s_new = jnp.dot(lhs, kt_new_bf, preferred_element_type=F32)
        s = jnp.concatenate([jnp.where(low[:, :N_BUF], scores[0], scores[1]), s_new],
                            axis=1)
        valid = valid_cache | (causal_new & (seq_of_col == 2 * j))
        s_ref[j * rows:(j + 1) * rows, :] = jnp.where(valid, s, -jnp.inf)

    for j in range(n_pairs):
        s = s_ref[j * rows:(j + 1) * rows, :]
        m = jnp.maximum(jnp.max(s, axis=-1, keepdims=True), sink)
        p = jnp.exp(s - m)
        den = jnp.sum(p, axis=-1, keepdims=True) + jnp.exp(sink - m)
        p_ref[j * rows:(j + 1) * rows, :] = (p * (1.0 / den)).astype(BF16)

    for j in range(n_pairs):
        r0 = j * pair
        p = p_ref[j * rows:(j + 1) * rows, :]
        p_old = p[:, :N_BUF]
        outs = []
        for b in (2 * j, 2 * j + 1):
            kt = ck_ref[b]
            vt = cv_ref[b]
            outs.append(lax.dot_general(p_old, vt.astype(BF16), nt,
                                        preferred_element_type=F32))
            if shift_in is not None:
                shift_in(b, kt, vt, kt_new, vt_new)
        o = jnp.where(low, outs[0], outs[1])
        o = o + lax.dot_general(p[:, N_BUF:], vt_new_bf, nt, preferred_element_type=F32)
        o = jnp.where(head == row_head, o, 0.0)
        og = o[0:grp]
        for k in range(1, N_KV_HEADS):
            og = og + o[k * grp:(k + 1) * grp]
        for g in range(GROUP):
            mix_ref[r0:r0 + pair,
                    POOL_DIM + g * KV_DIM:POOL_DIM + (g + 1) * KV_DIM] = (
                        og[g * pair:(g + 1) * pair].astype(BF16))

    o_ref[...] = x_ref[...] + jnp.dot(mix_ref[...], wo_ref[...],
                                      preferred_element_type=F32)


def _sample_attn(layer, sinks, d_rows, proj, kvts, cache_kt, cache_vt, x, pool_w,
                 pool_scale, w_out):
    phases = kvts.shape[0]
    update_cache = phases > 1
    score_shape = (ATTN_SB // 2 * N_HEADS * 2 * DEC_SEQ, N_BUF + ATTN_ROWS)
    assert phases == 1 or (phases == DEPTH and layer == DEPTH - 1)
    layer_of = (lambda p: p) if update_cache else (lambda p: layer)
    row = lambda p, i: jnp.where(p == phases - 1, i, 0)
    cache_spec = pl.BlockSpec((None, ATTN_SB, KV_DIM, N_BUF),
                              lambda p, i: (layer_of(p), i, 0, 0))
    out_shape = [jax.ShapeDtypeStruct((SAMPLE_ROWS, D_MODEL), F32)]
    out_specs = [pl.BlockSpec((ATTN_ROWS, D_MODEL), lambda p, i: (row(p, i), 0))]
    if update_cache:
        out_shape += [jax.ShapeDtypeStruct((DEPTH, DEC_BATCH, KV_DIM, N_BUF), F32)] * 2
        out_specs += [pl.BlockSpec((None, ATTN_SB, KV_DIM, N_BUF),
                                   lambda p, i: (p, i, 0, 0))] * 2
    return pl.pallas_call(
        functools.partial(_sample_attn_body, update_cache=update_cache),
        out_shape=out_shape,
        grid=(phases, DEC_BATCH // ATTN_SB),
        in_specs=[
            pl.BlockSpec(memory_space=pltpu.SMEM),
            pl.BlockSpec((ATTN_ROWS, POOL_DIM), lambda p, i: (row(p, i), 0)),
            pl.BlockSpec((ATTN_ROWS, Q_DIM), lambda p, i: (row(p, i), 1)),
            pl.BlockSpec((None, 2 * KV_DIM, ATTN_ROWS), lambda p, i: (p, 0, i)),
            cache_spec,
            cache_spec,
            pl.BlockSpec((ATTN_ROWS, D_MODEL), lambda p, i: (row(p, i), 0)),
            pl.BlockSpec((len(POOL_WINDOWS), POOL_GROUP_DIM, POOL_GROUP_DIM),
                         lambda p, i: (0, 0, 0)),
            pl.BlockSpec((1, POOL_DIM), lambda p, i: (0, 0)),
            pl.BlockSpec((D_MODEL, D_MODEL), lambda p, i: (0, 0),
                         pipeline_mode=pl.Buffered(1)),
        ],
        out_specs=out_specs,
        scratch_shapes=[pltpu.VMEM((ATTN_ROWS, D_MODEL), BF16),
                        pltpu.VMEM(score_shape, F32),
                        pltpu.VMEM(score_shape, BF16)],
        compiler_params=_params("arbitrary", "arbitrary"),
        name="sample_attn",
    )(sinks, d_rows, proj, kvts, cache_kt, cache_vt, x, pool_w, pool_scale, w_out)


def kernel(x_prompt, x_sample, cache_k, cache_v, state_pool, norm_ffn1, ffn1_gate,
           ffn1_up, ffn1_down, norm_mix, w_in, pool_w, pool_scale, attn_sinks,
           w_out, norm_ffn2, ffn2_gate, ffn2_up, ffn2_down, final_norm):
    xp = x_prompt.reshape(PROMPT_ROWS, D_MODEL)
    xs = x_sample.reshape(SAMPLE_ROWS, D_MODEL)
    gf = final_norm.reshape(1, D_MODEL)
    state_t = jnp.transpose(state_pool, (0, 2, 1, 3))

    def to_cache_t(c):
        return jnp.transpose(c, (0, 1, 3, 4, 2)).reshape(DEPTH, DEC_BATCH, KV_DIM, N_BUF)

    def from_cache_t(c):
        c = c.reshape(DEPTH, DEC_BATCH, N_KV_HEADS, HEAD_DIM, N_BUF)
        return jnp.transpose(c, (0, 1, 4, 2, 3))

    cache_kt = to_cache_t(cache_k)
    cache_vt = to_cache_t(cache_v)
    keep = min(WINDOW, SEQ)
    assert keep >= POOL_HIST
    k0 = POOL_DIM + Q_DIM

    ffn1_w = (ffn1_gate, ffn1_up, ffn1_down)
    ffn2_w = (ffn2_gate, ffn2_up, ffn2_down)
    head, *w_bf = _ffn_head(0, xp, norm_ffn1[0].reshape(1, D_MODEL), *ffn1_w)
    kp_l, vp_l, pp_l, ps_l, kvt_l = [], [], [], [], []
    for l in range(DEPTH):
        last = l == DEPTH - 1
        pool_w_l = pool_w[l].astype(BF16)
        pool_scale_l = pool_scale[l].reshape(1, POOL_DIM)
        sinks_l = attn_sinks[l].astype(F32)

        xp, xs, w_bf, (w_in_l, w_out_l) = _ffn(
            xp, xs, norm_ffn1[l].reshape(1, D_MODEL), *w_bf, gf,
            next_w=(l,) + ffn2_w, mix_w=(l, w_in, w_out),
            head=head if l == 0 else None)
        proj_p, proj_s, kvt = _inproj(xp, xs, norm_mix[l].reshape(1, D_MODEL), w_in_l)

        tails = [proj_p[(b + 1) * SEQ - keep:(b + 1) * SEQ] for b in range(BATCH)]
        kp_l.append(jnp.stack([t[:, k0:k0 + KV_DIM] for t in tails])
                    .reshape(BATCH, keep, N_KV_HEADS, HEAD_DIM))
        vp_l.append(jnp.stack([t[:, k0 + KV_DIM:] for t in tails])
                    .reshape(BATCH, keep, N_KV_HEADS, HEAD_DIM))
        pp_l.append(jnp.stack([t[keep - POOL_HIST:, :POOL_DIM] for t in tails]))

        d_flat, new_pool = _sample_pool(
            l, state_t, proj_s.reshape(DEC_BATCH, DEC_SEQ * IN_DIM))
        ps_l.append(new_pool)

        xp = _prompt_mix(sinks_l, proj_p, xp, pool_w_l, pool_scale_l, w_out_l)
        kvt_l.append(kvt)
        outs = _sample_attn(
            l, sinks_l, d_flat.reshape(SAMPLE_ROWS, POOL_DIM), proj_s,
            jnp.stack(kvt_l) if last else kvt[None],
            cache_kt, cache_vt, xs, pool_w_l, pool_scale_l, w_out_l)
        xs = outs[0]
        if last:
            new_kt, new_vt = outs[1:]

        xp, xs, w_bf, _ = _ffn(xp, xs, norm_ffn2[l].reshape(1, D_MODEL), *w_bf, gf,
                               next_w=None if last else (l + 1,) + ffn1_w,
                               final_norm=last)

    y_prompt = xp.reshape(BATCH, SEQ, D_MODEL)
    y_sample = xs.reshape(DEC_BATCH, DEC_SEQ, D_MODEL)
    new_pool_sample = jnp.transpose(jnp.stack(ps_l), (0, 2, 1, 3))
    return (y_prompt, y_sample, jnp.stack(kp_l), jnp.stack(vp_l), jnp.stack(pp_l),
            from_cache_t(new_kt), from_cache_t(new_vt), new_pool_sample)
```

```python
import functools

import jax
import jax.numpy as jnp
from jax import lax
from jax.experimental import pallas as pl
from jax.experimental.pallas import tpu as pltpu

F32 = jnp.float32
BF16 = jnp.bfloat16

D_MODEL = 2048
BATCH = 2
SEQ = 4096
DEPTH = 2
DEC_BATCH = 128
DEC_SEQ = 4
PAST_LEN = 8192

POOL_DIM = 1024
POOL_WINDOWS = (2, 4, 8, 16)
POOL_GROUP_DIM = POOL_DIM // len(POOL_WINDOWS)
POOL_HIST = max(POOL_WINDOWS) - 1
HEAD_DIM = 64
N_HEADS = 16
N_KV_HEADS = 4
GROUP = N_HEADS // N_KV_HEADS
Q_DIM = N_HEADS * HEAD_DIM
KV_DIM = N_KV_HEADS * HEAD_DIM
IN_DIM = POOL_DIM + Q_DIM + 2 * KV_DIM
WINDOW = 128
FFN_DIM = 5632
RMS_EPS = 1e-5
ATTN_SCALE = HEAD_DIM ** -0.5
N_BUF = min(WINDOW, PAST_LEN)
N_HIST_SAMPLE = min(POOL_HIST, PAST_LEN)

PROMPT_ROWS = BATCH * SEQ
SAMPLE_ROWS = DEC_BATCH * DEC_SEQ
M_ROWS = PROMPT_ROWS + SAMPLE_ROWS

VMEM_LIMIT_BYTES = 56 * 1024 * 1024

FFN_TM = 1088
FFN_TF = 512
FFN_TILES = M_ROWS // FFN_TM
FFN_CHUNKS = FFN_DIM // FFN_TF
FFN_FULL_TILES = PROMPT_ROWS // FFN_TM
FFN_SPLIT = PROMPT_ROWS - FFN_FULL_TILES * FFN_TM
FFN_TF_HEAD = 256
WIN_ROWS = 32
WIN_BLOCKS = D_MODEL // WIN_ROWS
WOUT_BLOCKS = D_MODEL // HEAD_DIM
PROJ_TM = 512
PROJ_PROMPT_TILES = PROMPT_ROWS // PROJ_TM
MIX_TQ = 512
HALO_U = 16
ATT_CHUNK = 64
POOL_SB = 32
ATTN_SB = 32
ATTN_ROWS = ATTN_SB * DEC_SEQ

assert FFN_TILES * FFN_TM == M_ROWS and FFN_CHUNKS * FFN_TF == FFN_DIM
assert FFN_FULL_TILES + 1 == FFN_TILES and FFN_SPLIT + SAMPLE_ROWS == FFN_TM
assert FFN_SPLIT % 8 == 0 and FFN_FULL_TILES >= 1
assert FFN_CHUNKS >= 2
assert FFN_DIM % FFN_TF_HEAD == 0
assert SAMPLE_ROWS == PROJ_TM and PROJ_PROMPT_TILES * PROJ_TM == PROMPT_ROWS


def _rms(x, g):
    return x * lax.rsqrt(jnp.mean(x * x, axis=-1, keepdims=True) + RMS_EPS) * g


def _params(*sem):
    return pltpu.CompilerParams(dimension_semantics=sem,
                                vmem_limit_bytes=VMEM_LIMIT_BYTES)


def _ffn_body(*refs, final_norm, cast_next, prep_mix, t0, n_tiles, head_in):
    xp_hbm, xs_hbm, g_ref, wg_ref, wu_ref, wd_ref, gf_ref = refs[:7]
    refs = refs[7:]
    if cast_next:
        ng_ref, nu_ref, nd_ref = refs[:3]
        refs = refs[3:]
    if prep_mix:
        win_ref, wout_ref = refs[:2]
        refs = refs[2:]
    if head_in:
        head_hbm = refs[0]
        refs = refs[1:]
    op_hbm, os_hbm = refs[:2]
    refs = refs[2:]
    if cast_next:
        cg_ref, cu_ref, cd_ref = refs[:3]
        refs = refs[3:]
    if prep_mix:
        cwin_ref, cwout_ref = refs[:2]
        refs = refs[2:]
    acc_ref, xn_ref, in_sem, out_sem = refs[:4]
    head_sem = refs[4] if head_in else None

    i = pl.program_id(0)
    f = pl.program_id(1)
    slot = i % 2
    tile = i + t0

    def head_copy():
        return pltpu.make_async_copy(head_hbm, op_hbm.at[pl.ds(0, t0 * FFN_TM)], head_sem)

    def copies(p_hbm, s_hbm, tile, s, sem, mixed, to_vmem):
        if mixed:
            pairs = [(p_hbm.at[pl.ds(FFN_FULL_TILES * FFN_TM, FFN_SPLIT)],
                      acc_ref.at[s, pl.ds(0, FFN_SPLIT)]),
                     (s_hbm, acc_ref.at[s, pl.ds(FFN_SPLIT, SAMPLE_ROWS)])]
        else:
            pairs = [(p_hbm.at[pl.ds(tile * FFN_TM, FFN_TM)], acc_ref.at[s])]
        return [pltpu.make_async_copy(h, v, sem.at[s]) if to_vmem
                else pltpu.make_async_copy(v, h, sem.at[s]) for h, v in pairs]

    def on_tile(tile, s, op, to_vmem):
        p_hbm, s_hbm, sem = ((xp_hbm, xs_hbm, in_sem) if to_vmem
                             else (op_hbm, os_hbm, out_sem))

        @pl.when(tile < FFN_FULL_TILES)
        def _():
            for c in copies(p_hbm, s_hbm, tile, s, sem, False, to_vmem):
                op(c)

        @pl.when(tile == FFN_FULL_TILES)
        def _():
            for c in copies(p_hbm, s_hbm, tile, s, sem, True, to_vmem):
                op(c)

    start = lambda c: c.start()
    wait = lambda c: c.wait()

    @pl.when((i == 0) & (f == 0))
    def _():
        on_tile(tile, slot, start, True)
        if head_in:
            head_copy().start()

    @pl.when(f == 0)
    def _():
        on_tile(tile, slot, wait, True)
        xn_ref[...] = _rms(acc_ref[slot], g_ref[...]).astype(BF16)

    a = xn_ref[...]
    gate = jnp.dot(a, wg_ref[...], preferred_element_type=F32)
    if cast_next:
        cg_ref[...] = ng_ref[...].astype(BF16)
        cu_ref[...] = nu_ref[...].astype(BF16)
    up = jnp.dot(a, wu_ref[...], preferred_element_type=F32)
    if cast_next:
        cd_ref[...] = nd_ref[...].astype(BF16)
    if prep_mix:
        _prep_w_in(win_ref, cwin_ref)
        cwout_ref[...] = wout_ref[...].astype(BF16)
    h = (gate * jax.nn.sigmoid(gate) * up).astype(BF16)
    acc_ref[slot] += 0.5 * jnp.dot(h, wd_ref[...], preferred_element_type=F32)

    @pl.when(f == 1)
    def _():
        @pl.when(i >= 1)
        def _():
            on_tile(tile - 1, 1 - slot, wait, False)

        @pl.when(i + 1 < n_tiles)
        def _():
            on_tile(tile + 1, 1 - slot, start, True)

    @pl.when(f == FFN_CHUNKS - 1)
    def _():
        if final_norm:
            acc_ref[slot] = _rms(acc_ref[slot], gf_ref[...])
        on_tile(tile, slot, start, False)

        @pl.when(i == n_tiles - 1)
        def _():
            on_tile(tile, slot, wait, False)
            if head_in:
                head_copy().wait()


def _prep_w_in(w_ref, o_ref):
    k0 = POOL_DIM + Q_DIM
    o_ref[:, :POOL_DIM] = w_ref[:, :POOL_DIM].astype(BF16)
    o_ref[:, k0:] = w_ref[:, k0:].astype(BF16)
    lane = lax.broadcasted_iota(jnp.int32, (WIN_ROWS, 2 * HEAD_DIM), 1)
    for g in range(GROUP):
        for kp in range(N_KV_HEADS // 2):
            src = [POOL_DIM + (2 * kp + j) * KV_DIM + (g // 2) * 2 * HEAD_DIM
                   for j in (0, 1)]
            lo, hi = [w_ref[:, c:c + 2 * HEAD_DIM] for c in src]
            if g % 2 == 0:
                hi = pltpu.roll(hi, HEAD_DIM, axis=1)
            else:
                lo = pltpu.roll(lo, HEAD_DIM, axis=1)
            dst = POOL_DIM + g * KV_DIM + kp * 2 * HEAD_DIM
            o_ref[:, dst:dst + 2 * HEAD_DIM] = (
                jnp.where(lane < HEAD_DIM, lo, hi) * ATTN_SCALE).astype(BF16)


def _w_out_src_block(out_blk):
    pool_blocks = POOL_DIM // HEAD_DIM
    h = out_blk - pool_blocks
    src = pool_blocks + (h % N_KV_HEADS) * GROUP + h // N_KV_HEADS
    return jnp.where(out_blk < pool_blocks, out_blk, src)


def _ffn(xp, xs, g, wg, wu, wd, gf, next_w=None, mix_w=None, *, final_norm=False,
         head=None):
    cast_next = next_w is not None
    prep_mix = mix_w is not None
    head_in = head is not None
    t0 = head.shape[0] // FFN_TM if head_in else 0
    n_tiles = FFN_TILES - t0
    cast_blocks = 1 << (n_tiles.bit_length() - 1)
    cast_rows = D_MODEL // cast_blocks
    win_steps = pl.cdiv(WIN_BLOCKS, n_tiles)
    wout_steps = pl.cdiv(WOUT_BLOCKS, n_tiles)
    assert max(win_steps, wout_steps) <= FFN_CHUNKS
    cast_blk = lambda i: jnp.minimum(i, cast_blocks - 1)
    cast_chunk = lambda i, f: jnp.where(i < cast_blocks, f, FFN_CHUNKS - 1)
    vec = pl.BlockSpec((1, D_MODEL), lambda i, f: (0, 0))
    in_specs = [
        pl.BlockSpec(memory_space=pl.ANY),
        pl.BlockSpec(memory_space=pl.ANY),
        vec,
        pl.BlockSpec((D_MODEL, FFN_TF), lambda i, f: (0, f)),
        pl.BlockSpec((D_MODEL, FFN_TF), lambda i, f: (0, f)),
        pl.BlockSpec((FFN_TF, D_MODEL), lambda i, f: (f, 0)),
        vec,
    ]
    out_shape = [jax.ShapeDtypeStruct((PROMPT_ROWS, D_MODEL), F32),
                 jax.ShapeDtypeStruct((SAMPLE_ROWS, D_MODEL), F32)]
    out_specs = [pl.BlockSpec(memory_space=pl.ANY), pl.BlockSpec(memory_space=pl.ANY)]
    args = [xp, xs, g, wg, wu, wd, gf]
    if cast_next:
        nl = next_w[0]
        in_specs += [
            pl.BlockSpec((None, cast_rows, FFN_TF),
                         lambda i, f: (nl, cast_blk(i), cast_chunk(i, f))),
            pl.BlockSpec((None, cast_rows, FFN_TF),
                         lambda i, f: (nl, cast_blk(i), cast_chunk(i, f))),
            pl.BlockSpec((None, FFN_TF, cast_rows),
                         lambda i, f: (nl, cast_chunk(i, f), cast_blk(i))),
        ]
        out_shape += [jax.ShapeDtypeStruct((D_MODEL, FFN_DIM), BF16),
                      jax.ShapeDtypeStruct((D_MODEL, FFN_DIM), BF16),
                      jax.ShapeDtypeStruct((FFN_DIM, D_MODEL), BF16)]
        out_specs += [
            pl.BlockSpec((cast_rows, FFN_TF), lambda i, f: (cast_blk(i), cast_chunk(i, f))),
            pl.BlockSpec((cast_rows, FFN_TF), lambda i, f: (cast_blk(i), cast_chunk(i, f))),
            pl.BlockSpec((FFN_TF, cast_rows), lambda i, f: (cast_chunk(i, f), cast_blk(i))),
        ]
        args += list(next_w[1:])
    if prep_mix:
        ml = mix_w[0]
        win_blk = lambda i, f: jnp.minimum(
            i * win_steps + jnp.minimum(f, win_steps - 1), WIN_BLOCKS - 1)
        wout_blk = lambda i, f: jnp.minimum(
            i * wout_steps + jnp.minimum(f, wout_steps - 1), WOUT_BLOCKS - 1)
        in_specs += [
            pl.BlockSpec((None, WIN_ROWS, IN_DIM), lambda i, f: (ml, win_blk(i, f), 0)),
            pl.BlockSpec((None, HEAD_DIM, D_MODEL),
                         lambda i, f: (ml, _w_out_src_block(wout_blk(i, f)), 0)),
        ]
        out_shape += [jax.ShapeDtypeStruct((D_MODEL, IN_DIM), BF16),
                      jax.ShapeDtypeStruct((D_MODEL, D_MODEL), BF16)]
        out_specs += [
            pl.BlockSpec((WIN_ROWS, IN_DIM), lambda i, f: (win_blk(i, f), 0)),
            pl.BlockSpec((HEAD_DIM, D_MODEL), lambda i, f: (wout_blk(i, f), 0)),
        ]
        args += list(mix_w[1:])
    scratch = [pltpu.VMEM((2, FFN_TM, D_MODEL), F32),
               pltpu.VMEM((FFN_TM, D_MODEL), BF16),
               pltpu.SemaphoreType.DMA((2,)),
               pltpu.SemaphoreType.DMA((2,))]
    if head_in:
        in_specs.append(pl.BlockSpec(memory_space=pl.ANY))
        args.append(head)
        scratch.append(pltpu.SemaphoreType.DMA(()))
    outs = pl.pallas_call(
        functools.partial(_ffn_body, final_norm=final_norm, cast_next=cast_next,
                          prep_mix=prep_mix, t0=t0, n_tiles=n_tiles, head_in=head_in),
        out_shape=out_shape,
        grid=(n_tiles, FFN_CHUNKS),
        in_specs=in_specs,
        out_specs=out_specs,
        scratch_shapes=scratch,
        compiler_params=_params("arbitrary", "arbitrary"),
        name="ffn_final" if final_norm else "ffn",
    )(*args)
    n_cast = 3 if cast_next else 0
    return outs[0], outs[1], tuple(outs[2:2 + n_cast]), tuple(outs[2 + n_cast:])


def _ffn_head_body(x_ref, g_ref, wg_ref, wu_ref, wd_ref, o_ref, og_ref, ou_ref, od_ref,
                   xn_ref, wgs_ref, wus_ref, wds_ref):
    step = pl.program_id(0)
    slot = step % 2

    @pl.when(step == 0)
    def _():
        x = x_ref[...]
        xn_ref[...] = _rms(x, g_ref[...]).astype(BF16)
        o_ref[...] = x
        wgs_ref[1] = jnp.zeros(wgs_ref.shape[1:], BF16)
        wus_ref[1] = jnp.zeros(wus_ref.shape[1:], BF16)
        wds_ref[1] = jnp.zeros(wds_ref.shape[1:], BF16)

    a = xn_ref[...]
    wg = wg_ref[...].astype(BF16)
    wgs_ref[slot] = wg
    og_ref[...] = wg
    gate = jnp.dot(a, wgs_ref[1 - slot], preferred_element_type=F32)
    wu = wu_ref[...].astype(BF16)
    wus_ref[slot] = wu
    ou_ref[...] = wu
    up = jnp.dot(a, wus_ref[1 - slot], preferred_element_type=F32)
    wd = wd_ref[...].astype(BF16)
    wds_ref[slot] = wd
    od_ref[...] = wd
    h = (gate * jax.nn.sigmoid(gate) * up).astype(BF16)
    o_ref[...] += 0.5 * jnp.dot(h, wds_ref[1 - slot], preferred_element_type=F32)


def _ffn_head(layer, xp, g, gate, up, down):
    tf = FFN_TF_HEAD
    n_chunks = FFN_DIM // tf
    fixed = lambda s: (0, 0)
    chunk = lambda s: jnp.minimum(s, n_chunks - 1)
    return pl.pallas_call(
        _ffn_head_body,
        out_shape=(jax.ShapeDtypeStruct((FFN_TM, D_MODEL), F32),
                   jax.ShapeDtypeStruct((D_MODEL, FFN_DIM), BF16),
                   jax.ShapeDtypeStruct((D_MODEL, FFN_DIM), BF16),
                   jax.ShapeDtypeStruct((FFN_DIM, D_MODEL), BF16)),
        grid=(n_chunks + 1,),
        in_specs=[
            pl.BlockSpec((FFN_TM, D_MODEL), fixed, pipeline_mode=pl.Buffered(1)),
            pl.BlockSpec((1, D_MODEL), fixed),
            pl.BlockSpec((None, D_MODEL, tf), lambda s: (layer, 0, chunk(s))),
            pl.BlockSpec((None, D_MODEL, tf), lambda s: (layer, 0, chunk(s))),
            pl.BlockSpec((None, tf, D_MODEL), lambda s: (layer, chunk(s), 0)),
        ],
        out_specs=(pl.BlockSpec((FFN_TM, D_MODEL), fixed),
                   pl.BlockSpec((D_MODEL, tf), lambda s: (0, chunk(s))),
                   pl.BlockSpec((D_MODEL, tf), lambda s: (0, chunk(s))),
                   pl.BlockSpec((tf, D_MODEL), lambda s: (chunk(s), 0))),
        scratch_shapes=[pltpu.VMEM((FFN_TM, D_MODEL), BF16),
                        pltpu.VMEM((2, D_MODEL, tf), BF16),
                        pltpu.VMEM((2, D_MODEL, tf), BF16),
                        pltpu.VMEM((2, tf, D_MODEL), BF16)],
        compiler_params=_params("arbitrary"),
        name="ffn_head",
    )(xp, g, gate, up, down)


def _inproj_body(xp_ref, xs_ref, g_ref, w_ref, pp_ref, ps_ref, kvt_ref):
    i = pl.program_id(0)

    @pl.when(i < PROJ_PROMPT_TILES)
    def _():
        xn = _rms(xp_ref[...], g_ref[...]).astype(BF16)
        pp_ref[...] = jnp.dot(xn, w_ref[...], preferred_element_type=F32)

    @pl.when(i == PROJ_PROMPT_TILES)
    def _():
        xn = _rms(xs_ref[...], g_ref[...]).astype(BF16)
        proj = jnp.dot(xn, w_ref[...], preferred_element_type=F32)
        ps_ref[...] = proj
        kvt_ref[...] = proj[:, POOL_DIM + Q_DIM:].T


def _inproj(xp, xs, g, w):
    prompt_tile = lambda i: (jnp.minimum(i, PROJ_PROMPT_TILES - 1), 0)
    fixed = lambda i: (0, 0)
    return pl.pallas_call(
        _inproj_body,
        out_shape=(jax.ShapeDtypeStruct((PROMPT_ROWS, IN_DIM), F32),
                   jax.ShapeDtypeStruct((SAMPLE_ROWS, IN_DIM), F32),
                   jax.ShapeDtypeStruct((2 * KV_DIM, SAMPLE_ROWS), F32)),
        grid=(PROJ_PROMPT_TILES + 1,),
        in_specs=[
            pl.BlockSpec((PROJ_TM, D_MODEL), prompt_tile),
            pl.BlockSpec((PROJ_TM, D_MODEL), fixed),
            pl.BlockSpec((1, D_MODEL), fixed),
            pl.BlockSpec((D_MODEL, IN_DIM), fixed),
        ],
        out_specs=(pl.BlockSpec((PROJ_TM, IN_DIM), prompt_tile),
                   pl.BlockSpec((PROJ_TM, IN_DIM), fixed),
                   pl.BlockSpec((2 * KV_DIM, SAMPLE_ROWS), fixed)),
        compiler_params=_params("arbitrary"),
        name="inproj",
    )(xp, xs, g, w)


def _kv_head_of_lane(shape):
    return lax.broadcasted_iota(jnp.int32, shape, 1) // HEAD_DIM


def _pool_linear(d, gi, pw_ref, ps_ref):
    lanes = slice(gi * POOL_GROUP_DIM, (gi + 1) * POOL_GROUP_DIM)
    y = jnp.dot(d.astype(BF16), pw_ref[gi], preferred_element_type=F32)
    return y * ps_ref[:, lanes]


def _prompt_mix_body(sink_ref, u_ref, q_ref, kv_ref, kvh_ref, uh_ref, x_ref,
                     pw_ref, ps_ref, wo_ref, o_ref, mix_ref, s_ref, p_ref):
    tiles_per_seq = SEQ // MIX_TQ
    tile = pl.program_id(0) % tiles_per_seq
    first = tile == 0

    lo = jnp.where(first, WINDOW, 0)
    keep_u = jnp.full((HALO_U, POOL_DIM), lo, jnp.int32) == 0
    keep_kv = jnp.full((WINDOW, 2 * KV_DIM), lo, jnp.int32) == 0

    def pool_mixer():
        uh = jnp.where(keep_u, uh_ref[...], 0.0)
        pos = tile * MIX_TQ + lax.broadcasted_iota(jnp.int32, (MIX_TQ, 1), 0)
        for gi, w in enumerate(POOL_WINDOWS):
            lanes = slice(gi * POOL_GROUP_DIM, (gi + 1) * POOL_GROUP_DIM)
            xg = jnp.concatenate([uh[:, lanes], u_ref[:, lanes]], axis=0)
            s = xg
            sh = 1
            while sh < w:
                s = s + pltpu.roll(s, sh, axis=0)
                sh *= 2
            cnt = jnp.minimum(w, pos + 1).astype(F32)
            d = s[HALO_U:] / cnt - xg[HALO_U:]
            mix_ref[:, lanes] = _pool_linear(d, gi, pw_ref, ps_ref).astype(BF16)

    pool_mixer()

    kvh = jnp.where(keep_kv, kvh_ref[...], 0.0)
    k_all = jnp.concatenate([kvh[:, :KV_DIM], kv_ref[:, :KV_DIM]], axis=0)
    v_all = jnp.concatenate([kvh[:, KV_DIM:], kv_ref[:, KV_DIM:]], axis=0)
    head = _kv_head_of_lane((2 * WINDOW, KV_DIM))
    n_blocks = MIX_TQ // WINDOW
    rows = GROUP * WINDOW
    for n in range(n_blocks):
        kw = k_all[n * WINDOW:(n + 2) * WINDOW]
        qn = q_ref[n * WINDOW:(n + 1) * WINDOW, :].astype(BF16)
        qs = jnp.concatenate(
            [qn[:, g * KV_DIM:(g + 1) * KV_DIM] for g in range(GROUP)], axis=0)
        for k in range(N_KV_HEADS):
            kk = jnp.where(head == k, kw, 0.0).astype(BF16)
            s_ref[n * rows:(n + 1) * rows, k * 2 * WINDOW:(k + 1) * 2 * WINDOW] = (
                lax.dot_general(qs, kk, (((1,), (1,)), ((), ())),
                                preferred_element_type=F32))

    o_ref[...] = x_ref[...] + jnp.dot(mix_ref[:, :POOL_DIM], wo_ref[:POOL_DIM, :],
                                      preferred_element_type=F32)

    fi = lax.broadcasted_iota(jnp.int32, (ATT_CHUNK, WINDOW), 0)
    fj = lax.broadcasted_iota(jnp.int32, (ATT_CHUNK, WINDOW), 1)
    from_prev = [fj > fi + c * ATT_CHUNK for c in range(WINDOW // ATT_CHUNK)]
    no_prev = [jnp.where(fp & (fj < lo), -jnp.inf, 0.0) for fp in from_prev]
    for n in range(n_blocks):
        for k in range(N_KV_HEADS):
            c0 = k * 2 * WINDOW
            for g in range(GROUP):
                sink = sink_ref[k * GROUP + g]
                for c in range(WINDOW // ATT_CHUNK):
                    r0 = n * rows + g * WINDOW + c * ATT_CHUNK
                    sc = jnp.where(from_prev[c],
                                   s_ref[r0:r0 + ATT_CHUNK, c0:c0 + WINDOW],
                                   s_ref[r0:r0 + ATT_CHUNK, c0 + WINDOW:c0 + 2 * WINDOW])
                    if n == 0:
                        sc = sc + no_prev[c]
                    m = jnp.maximum(jnp.max(sc, axis=-1, keepdims=True), sink)
                    p = jnp.exp(sc - m)
                    den = jnp.sum(p, axis=-1, keepdims=True) + jnp.exp(sink - m)
                    p = p * (1.0 / den)
                    p_ref[r0:r0 + ATT_CHUNK, c0:c0 + WINDOW] = (
                        jnp.where(from_prev[c], p, 0.0).astype(BF16))
                    p_ref[r0:r0 + ATT_CHUNK, c0 + WINDOW:c0 + 2 * WINDOW] = (
                        jnp.where(from_prev[c], 0.0, p).astype(BF16))

    for n in range(n_blocks):
        vw = v_all[n * WINDOW:(n + 2) * WINDOW]
        vcat = jnp.concatenate(
            [jnp.where(head == k, vw, 0.0).astype(BF16)
             for k in range(N_KV_HEADS)], axis=0)
        o = jnp.dot(p_ref[n * rows:(n + 1) * rows, :], vcat, preferred_element_type=F32)
        blk = slice(n * WINDOW, (n + 1) * WINDOW)
        for g in range(GROUP):
            mix_ref[blk, POOL_DIM + g * KV_DIM:POOL_DIM + (g + 1) * KV_DIM] = (
                o[g * WINDOW:(g + 1) * WINDOW].astype(BF16))

    o_ref[...] += jnp.dot(mix_ref[:, POOL_DIM:], wo_ref[POOL_DIM:, :],
                          preferred_element_type=F32)


def _prompt_mix(sinks, proj, x, pool_w, pool_scale, w_out):
    blocks_kv = MIX_TQ // WINDOW
    blocks_u = MIX_TQ // HALO_U
    return pl.pallas_call(
        _prompt_mix_body,
        out_shape=jax.ShapeDtypeStruct((PROMPT_ROWS, D_MODEL), F32),
        grid=(PROMPT_ROWS // MIX_TQ,),
        in_specs=[
            pl.BlockSpec(memory_space=pltpu.SMEM),
            pl.BlockSpec((MIX_TQ, POOL_DIM), lambda i: (i, 0)),
            pl.BlockSpec((MIX_TQ, Q_DIM), lambda i: (i, 1)),
            pl.BlockSpec((MIX_TQ, 2 * KV_DIM), lambda i: (i, 4)),
            pl.BlockSpec((WINDOW, 2 * KV_DIM),
                         lambda i: (jnp.maximum(i * blocks_kv - 1, 0), 4)),
            pl.BlockSpec((HALO_U, POOL_DIM),
                         lambda i: (jnp.maximum(i * blocks_u - 1, 0), 0)),
            pl.BlockSpec((MIX_TQ, D_MODEL), lambda i: (i, 0)),
            pl.BlockSpec((len(POOL_WINDOWS), POOL_GROUP_DIM, POOL_GROUP_DIM),
                         lambda i: (0, 0, 0)),
            pl.BlockSpec((1, POOL_DIM), lambda i: (0, 0)),
            pl.BlockSpec((D_MODEL, D_MODEL), lambda i: (0, 0)),
        ],
        out_specs=pl.BlockSpec((MIX_TQ, D_MODEL), lambda i: (i, 0)),
        scratch_shapes=[pltpu.VMEM((MIX_TQ, D_MODEL), BF16),
                        pltpu.VMEM((GROUP * MIX_TQ, N_KV_HEADS * 2 * WINDOW), F32),
                        pltpu.VMEM((GROUP * MIX_TQ, N_KV_HEADS * 2 * WINDOW), BF16)],
        compiler_params=_params("parallel"),
        name="prompt_mix",
    )(sinks, proj, proj, proj, proj, proj, x, pool_w, pool_scale, w_out)


def _sample_pool_body(hist_ref, proj_ref, d_ref, newpool_ref):
    for gi, w in enumerate(POOL_WINDOWS):
        base = gi * POOL_GROUP_DIM
        seq = [hist_ref[j, :, base:base + POOL_GROUP_DIM] for j in range(POOL_HIST)]
        seq += [proj_ref[:, t * IN_DIM + base:t * IN_DIM + base + POOL_GROUP_DIM]
                for t in range(DEC_SEQ)]
        for t in range(DEC_SEQ):
            last = POOL_HIST + t
            s = seq[last - w + 1]
            for j in range(last - w + 2, last + 1):
                s = s + seq[j]
            cnt = float(min(w, N_HIST_SAMPLE + t + 1))
            d_ref[:, t * POOL_DIM + base:t * POOL_DIM + base + POOL_GROUP_DIM] = (
                s / cnt - seq[last])
    for j in range(POOL_HIST - DEC_SEQ):
        newpool_ref[j] = hist_ref[j + DEC_SEQ]
    for t in range(DEC_SEQ):
        newpool_ref[POOL_HIST - DEC_SEQ + t] = proj_ref[:, t * IN_DIM:t * IN_DIM + POOL_DIM]


def _sample_pool(layer, state_t, proj_flat):
    return pl.pallas_call(
        _sample_pool_body,
        out_shape=(jax.ShapeDtypeStruct((DEC_BATCH, DEC_SEQ * POOL_DIM), F32),
                   jax.ShapeDtypeStruct((POOL_HIST, DEC_BATCH, POOL_DIM), F32)),
        grid=(DEC_BATCH // POOL_SB,),
        in_specs=[
            pl.BlockSpec((None, POOL_HIST, POOL_SB, POOL_DIM), lambda i: (layer, 0, i, 0)),
            pl.BlockSpec((POOL_SB, DEC_SEQ * IN_DIM), lambda i: (i, 0)),
        ],
        out_specs=(pl.BlockSpec((POOL_SB, DEC_SEQ * POOL_DIM), lambda i: (i, 0)),
                   pl.BlockSpec((POOL_HIST, POOL_SB, POOL_DIM), lambda i: (0, i, 0))),
        compiler_params=_params("parallel"),
        name="sample_pool",
    )(state_t, proj_flat)


def _sample_attn_body(*refs, update_cache):
    if update_cache:
        (sink_ref, d_ref, q_ref, kvt_ref, ck_ref, cv_ref, x_ref, pw_ref, ps_ref, wo_ref,
         o_ref, nk_ref, nv_ref, mix_ref, s_ref, p_ref) = refs
    else:
        (sink_ref, d_ref, q_ref, kvt_ref, ck_ref, cv_ref, x_ref, pw_ref, ps_ref, wo_ref,
         o_ref, mix_ref, s_ref, p_ref) = refs

    lane = lax.broadcasted_iota(jnp.int32, (KV_DIM, N_BUF), 1)
    is_old = lane < N_BUF - DEC_SEQ

    def shift_in(b, kt, vt, kt_new, vt_new):
        shift_new = (N_BUF - DEC_SEQ - b * DEC_SEQ) % N_BUF
        nk_ref[b] = jnp.where(is_old, pltpu.roll(kt, N_BUF - DEC_SEQ, axis=1),
                              pltpu.roll(kt_new, shift_new, axis=1))
        nv_ref[b] = jnp.where(is_old, pltpu.roll(vt, N_BUF - DEC_SEQ, axis=1),
                              pltpu.roll(vt_new, shift_new, axis=1))

    def attend():
        _sample_attend(sink_ref, d_ref, q_ref, kvt_ref, ck_ref, cv_ref, x_ref, pw_ref,
                       ps_ref, wo_ref, o_ref, mix_ref, s_ref, p_ref,
                       shift_in if update_cache else None)

    if not update_cache:
        attend()
        return

    phase = pl.program_id(0)

    @pl.when(phase < DEPTH - 1)
    def _():
        kt_new = kvt_ref[:KV_DIM, :]
        vt_new = kvt_ref[KV_DIM:, :]
        for b in range(ATTN_SB):
            shift_in(b, ck_ref[b], cv_ref[b], kt_new, vt_new)

    @pl.when(phase == DEPTH - 1)
    def _():
        attend()


def _sample_attend(sink_ref, d_ref, q_ref, kvt_ref, ck_ref, cv_ref, x_ref, pw_ref, ps_ref,
                   wo_ref, o_ref, mix_ref, s_ref, p_ref, shift_in):
    for gi in range(len(POOL_WINDOWS)):
        lanes = slice(gi * POOL_GROUP_DIM, (gi + 1) * POOL_GROUP_DIM)
        mix_ref[:, lanes] = _pool_linear(d_ref[:, lanes], gi, pw_ref, ps_ref).astype(BF16)

    pair = 2 * DEC_SEQ
    grp = GROUP * pair
    rows = N_KV_HEADS * grp
    new_cols = ATTN_SB * DEC_SEQ
    keys = N_BUF + new_cols
    assert keys == KV_DIM and new_cols == N_BUF
    head = _kv_head_of_lane((rows, KV_DIM))
    row = lax.broadcasted_iota(jnp.int32, (rows, keys), 0)
    row_head = row // grp
    in_pair = row % pair
    low = in_pair < DEC_SEQ
    tok = in_pair % DEC_SEQ
    col = lax.broadcasted_iota(jnp.int32, (rows, keys), 1)
    new_col = col - N_BUF
    valid_cache = (col < N_BUF) & (col > tok)
    causal_new = (new_col >= 0) & ((new_col % DEC_SEQ) <= tok)
    seq_of_col = new_col // DEC_SEQ - in_pair // DEC_SEQ
    sink = jnp.concatenate(
        [jnp.full((pair, 1), sink_ref[k * GROUP + g], F32)
         for k in range(N_KV_HEADS) for g in range(GROUP)], axis=0)
    kt_new = kvt_ref[:KV_DIM, :]
    vt_new = kvt_ref[KV_DIM:, :]
    kt_new_bf = kt_new.astype(BF16)
    vt_new_bf = vt_new.astype(BF16)
    nt = (((1,), (1,)), ((), ()))

    n_pairs = ATTN_SB // 2
    for j in range(n_pairs):
        r0 = j * pair
        q8 = q_ref[r0:r0 + pair, :]
        qs = jnp.concatenate(
            [q8[:, g * KV_DIM:(g + 1) * KV_DIM] for g in range(GROUP)], axis=0)
        qrep = jnp.concatenate([qs] * N_KV_HEADS, axis=0)
        lhs = jnp.where(head == row_head, qrep, 0.0).astype(BF16)
        scores = [jnp.dot(lhs, ck_ref[b].astype(BF16), preferred_element_type=F32)
                  for b in (2 * j, 2 * j + 1)]
        s_new = jnp.dot(lhs, kt_new_bf, preferred_element_type=F32)
        s = jnp.concatenate([jnp.where(low[:, :N_BUF], scores[0], scores[1]), s_new],
                            axis=1)
        valid = valid_cache | (causal_new & (seq_of_col == 2 * j))
        s_ref[j * rows:(j + 1) * rows, :] = jnp.where(valid, s, -jnp.inf)

    for j in range(n_pairs):
        s = s_ref[j * rows:(j + 1) * rows, :]
        m = jnp.maximum(jnp.max(s, axis=-1, keepdims=True), sink)
        p = jnp.exp(s - m)
        den = jnp.sum(p, axis=-1, keepdims=True) + jnp.exp(sink - m)
        p_ref[j * rows:(j + 1) * rows, :] = (p * (1.0 / den)).astype(BF16)

    for j in range(n_pairs):
        r0 = j * pair
        p = p_ref[j * rows:(j + 1) * rows, :]
        p_old = p[:, :N_BUF]
        outs = []
        for b in (2 * j, 2 * j + 1):
            kt = ck_ref[b]
            vt = cv_ref[b]
            outs.append(lax.dot_general(p_old, vt.astype(BF16), nt,
                                        preferred_element_type=F32))
            if shift_in is not None:
                shift_in(b, kt, vt, kt_new, vt_new)
        o = jnp.where(low, outs[0], outs[1])
        o = o + lax.dot_general(p[:, N_BUF:], vt_new_bf, nt, preferred_element_type=F32)
        o = jnp.where(head == row_head, o, 0.0)
        og = o[0:grp]
        for k in range(1, N_KV_HEADS):
            og = og + o[k * grp:(k + 1) * grp]
        for g in range(GROUP):
            mix_ref[r0:r0 + pair,
                    POOL_DIM + g * KV_DIM:POOL_DIM + (g + 1) * KV_DIM] = (
                        og[g * pair:(g + 1) * pair].astype(BF16))

    o_ref[...] = x_ref[...] + jnp.dot(mix_ref[...], wo_ref[...],
                                      preferred_element_type=F32)


def _sample_attn(layer, sinks, d_rows, proj, kvts, cache_kt, cache_vt, x, pool_w,
                 pool_scale, w_out):
    phases = kvts.shape[0]
    update_cache = phases > 1
    score_shape = (ATTN_SB // 2 * N_HEADS * 2 * DEC_SEQ, N_BUF + ATTN_ROWS)
    assert phases == 1 or (phases == DEPTH and layer == DEPTH - 1)
    layer_of = (lambda p: p) if update_cache else (lambda p: layer)
    row = lambda p, i: jnp.where(p == phases - 1, i, 0)
    cache_spec = pl.BlockSpec((None, ATTN_SB, KV_DIM, N_BUF),
                              lambda p, i: (layer_of(p), i, 0, 0))
    out_shape = [jax.ShapeDtypeStruct((SAMPLE_ROWS, D_MODEL), F32)]
    out_specs = [pl.BlockSpec((ATTN_ROWS, D_MODEL), lambda p, i: (row(p, i), 0))]
    if update_cache:
        out_shape += [jax.ShapeDtypeStruct((DEPTH, DEC_BATCH, KV_DIM, N_BUF), F32)] * 2
        out_specs += [pl.BlockSpec((None, ATTN_SB, KV_DIM, N_BUF),
                                   lambda p, i: (p, i, 0, 0))] * 2
    return pl.pallas_call(
        functools.partial(_sample_attn_body, update_cache=update_cache),
        out_shape=out_shape,
        grid=(phases, DEC_BATCH // ATTN_SB),
        in_specs=[
            pl.BlockSpec(memory_space=pltpu.SMEM),
            pl.BlockSpec((ATTN_ROWS, POOL_DIM), lambda p, i: (row(p, i), 0)),
            pl.BlockSpec((ATTN_ROWS, Q_DIM), lambda p, i: (row(p, i), 1)),
            pl.BlockSpec((None, 2 * KV_DIM, ATTN_ROWS), lambda p, i: (p, 0, i)),
            cache_spec,
            cache_spec,
            pl.BlockSpec((ATTN_ROWS, D_MODEL), lambda p, i: (row(p, i), 0)),
            pl.BlockSpec((len(POOL_WINDOWS), POOL_GROUP_DIM, POOL_GROUP_DIM),
                         lambda p, i: (0, 0, 0)),
            pl.BlockSpec((1, POOL_DIM), lambda p, i: (0, 0)),
            pl.BlockSpec((D_MODEL, D_MODEL), lambda p, i: (0, 0),
                         pipeline_mode=pl.Buffered(1)),
        ],
        out_specs=out_specs,
        scratch_shapes=[pltpu.VMEM((ATTN_ROWS, D_MODEL), BF16),
                        pltpu.VMEM(score_shape, F32),
                        pltpu.VMEM(score_shape, BF16)],
        compiler_params=_params("arbitrary", "arbitrary"),
        name="sample_attn",
    )(sinks, d_rows, proj, kvts, cache_kt, cache_vt, x, pool_w, pool_scale, w_out)


def kernel(x_prompt, x_sample, cache_k, cache_v, state_pool, norm_ffn1, ffn1_gate,
           ffn1_up, ffn1_down, norm_mix, w_in, pool_w, pool_scale, attn_sinks,
           w_out, norm_ffn2, ffn2_gate, ffn2_up, ffn2_down, final_norm):
    xp = x_prompt.reshape(PROMPT_ROWS, D_MODEL)
    xs = x_sample.reshape(SAMPLE_ROWS, D_MODEL)
    gf = final_norm.reshape(1, D_MODEL)
    state_t = jnp.transpose(state_pool, (0, 2, 1, 3))

    def to_cache_t(c):
        return jnp.transpose(c, (0, 1, 3, 4, 2)).reshape(DEPTH, DEC_BATCH, KV_DIM, N_BUF)

    def from_cache_t(c):
        c = c.reshape(DEPTH, DEC_BATCH, N_KV_HEADS, HEAD_DIM, N_BUF)
        return jnp.transpose(c, (0, 1, 4, 2, 3))

    cache_kt = to_cache_t(cache_k)
    cache_vt = to_cache_t(cache_v)
    keep = min(WINDOW, SEQ)
    assert keep >= POOL_HIST
    k0 = POOL_DIM + Q_DIM

    ffn1_w = (ffn1_gate, ffn1_up, ffn1_down)
    ffn2_w = (ffn2_gate, ffn2_up, ffn2_down)
    head, *w_bf = _ffn_head(0, xp, norm_ffn1[0].reshape(1, D_MODEL), *ffn1_w)
    kp_l, vp_l, pp_l, ps_l, kvt_l = [], [], [], [], []
    for l in range(DEPTH):
        last = l == DEPTH - 1
        pool_w_l = pool_w[l].astype(BF16)
        pool_scale_l = pool_scale[l].reshape(1, POOL_DIM)
        sinks_l = attn_sinks[l].astype(F32)

        xp, xs, w_bf, (w_in_l, w_out_l) = _ffn(
            xp, xs, norm_ffn1[l].reshape(1, D_MODEL), *w_bf, gf,
            next_w=(l,) + ffn2_w, mix_w=(l, w_in, w_out),
            head=head if l == 0 else None)
        proj_p, proj_s, kvt = _inproj(xp, xs, norm_mix[l].reshape(1, D_MODEL), w_in_l)

        tails = [proj_p[(b + 1) * SEQ - keep:(b + 1) * SEQ] for b in range(BATCH)]
        kp_l.append(jnp.stack([t[:, k0:k0 + KV_DIM] for t in tails])
                    .reshape(BATCH, keep, N_KV_HEADS, HEAD_DIM))
        vp_l.append(jnp.stack([t[:, k0 + KV_DIM:] for t in tails])
                    .reshape(BATCH, keep, N_KV_HEADS, HEAD_DIM))
        pp_l.append(jnp.stack([t[keep - POOL_HIST:, :POOL_DIM] for t in tails]))

        d_flat, new_pool = _sample_pool(
            l, state_t, proj_s.reshape(DEC_BATCH, DEC_SEQ * IN_DIM))
        ps_l.append(new_pool)

        xp = _prompt_mix(sinks_l, proj_p, xp, pool_w_l, pool_scale_l, w_out_l)
        kvt_l.append(kvt)
        outs = _sample_attn(
            l, sinks_l, d_flat.reshape(SAMPLE_ROWS, POOL_DIM), proj_s,
            jnp.stack(kvt_l) if last else kvt[None],
            cache_kt, cache_vt, xs, pool_w_l, pool_scale_l, w_out_l)
        xs = outs[0]
        if last:
            new_kt, new_vt = outs[1:]

        xp, xs, w_bf, _ = _ffn(xp, xs, norm_ffn2[l].reshape(1, D_MODEL), *w_bf, gf,
                               next_w=None if last else (l + 1,) + ffn1_w,
                               final_norm=last)

    y_prompt = xp.reshape(BATCH, SEQ, D_MODEL)
    y_sample = xs.reshape(DEC_BATCH, DEC_SEQ, D_MODEL)
    new_pool_sample = jnp.transpose(jnp.stack(ps_l), (0, 2, 1, 3))
    return (y_prompt, y_sample, jnp.stack(kp_l), jnp.stack(vp_l), jnp.stack(pp_l),
            from_cache_t(new_kt), from_cache_t(new_vt), new_pool_sample)
```

```python
import functools

import jax
import jax.numpy as jnp
from jax import lax
from jax.experimental import pallas as pl
from jax.experimental.pallas import tpu as pltpu

F32 = jnp.float32
BF16 = jnp.bfloat16

D_MODEL = 2048
BATCH = 2
SEQ = 4096
DEPTH = 2
DEC_BATCH = 128
DEC_SEQ = 4
PAST_LEN = 8192

POOL_DIM = 1024
POOL_WINDOWS = (2, 4, 8, 16)
POOL_GROUP_DIM = POOL_DIM // len(POOL_WINDOWS)
POOL_HIST = max(POOL_WINDOWS) - 1
HEAD_DIM = 64
N_HEADS = 16
N_KV_HEADS = 4
GROUP = N_HEADS // N_KV_HEADS
Q_DIM = N_HEADS * HEAD_DIM
KV_DIM = N_KV_HEADS * HEAD_DIM
IN_DIM = POOL_DIM + Q_DIM + 2 * KV_DIM
WINDOW = 128
FFN_DIM = 5632
RMS_EPS = 1e-5
ATTN_SCALE = HEAD_DIM ** -0.5
N_BUF = min(WINDOW, PAST_LEN)
N_HIST_SAMPLE = min(POOL_HIST, PAST_LEN)

PROMPT_ROWS = BATCH * SEQ
SAMPLE_ROWS = DEC_BATCH * DEC_SEQ
M_ROWS = PROMPT_ROWS + SAMPLE_ROWS

VMEM_LIMIT_BYTES = 56 * 1024 * 1024
LANES = 128

FFN_TM = 1088
FFN_TF = 512
FFN_TILES = M_ROWS // FFN_TM
FFN_CHUNKS = FFN_DIM // FFN_TF
FFN_FULL_TILES = PROMPT_ROWS // FFN_TM
FFN_SPLIT = PROMPT_ROWS - FFN_FULL_TILES * FFN_TM
FFN_TF_HEAD = 256
WIN_ROWS = 32
WIN_BLOCKS = D_MODEL // WIN_ROWS
WOUT_BLOCKS = D_MODEL // HEAD_DIM
PROJ_TM = 512
PROJ_PROMPT_TILES = PROMPT_ROWS // PROJ_TM
MIX_TQ = 512
HALO_U = 16
ATT_CHUNK = 64
POOL_SB = 32
ATTN_SB = 32
ATTN_ROWS = ATTN_SB * DEC_SEQ

assert FFN_TILES * FFN_TM == M_ROWS and FFN_CHUNKS * FFN_TF == FFN_DIM
assert FFN_FULL_TILES + 1 == FFN_TILES and FFN_SPLIT + SAMPLE_ROWS == FFN_TM
assert FFN_SPLIT % 8 == 0 and FFN_FULL_TILES >= 1
assert FFN_CHUNKS >= 2
assert FFN_DIM % FFN_TF_HEAD == 0
assert SAMPLE_ROWS == PROJ_TM and PROJ_PROMPT_TILES * PROJ_TM == PROMPT_ROWS


def _rms(x, g):
    return x * lax.rsqrt(jnp.mean(x * x, axis=-1, keepdims=True) + RMS_EPS) * g


def _params(*sem):
    return pltpu.CompilerParams(dimension_semantics=sem,
                                vmem_limit_bytes=VMEM_LIMIT_BYTES)


def _ffn_body(*refs, final_norm, cast_next, prep_mix, t0, n_tiles, head_in):
    xp_hbm, xs_hbm, g_ref, wg_ref, wu_ref, wd_ref, gf_ref = refs[:7]
    refs = refs[7:]
    if cast_next:
        ng_ref, nu_ref, nd_ref = refs[:3]
        refs = refs[3:]
    if prep_mix:
        win_ref, wout_ref = refs[:2]
        refs = refs[2:]
    if head_in:
        head_hbm = refs[0]
        refs = refs[1:]
    op_hbm, os_hbm = refs[:2]
    refs = refs[2:]
    if cast_next:
        cg_ref, cu_ref, cd_ref = refs[:3]
        refs = refs[3:]
    if prep_mix:
        cwin_ref, cwout_ref = refs[:2]
        refs = refs[2:]
    acc_ref, xn_ref, in_sem, out_sem = refs[:4]
    head_sem = refs[4] if head_in else None

    i = pl.program_id(0)
    f = pl.program_id(1)
    slot = i % 2
    tile = i + t0

    def head_copy():
        return pltpu.make_async_copy(head_hbm, op_hbm.at[pl.ds(0, t0 * FFN_TM)], head_sem)

    def copies(p_hbm, s_hbm, tile, s, sem, mixed, to_vmem):
        if mixed:
            pairs = [(p_hbm.at[pl.ds(FFN_FULL_TILES * FFN_TM, FFN_SPLIT)],
                      acc_ref.at[s, pl.ds(0, FFN_SPLIT)]),
                     (s_hbm, acc_ref.at[s, pl.ds(FFN_SPLIT, SAMPLE_ROWS)])]
        else:
            pairs = [(p_hbm.at[pl.ds(tile * FFN_TM, FFN_TM)], acc_ref.at[s])]
        return [pltpu.make_async_copy(h, v, sem.at[s]) if to_vmem
                else pltpu.make_async_copy(v, h, sem.at[s]) for h, v in pairs]

    def on_tile(tile, s, op, to_vmem):
        p_hbm, s_hbm, sem = ((xp_hbm, xs_hbm, in_sem) if to_vmem
                             else (op_hbm, os_hbm, out_sem))

        @pl.when(tile < FFN_FULL_TILES)
        def _():
            for c in copies(p_hbm, s_hbm, tile, s, sem, False, to_vmem):
                op(c)

        @pl.when(tile == FFN_FULL_TILES)
        def _():
            for c in copies(p_hbm, s_hbm, tile, s, sem, True, to_vmem):
                op(c)

    start = lambda c: c.start()
    wait = lambda c: c.wait()

    @pl.when((i == 0) & (f == 0))
    def _():
        on_tile(tile, slot, start, True)
        if head_in:
            head_copy().start()

    @pl.when(f == 0)
    def _():
        on_tile(tile, slot, wait, True)
        xn_ref[...] = _rms(acc_ref[slot], g_ref[...]).astype(BF16)

    a = xn_ref[...]
    gate = jnp.dot(a, wg_ref[...], preferred_element_type=F32)
    if cast_next:
        cg_ref[...] = ng_ref[...].astype(BF16)
        cu_ref[...] = nu_ref[...].astype(BF16)
    up = jnp.dot(a, wu_ref[...], preferred_element_type=F32)
    if cast_next:
        cd_ref[...] = nd_ref[...].astype(BF16)
    if prep_mix:
        _prep_w_in(win_ref, cwin_ref)
        cwout_ref[...] = wout_ref[...].astype(BF16)
    h = (gate * jax.nn.sigmoid(gate) * up).astype(BF16)
    acc_ref[slot] += 0.5 * jnp.dot(h, wd_ref[...], preferred_element_type=F32)

    @pl.when(f == 1)
    def _():
        @pl.when(i >= 1)
        def _():
            on_tile(tile - 1, 1 - slot, wait, False)

        @pl.when(i + 1 < n_tiles)
        def _():
            on_tile(tile + 1, 1 - slot, start, True)

    @pl.when(f == FFN_CHUNKS - 1)
    def _():
        if final_norm:
            acc_ref[slot] = _rms(acc_ref[slot], gf_ref[...])
        on_tile(tile, slot, start, False)

        @pl.when(i == n_tiles - 1)
        def _():
            on_tile(tile, slot, wait, False)
            if head_in:
                head_copy().wait()


def _prep_w_in(w_ref, o_ref):
    k0 = POOL_DIM + Q_DIM
    o_ref[:, :POOL_DIM] = w_ref[:, :POOL_DIM].astype(BF16)
    o_ref[:, k0:] = w_ref[:, k0:].astype(BF16)
    lane = lax.broadcasted_iota(jnp.int32, (WIN_ROWS, 2 * HEAD_DIM), 1)
    for g in range(GROUP):
        for kp in range(N_KV_HEADS // 2):
            src = [POOL_DIM + (2 * kp + j) * KV_DIM + (g // 2) * 2 * HEAD_DIM
                   for j in (0, 1)]
            lo, hi = [w_ref[:, c:c + 2 * HEAD_DIM] for c in src]
            if g % 2 == 0:
                hi = pltpu.roll(hi, HEAD_DIM, axis=1)
            else:
                lo = pltpu.roll(lo, HEAD_DIM, axis=1)
            dst = POOL_DIM + g * KV_DIM + kp * 2 * HEAD_DIM
            o_ref[:, dst:dst + 2 * HEAD_DIM] = (
                jnp.where(lane < HEAD_DIM, lo, hi) * ATTN_SCALE).astype(BF16)


def _w_out_src_block(out_blk):
    pool_blocks = POOL_DIM // HEAD_DIM
    h = out_blk - pool_blocks
    src = pool_blocks + (h % N_KV_HEADS) * GROUP + h // N_KV_HEADS
    return jnp.where(out_blk < pool_blocks, out_blk, src)


def _ffn(xp, xs, g, wg, wu, wd, gf, next_w=None, mix_w=None, *, final_norm=False,
         head=None):
    cast_next = next_w is not None
    prep_mix = mix_w is not None
    head_in = head is not None
    t0 = head.shape[0] // FFN_TM if head_in else 0
    n_tiles = FFN_TILES - t0
    cast_blocks = 1 << (n_tiles.bit_length() - 1)
    cast_rows = D_MODEL // cast_blocks
    win_steps = pl.cdiv(WIN_BLOCKS, n_tiles)
    wout_steps = pl.cdiv(WOUT_BLOCKS, n_tiles)
    assert max(win_steps, wout_steps) <= FFN_CHUNKS
    cast_blk = lambda i: jnp.minimum(i, cast_blocks - 1)
    cast_chunk = lambda i, f: jnp.where(i < cast_blocks, f, FFN_CHUNKS - 1)
    vec = pl.BlockSpec((1, D_MODEL), lambda i, f: (0, 0))
    in_specs = [
        pl.BlockSpec(memory_space=pl.ANY),
        pl.BlockSpec(memory_space=pl.ANY),
        vec,
        pl.BlockSpec((D_MODEL, FFN_TF), lambda i, f: (0, f)),
        pl.BlockSpec((D_MODEL, FFN_TF), lambda i, f: (0, f)),
        pl.BlockSpec((FFN_TF, D_MODEL), lambda i, f: (f, 0)),
        vec,
    ]
    out_shape = [jax.ShapeDtypeStruct((PROMPT_ROWS, D_MODEL), F32),
                 jax.ShapeDtypeStruct((SAMPLE_ROWS, D_MODEL), F32)]
    out_specs = [pl.BlockSpec(memory_space=pl.ANY), pl.BlockSpec(memory_space=pl.ANY)]
    args = [xp, xs, g, wg, wu, wd, gf]
    if cast_next:
        nl = next_w[0]
        in_specs += [
            pl.BlockSpec((None, cast_rows, FFN_TF),
                         lambda i, f: (nl, cast_blk(i), cast_chunk(i, f))),
            pl.BlockSpec((None, cast_rows, FFN_TF),
                         lambda i, f: (nl, cast_blk(i), cast_chunk(i, f))),
            pl.BlockSpec((None, FFN_TF, cast_rows),
                         lambda i, f: (nl, cast_chunk(i, f), cast_blk(i))),
        ]
        out_shape += [jax.ShapeDtypeStruct((D_MODEL, FFN_DIM), BF16),
                      jax.ShapeDtypeStruct((D_MODEL, FFN_DIM), BF16),
                      jax.ShapeDtypeStruct((FFN_DIM, D_MODEL), BF16)]
        out_specs += [
            pl.BlockSpec((cast_rows, FFN_TF), lambda i, f: (cast_blk(i), cast_chunk(i, f))),
            pl.BlockSpec((cast_rows, FFN_TF), lambda i, f: (cast_blk(i), cast_chunk(i, f))),
            pl.BlockSpec((FFN_TF, cast_rows), lambda i, f: (cast_chunk(i, f), cast_blk(i))),
        ]
        args += list(next_w[1:])
    if prep_mix:
        ml = mix_w[0]
        win_blk = lambda i, f: jnp.minimum(
            i * win_steps + jnp.minimum(f, win_steps - 1), WIN_BLOCKS - 1)
        wout_blk = lambda i, f: jnp.minimum(
            i * wout_steps + jnp.minimum(f, wout_steps - 1), WOUT_BLOCKS - 1)
        in_specs += [
            pl.BlockSpec((None, WIN_ROWS, IN_DIM), lambda i, f: (ml, win_blk(i, f), 0)),
            pl.BlockSpec((None, HEAD_DIM, D_MODEL),
                         lambda i, f: (ml, _w_out_src_block(wout_blk(i, f)), 0)),
        ]
        out_shape += [jax.ShapeDtypeStruct((D_MODEL, IN_DIM), BF16),
                      jax.ShapeDtypeStruct((D_MODEL, D_MODEL), BF16)]
        out_specs += [
            pl.BlockSpec((WIN_ROWS, IN_DIM), lambda i, f: (win_blk(i, f), 0)),
            pl.BlockSpec((HEAD_DIM, D_MODEL), lambda i, f: (wout_blk(i, f), 0)),
        ]
        args += list(mix_w[1:])
    scratch = [pltpu.VMEM((2, FFN_TM, D_MODEL), F32),
               pltpu.VMEM((FFN_TM, D_MODEL), BF16),
               pltpu.SemaphoreType.DMA((2,)),
               pltpu.SemaphoreType.DMA((2,))]
    if head_in:
        in_specs.append(pl.BlockSpec(memory_space=pl.ANY))
        args.append(head)
        scratch.append(pltpu.SemaphoreType.DMA(()))
    outs = pl.pallas_call(
        functools.partial(_ffn_body, final_norm=final_norm, cast_next=cast_next,
                          prep_mix=prep_mix, t0=t0, n_tiles=n_tiles, head_in=head_in),
        out_shape=out_shape,
        grid=(n_tiles, FFN_CHUNKS),
        in_specs=in_specs,
        out_specs=out_specs,
        scratch_shapes=scratch,
        compiler_params=_params("arbitrary", "arbitrary"),
        name="ffn_final" if final_norm else "ffn",
    )(*args)
    n_cast = 3 if cast_next else 0
    return outs[0], outs[1], tuple(outs[2:2 + n_cast]), tuple(outs[2 + n_cast:])


def _ffn_head_body(x_ref, g_ref, wg_ref, wu_ref, wd_ref, o_ref, og_ref, ou_ref, od_ref,
                   xn_ref, wgs_ref, wus_ref, wds_ref):
    step = pl.program_id(0)
    slot = step % 2

    @pl.when(step == 0)
    def _():
        x = x_ref[...]
        xn_ref[...] = _rms(x, g_ref[...]).astype(BF16)
        o_ref[...] = x
        wgs_ref[1] = jnp.zeros(wgs_ref.shape[1:], BF16)
        wus_ref[1] = jnp.zeros(wus_ref.shape[1:], BF16)
        wds_ref[1] = jnp.zeros(wds_ref.shape[1:], BF16)

    a = xn_ref[...]
    wg = wg_ref[...].astype(BF16)
    wgs_ref[slot] = wg
    og_ref[...] = wg
    gate = jnp.dot(a, wgs_ref[1 - slot], preferred_element_type=F32)
    wu = wu_ref[...].astype(BF16)
    wus_ref[slot] = wu
    ou_ref[...] = wu
    up = jnp.dot(a, wus_ref[1 - slot], preferred_element_type=F32)
    wd = wd_ref[...].astype(BF16)
    wds_ref[slot] = wd
    od_ref[...] = wd
    h = (gate * jax.nn.sigmoid(gate) * up).astype(BF16)
    o_ref[...] += 0.5 * jnp.dot(h, wds_ref[1 - slot], preferred_element_type=F32)


def _ffn_head(layer, xp, g, gate, up, down):
    tf = FFN_TF_HEAD
    n_chunks = FFN_DIM // tf
    fixed = lambda s: (0, 0)
    chunk = lambda s: jnp.minimum(s, n_chunks - 1)
    return pl.pallas_call(
        _ffn_head_body,
        out_shape=(jax.ShapeDtypeStruct((FFN_TM, D_MODEL), F32),
                   jax.ShapeDtypeStruct((D_MODEL, FFN_DIM), BF16),
                   jax.ShapeDtypeStruct((D_MODEL, FFN_DIM), BF16),
                   jax.ShapeDtypeStruct((FFN_DIM, D_MODEL), BF16)),
        grid=(n_chunks + 1,),
        in_specs=[
            pl.BlockSpec((FFN_TM, D_MODEL), fixed, pipeline_mode=pl.Buffered(1)),
            pl.BlockSpec((1, D_MODEL), fixed),
            pl.BlockSpec((None, D_MODEL, tf), lambda s: (layer, 0, chunk(s))),
            pl.BlockSpec((None, D_MODEL, tf), lambda s: (layer, 0, chunk(s))),
            pl.BlockSpec((None, tf, D_MODEL), lambda s: (layer, chunk(s), 0)),
        ],
        out_specs=(pl.BlockSpec((FFN_TM, D_MODEL), fixed),
                   pl.BlockSpec((D_MODEL, tf), lambda s: (0, chunk(s))),
                   pl.BlockSpec((D_MODEL, tf), lambda s: (0, chunk(s))),
                   pl.BlockSpec((tf, D_MODEL), lambda s: (chunk(s), 0))),
        scratch_shapes=[pltpu.VMEM((FFN_TM, D_MODEL), BF16),
                        pltpu.VMEM((2, D_MODEL, tf), BF16),
                        pltpu.VMEM((2, D_MODEL, tf), BF16),
                        pltpu.VMEM((2, tf, D_MODEL), BF16)],
        compiler_params=_params("arbitrary"),
        name="ffn_head",
    )(xp, g, gate, up, down)


def _inproj_body(xp_ref, xs_ref, g_ref, w_ref, hist_ref, pp_ref, ps_ref, kvt_ref, d_ref,
                 ut_ref, u_scr, d_scr):
    i = pl.program_id(0)

    @pl.when(i < PROJ_PROMPT_TILES)
    def _():
        xn = _rms(xp_ref[...], g_ref[...]).astype(BF16)
        pp_ref[...] = jnp.dot(xn, w_ref[...], preferred_element_type=F32)

    @pl.when(i == PROJ_PROMPT_TILES)
    def _():
        xn = _rms(xs_ref[...], g_ref[...]).astype(BF16)
        proj = jnp.dot(xn, w_ref[...], preferred_element_type=F32)
        ps_ref[...] = proj
        kvt_ref[...] = proj[:, POOL_DIM + Q_DIM:].T
        token_rows = [pl.ds(t, DEC_BATCH, stride=DEC_SEQ) for t in range(DEC_SEQ)]
        for c in range(POOL_DIM // LANES):
            lanes = slice(c * LANES, (c + 1) * LANES)
            w = POOL_WINDOWS[c * LANES // POOL_GROUP_DIM]
            u_scr[c] = proj[:, lanes]
            seq = [hist_ref[j, :, lanes] for j in range(POOL_HIST)]
            seq += [u_scr[c, rows, :] for rows in token_rows]
            for t in range(DEC_SEQ):
                last = POOL_HIST + t
                s = seq[last - w + 1]
                for j in range(last - w + 2, last + 1):
                    s = s + seq[j]
                cnt = float(min(w, N_HIST_SAMPLE + t + 1))
                d_scr[c, token_rows[t], :] = s / cnt - seq[last]
                ut_ref[t, :, lanes] = seq[last]
            d_ref[:, lanes] = d_scr[c]


def _inproj(layer, xp, xs, g, w, state_t):
    prompt_tile = lambda i: (jnp.minimum(i, PROJ_PROMPT_TILES - 1), 0)
    fixed = lambda i: (0, 0)
    return pl.pallas_call(
        _inproj_body,
        out_shape=(jax.ShapeDtypeStruct((PROMPT_ROWS, IN_DIM), F32),
                   jax.ShapeDtypeStruct((SAMPLE_ROWS, IN_DIM), F32),
                   jax.ShapeDtypeStruct((2 * KV_DIM, SAMPLE_ROWS), F32),
                   jax.ShapeDtypeStruct((SAMPLE_ROWS, POOL_DIM), F32),
                   jax.ShapeDtypeStruct((DEC_SEQ, DEC_BATCH, POOL_DIM), F32)),
        grid=(PROJ_PROMPT_TILES + 1,),
        in_specs=[
            pl.BlockSpec((PROJ_TM, D_MODEL), prompt_tile),
            pl.BlockSpec((PROJ_TM, D_MODEL), fixed),
            pl.BlockSpec((1, D_MODEL), fixed),
            pl.BlockSpec((D_MODEL, IN_DIM), fixed),
            pl.BlockSpec((None, POOL_HIST, DEC_BATCH, POOL_DIM),
                         lambda i: (layer, 0, 0, 0)),
        ],
        out_specs=(pl.BlockSpec((PROJ_TM, IN_DIM), prompt_tile),
                   pl.BlockSpec((PROJ_TM, IN_DIM), fixed),
                   pl.BlockSpec((2 * KV_DIM, SAMPLE_ROWS), fixed),
                   pl.BlockSpec((SAMPLE_ROWS, POOL_DIM), fixed),
                   pl.BlockSpec((DEC_SEQ, DEC_BATCH, POOL_DIM), lambda i: (0, 0, 0))),
        scratch_shapes=[pltpu.VMEM((POOL_DIM // LANES, SAMPLE_ROWS, LANES), F32),
                        pltpu.VMEM((POOL_DIM // LANES, SAMPLE_ROWS, LANES), F32)],
        compiler_params=_params("arbitrary"),
        name="inproj",
    )(xp, xs, g, w, state_t)


def _kv_head_of_lane(shape):
    return lax.broadcasted_iota(jnp.int32, shape, 1) // HEAD_DIM


def _pool_linear(d, gi, pw_ref, ps_ref):
    lanes = slice(gi * POOL_GROUP_DIM, (gi + 1) * POOL_GROUP_DIM)
    y = jnp.dot(d.astype(BF16), pw_ref[gi], preferred_element_type=F32)
    return y * ps_ref[:, lanes]


def _prompt_mix_body(sink_ref, u_ref, q_ref, kv_ref, kvh_ref, uh_ref, x_ref,
                     pw_ref, ps_ref, wo_ref, o_ref, mix_ref, s_ref, p_ref):
    tiles_per_seq = SEQ // MIX_TQ
    tile = pl.program_id(0) % tiles_per_seq
    first = tile == 0

    lo = jnp.where(first, WINDOW, 0)
    keep_u = jnp.full((HALO_U, POOL_DIM), lo, jnp.int32) == 0
    keep_kv = jnp.full((WINDOW, 2 * KV_DIM), lo, jnp.int32) == 0

    def pool_mixer():
        uh = jnp.where(keep_u, uh_ref[...], 0.0)
        pos = tile * MIX_TQ + lax.broadcasted_iota(jnp.int32, (MIX_TQ, 1), 0)
        for gi, w in enumerate(POOL_WINDOWS):
            lanes = slice(gi * POOL_GROUP_DIM, (gi + 1) * POOL_GROUP_DIM)
            xg = jnp.concatenate([uh[:, lanes], u_ref[:, lanes]], axis=0)
            s = xg
            sh = 1
            while sh < w:
                s = s + pltpu.roll(s, sh, axis=0)
                sh *= 2
            cnt = jnp.minimum(w, pos + 1).astype(F32)
            d = s[HALO_U:] / cnt - xg[HALO_U:]
            mix_ref[:, lanes] = _pool_linear(d, gi, pw_ref, ps_ref).astype(BF16)

    pool_mixer()

    kvh = jnp.where(keep_kv, kvh_ref[...], 0.0)
    k_all = jnp.concatenate([kvh[:, :KV_DIM], kv_ref[:, :KV_DIM]], axis=0)
    v_all = jnp.concatenate([kvh[:, KV_DIM:], kv_ref[:, KV_DIM:]], axis=0)
    head = _kv_head_of_lane((2 * WINDOW, KV_DIM))
    n_blocks = MIX_TQ // WINDOW
    rows = GROUP * WINDOW
    for n in range(n_blocks):
        kw = k_all[n * WINDOW:(n + 2) * WINDOW]
        qn = q_ref[n * WINDOW:(n + 1) * WINDOW, :].astype(BF16)
        qs = jnp.concatenate(
            [qn[:, g * KV_DIM:(g + 1) * KV_DIM] for g in range(GROUP)], axis=0)
        for k in range(N_KV_HEADS):
            kk = jnp.where(head == k, kw, 0.0).astype(BF16)
            s_ref[n * rows:(n + 1) * rows, k * 2 * WINDOW:(k + 1) * 2 * WINDOW] = (
                lax.dot_general(qs, kk, (((1,), (1,)), ((), ())),
                                preferred_element_type=F32))

    o_ref[...] = x_ref[...] + jnp.dot(mix_ref[:, :POOL_DIM], wo_ref[:POOL_DIM, :],
                                      preferred_element_type=F32)

    fi = lax.broadcasted_iota(jnp.int32, (ATT_CHUNK, WINDOW), 0)
    fj = lax.broadcasted_iota(jnp.int32, (ATT_CHUNK, WINDOW), 1)
    from_prev = [fj > fi + c * ATT_CHUNK for c in range(WINDOW // ATT_CHUNK)]
    no_prev = [jnp.where(fp & (fj < lo), -jnp.inf, 0.0) for fp in from_prev]
    for n in range(n_blocks):
        for k in range(N_KV_HEADS):
            c0 = k * 2 * WINDOW
            for g in range(GROUP):
                sink = sink_ref[k * GROUP + g]
                for c in range(WINDOW // ATT_CHUNK):
                    r0 = n * rows + g * WINDOW + c * ATT_CHUNK
                    sc = jnp.where(from_prev[c],
                                   s_ref[r0:r0 + ATT_CHUNK, c0:c0 + WINDOW],
                                   s_ref[r0:r0 + ATT_CHUNK, c0 + WINDOW:c0 + 2 * WINDOW])
                    if n == 0:
                        sc = sc + no_prev[c]
                    m = jnp.maximum(jnp.max(sc, axis=-1, keepdims=True), sink)
                    p = jnp.exp(sc - m)
                    den = jnp.sum(p, axis=-1, keepdims=True) + jnp.exp(sink - m)
                    p = p * (1.0 / den)
                    p_ref[r0:r0 + ATT_CHUNK, c0:c0 + WINDOW] = (
                        jnp.where(from_prev[c], p, 0.0).astype(BF16))
                    p_ref[r0:r0 + ATT_CHUNK, c0 + WINDOW:c0 + 2 * WINDOW] = (
                        jnp.where(from_prev[c], 0.0, p).astype(BF16))

    for n in range(n_blocks):
        vw = v_all[n * WINDOW:(n + 2) * WINDOW]
        vcat = jnp.concatenate(
            [jnp.where(head == k, vw, 0.0).astype(BF16)
             for k in range(N_KV_HEADS)], axis=0)
        o = jnp.dot(p_ref[n * rows:(n + 1) * rows, :], vcat, preferred_element_type=F32)
        blk = slice(n * WINDOW, (n + 1) * WINDOW)
        for g in range(GROUP):
            mix_ref[blk, POOL_DIM + g * KV_DIM:POOL_DIM + (g + 1) * KV_DIM] = (
                o[g * WINDOW:(g + 1) * WINDOW].astype(BF16))

    o_ref[...] += jnp.dot(mix_ref[:, POOL_DIM:], wo_ref[POOL_DIM:, :],
                          preferred_element_type=F32)


def _prompt_mix(sinks, proj, x, pool_w, pool_scale, w_out):
    blocks_kv = MIX_TQ // WINDOW
    blocks_u = MIX_TQ // HALO_U
    return pl.pallas_call(
        _prompt_mix_body,
        out_shape=jax.ShapeDtypeStruct((PROMPT_ROWS, D_MODEL), F32),
        grid=(PROMPT_ROWS // MIX_TQ,),
        in_specs=[
            pl.BlockSpec(memory_space=pltpu.SMEM),
            pl.BlockSpec((MIX_TQ, POOL_DIM), lambda i: (i, 0)),
            pl.BlockSpec((MIX_TQ, Q_DIM), lambda i: (i, 1)),
            pl.BlockSpec((MIX_TQ, 2 * KV_DIM), lambda i: (i, 4)),
            pl.BlockSpec((WINDOW, 2 * KV_DIM),
                         lambda i: (jnp.maximum(i * blocks_kv - 1, 0), 4)),
            pl.BlockSpec((HALO_U, POOL_DIM),
                         lambda i: (jnp.maximum(i * blocks_u - 1, 0), 0)),
            pl.BlockSpec((MIX_TQ, D_MODEL), lambda i: (i, 0)),
            pl.BlockSpec((len(POOL_WINDOWS), POOL_GROUP_DIM, POOL_GROUP_DIM),
                         lambda i: (0, 0, 0)),
            pl.BlockSpec((1, POOL_DIM), lambda i: (0, 0)),
            pl.BlockSpec((D_MODEL, D_MODEL), lambda i: (0, 0)),
        ],
        out_specs=pl.BlockSpec((MIX_TQ, D_MODEL), lambda i: (i, 0)),
        scratch_shapes=[pltpu.VMEM((MIX_TQ, D_MODEL), BF16),
                        pltpu.VMEM((GROUP * MIX_TQ, N_KV_HEADS * 2 * WINDOW), F32),
                        pltpu.VMEM((GROUP * MIX_TQ, N_KV_HEADS * 2 * WINDOW), BF16)],
        compiler_params=_params("parallel"),
        name="prompt_mix",
    )(sinks, proj, proj, proj, proj, proj, x, pool_w, pool_scale, w_out)


def _sample_pool_body(hist_ref, ut_ref, newpool_ref):
    for j in range(POOL_HIST - DEC_SEQ):
        newpool_ref[j] = hist_ref[j + DEC_SEQ]
    for t in range(DEC_SEQ):
        newpool_ref[POOL_HIST - DEC_SEQ + t] = ut_ref[t]


def _sample_pool(layer, state_t, u_tokens):
    return pl.pallas_call(
        _sample_pool_body,
        out_shape=jax.ShapeDtypeStruct((POOL_HIST, DEC_BATCH, POOL_DIM), F32),
        grid=(DEC_BATCH // POOL_SB,),
        in_specs=[
            pl.BlockSpec((None, POOL_HIST, POOL_SB, POOL_DIM), lambda i: (layer, 0, i, 0)),
            pl.BlockSpec((DEC_SEQ, POOL_SB, POOL_DIM), lambda i: (0, i, 0)),
        ],
        out_specs=pl.BlockSpec((POOL_HIST, POOL_SB, POOL_DIM), lambda i: (0, i, 0)),
        compiler_params=_params("parallel"),
        name="sample_pool",
    )(state_t, u_tokens)


def _sample_attn_body(*refs, update_cache):
    if update_cache:
        (sink_ref, d_ref, q_ref, kvt_ref, ck_ref, cv_ref, x_ref, pw_ref, ps_ref, wo_ref,
         o_ref, nk_ref, nv_ref, mix_ref, s_ref, p_ref) = refs
    else:
        (sink_ref, d_ref, q_ref, kvt_ref, ck_ref, cv_ref, x_ref, pw_ref, ps_ref, wo_ref,
         o_ref, mix_ref, s_ref, p_ref) = refs

    lane = lax.broadcasted_iota(jnp.int32, (KV_DIM, N_BUF), 1)
    is_old = lane < N_BUF - DEC_SEQ

    def shift_in(b, kt, vt, kt_new, vt_new):
        shift_new = (N_BUF - DEC_SEQ - b * DEC_SEQ) % N_BUF
        nk_ref[b] = jnp.where(is_old, pltpu.roll(kt, N_BUF - DEC_SEQ, axis=1),
                              pltpu.roll(kt_new, shift_new, axis=1))
        nv_ref[b] = jnp.where(is_old, pltpu.roll(vt, N_BUF - DEC_SEQ, axis=1),
                              pltpu.roll(vt_new, shift_new, axis=1))

    def attend():
        _sample_attend(sink_ref, d_ref, q_ref, kvt_ref, ck_ref, cv_ref, x_ref, pw_ref,
                       ps_ref, wo_ref, o_ref, mix_ref, s_ref, p_ref,
                       shift_in if update_cache else None)

    if not update_cache:
        attend()
        return

    phase = pl.program_id(0)

    @pl.when(phase < DEPTH - 1)
    def _():
        kt_new = kvt_ref[:KV_DIM, :]
        vt_new = kvt_ref[KV_DIM:, :]
        for b in range(ATTN_SB):
            shift_in(b, ck_ref[b], cv_ref[b], kt_new, vt_new)

    @pl.when(phase == DEPTH - 1)
    def _():
        attend()


def _sample_attend(sink_ref, d_ref, q_ref, kvt_ref, ck_ref, cv_ref, x_ref, pw_ref, ps_ref,
                   wo_ref, o_ref, mix_ref, s_ref, p_ref, shift_in):
    for gi in range(len(POOL_WINDOWS)):
        lanes = slice(gi * POOL_GROUP_DIM, (gi + 1) * POOL_GROUP_DIM)
        mix_ref[:, lanes] = _pool_linear(d_ref[:, lanes], gi, pw_ref, ps_ref).astype(BF16)

    pair = 2 * DEC_SEQ
    grp = GROUP * pair
    rows = N_KV_HEADS * grp
    new_cols = ATTN_SB * DEC_SEQ
    keys = N_BUF + new_cols
    assert keys == KV_DIM and new_cols == N_BUF
    head = _kv_head_of_lane((rows, KV_DIM))
    row = lax.broadcasted_iota(jnp.int32, (rows, keys), 0)
    row_head = row // grp
    in_pair = row % pair
    low = in_pair < DEC_SEQ
    tok = in_pair % DEC_SEQ
    col = lax.broadcasted_iota(jnp.int32, (rows, keys), 1)
    new_col = col - N_BUF
    valid_cache = (col < N_BUF) & (col > tok)
    causal_new = (new_col >= 0) & ((new_col % DEC_SEQ) <= tok)
    seq_of_col = new_col // DEC_SEQ - in_pair // DEC_SEQ
    sink = jnp.concatenate(
        [jnp.full((pair, 1), sink_ref[k * GROUP + g], F32)
         for k in range(N_KV_HEADS) for g in range(GROUP)], axis=0)
    kt_new = kvt_ref[:KV_DIM, :]
    vt_new = kvt_ref[KV_DIM:, :]
    kt_new_bf = kt_new.astype(BF16)
    vt_new_bf = vt_new.astype(BF16)
    nt = (((1,), (1,)), ((), ()))

    n_pairs = ATTN_SB // 2
    for j in range(n_pairs):
        r0 = j * pair
        q8 = q_ref[r0:r0 + pair, :]
        qs = jnp.concatenate(
            [q8[:, g * KV_DIM:(g + 1) * KV_DIM] for g in range(GROUP)], axis=0)
        qrep = jnp.concatenate([qs] * N_KV_HEADS, axis=0)
        lhs = jnp.where(head == row_head, qrep, 0.0).astype(BF16)
        scores = [jnp.dot(lhs, ck_ref[b].astype(BF16), preferred_element_type=F32)
                  for b in (2 * j, 2 * j + 1)]
        s_new = jnp.dot(lhs, kt_new_bf, preferred_element_type=F32)
        s = jnp.concatenate([jnp.where(low[:, :N_BUF], scores[0], scores[1]), s_new],
                            axis=1)
        valid = valid_cache | (causal_new & (seq_of_col == 2 * j))
        s_ref[j * rows:(j + 1) * rows, :] = jnp.where(valid, s, -jnp.inf)

    for j in range(n_pairs):
        s = s_ref[j * rows:(j + 1) * rows, :]
        m = jnp.maximum(jnp.max(s, axis=-1, keepdims=True), sink)
        p = jnp.exp(s - m)
        den = jnp.sum(p, axis=-1, keepdims=True) + jnp.exp(sink - m)
        p_ref[j * rows:(j + 1) * rows, :] = (p * (1.0 / den)).astype(BF16)

    for j in range(n_pairs):
        r0 = j * pair
        p = p_ref[j * rows:(j + 1) * rows, :]
        p_old = p[:, :N_BUF]
        outs = []
        for b in (2 * j, 2 * j + 1):
            kt = ck_ref[b]
            vt = cv_ref[b]
            outs.append(lax.dot_general(p_old, vt.astype(BF16), nt,
                                        preferred_element_type=F32))
            if shift_in is not None:
                shift_in(b, kt, vt, kt_new, vt_new)
        o = jnp.where(low, outs[0], outs[1])
        o = o + lax.dot_general(p[:, N_BUF:], vt_new_bf, nt, preferred_element_type=F32)
        o = jnp.where(head == row_head, o, 0.0)
        og = o[0:grp]
        for k in range(1, N_KV_HEADS):
            og = og + o[k * grp:(k + 1) * grp]
        for g in range(GROUP):
            mix_ref[r0:r0 + pair,
                    POOL_DIM + g * KV_DIM:POOL_DIM + (g + 1) * KV_DIM] = (
                        og[g * pair:(g + 1) * pair].astype(BF16))

    o_ref[...] = x_ref[...] + jnp.dot(mix_ref[...], wo_ref[...],
                                      preferred_element_type=F32)


def _sample_attn(layer, sinks, d_rows, proj, kvts, cache_kt, cache_vt, x, pool_w,
                 pool_scale, w_out):
    phases = kvts.shape[0]
    update_cache = phases > 1
    score_shape = (ATTN_SB // 2 * N_HEADS * 2 * DEC_SEQ, N_BUF + ATTN_ROWS)
    assert phases == 1 or (phases == DEPTH and layer == DEPTH - 1)
    layer_of = (lambda p: p) if update_cache else (lambda p: layer)
    row = lambda p, i: jnp.where(p == phases - 1, i, 0)
    cache_spec = pl.BlockSpec((None, ATTN_SB, KV_DIM, N_BUF),
                              lambda p, i: (layer_of(p), i, 0, 0))
    out_shape = [jax.ShapeDtypeStruct((SAMPLE_ROWS, D_MODEL), F32)]
    out_specs = [pl.BlockSpec((ATTN_ROWS, D_MODEL), lambda p, i: (row(p, i), 0))]
    if update_cache:
        out_shape += [jax.ShapeDtypeStruct((DEPTH, DEC_BATCH, KV_DIM, N_BUF), F32)] * 2
        out_specs += [pl.BlockSpec((None, ATTN_SB, KV_DIM, N_BUF),
                                   lambda p, i: (p, i, 0, 0))] * 2
    return pl.pallas_call(
        functools.partial(_sample_attn_body, update_cache=update_cache),
        out_shape=out_shape,
        grid=(phases, DEC_BATCH // ATTN_SB),
        in_specs=[
            pl.BlockSpec(memory_space=pltpu.SMEM),
            pl.BlockSpec((ATTN_ROWS, POOL_DIM), lambda p, i: (row(p, i), 0)),
            pl.BlockSpec((ATTN_ROWS, Q_DIM), lambda p, i: (row(p, i), 1)),
            pl.BlockSpec((None, 2 * KV_DIM, ATTN_ROWS), lambda p, i: (p, 0, i)),
            cache_spec,
            cache_spec,
            pl.BlockSpec((ATTN_ROWS, D_MODEL), lambda p, i: (row(p, i), 0)),
            pl.BlockSpec((len(POOL_WINDOWS), POOL_GROUP_DIM, POOL_GROUP_DIM),
                         lambda p, i: (0, 0, 0)),
            pl.BlockSpec((1, POOL_DIM), lambda p, i: (0, 0)),
            pl.BlockSpec((D_MODEL, D_MODEL), lambda p, i: (0, 0),
                         pipeline_mode=pl.Buffered(1)),
        ],
        out_specs=out_specs,
        scratch_shapes=[pltpu.VMEM((ATTN_ROWS, D_MODEL), BF16),
                        pltpu.VMEM(score_shape, F32),
                        pltpu.VMEM(score_shape, BF16)],
        compiler_params=_params("arbitrary", "arbitrary"),
        name="sample_attn",
    )(sinks, d_rows, proj, kvts, cache_kt, cache_vt, x, pool_w, pool_scale, w_out)


def kernel(x_prompt, x_sample, cache_k, cache_v, state_pool, norm_ffn1, ffn1_gate,
           ffn1_up, ffn1_down, norm_mix, w_in, pool_w, pool_scale, attn_sinks,
           w_out, norm_ffn2, ffn2_gate, ffn2_up, ffn2_down, final_norm):
    xp = x_prompt.reshape(PROMPT_ROWS, D_MODEL)
    xs = x_sample.reshape(SAMPLE_ROWS, D_MODEL)
    gf = final_norm.reshape(1, D_MODEL)
    state_t = jnp.transpose(state_pool, (0, 2, 1, 3))

    def to_cache_t(c):
        return jnp.transpose(c, (0, 1, 3, 4, 2)).reshape(DEPTH, DEC_BATCH, KV_DIM, N_BUF)

    def from_cache_t(c):
        c = c.reshape(DEPTH, DEC_BATCH, N_KV_HEADS, HEAD_DIM, N_BUF)
        return jnp.transpose(c, (0, 1, 4, 2, 3))

    cache_kt = to_cache_t(cache_k)
    cache_vt = to_cache_t(cache_v)
    keep = min(WINDOW, SEQ)
    assert keep >= POOL_HIST
    k0 = POOL_DIM + Q_DIM

    ffn1_w = (ffn1_gate, ffn1_up, ffn1_down)
    ffn2_w = (ffn2_gate, ffn2_up, ffn2_down)
    head, *w_bf = _ffn_head(0, xp, norm_ffn1[0].reshape(1, D_MODEL), *ffn1_w)
    kp_l, vp_l, pp_l, ps_l, kvt_l = [], [], [], [], []
    for l in range(DEPTH):
        last = l == DEPTH - 1
        pool_w_l = pool_w[l].astype(BF16)
        pool_scale_l = pool_scale[l].reshape(1, POOL_DIM)
        sinks_l = attn_sinks[l].astype(F32)

        xp, xs, w_bf, (w_in_l, w_out_l) = _ffn(
            xp, xs, norm_ffn1[l].reshape(1, D_MODEL), *w_bf, gf,
            next_w=(l,) + ffn2_w, mix_w=(l, w_in, w_out),
            head=head if l == 0 else None)
        proj_p, proj_s, kvt, d_rows, u_tokens = _inproj(
            l, xp, xs, norm_mix[l].reshape(1, D_MODEL), w_in_l, state_t)

        tails = [proj_p[(b + 1) * SEQ - keep:(b + 1) * SEQ] for b in range(BATCH)]
        kp_l.append(jnp.stack([t[:, k0:k0 + KV_DIM] for t in tails])
                    .reshape(BATCH, keep, N_KV_HEADS, HEAD_DIM))
        vp_l.append(jnp.stack([t[:, k0 + KV_DIM:] for t in tails])
                    .reshape(BATCH, keep, N_KV_HEADS, HEAD_DIM))
        pp_l.append(jnp.stack([t[keep - POOL_HIST:, :POOL_DIM] for t in tails]))

        ps_l.append(_sample_pool(l, state_t, u_tokens))

        xp = _prompt_mix(sinks_l, proj_p, xp, pool_w_l, pool_scale_l, w_out_l)
        kvt_l.append(kvt)
        outs = _sample_attn(
            l, sinks_l, d_rows, proj_s,
            jnp.stack(kvt_l) if last else kvt[None],
            cache_kt, cache_vt, xs, pool_w_l, pool_scale_l, w_out_l)
        xs = outs[0]
        if last:
            new_kt, new_vt = outs[1:]

        xp, xs, w_bf, _ = _ffn(xp, xs, norm_ffn2[l].reshape(1, D_MODEL), *w_bf, gf,
                               next_w=None if last else (l + 1,) + ffn1_w,
                               final_norm=last)

    y_prompt = xp.reshape(BATCH, SEQ, D_MODEL)
    y_sample = xs.reshape(DEC_BATCH, DEC_SEQ, D_MODEL)
    new_pool_sample = jnp.transpose(jnp.stack(ps_l), (0, 2, 1, 3))
    return (y_prompt, y_sample, jnp.stack(kp_l), jnp.stack(vp_l), jnp.stack(pp_l),
            from_cache_t(new_kt), from_cache_t(new_vt), new_pool_sample)
```

```python
import functools

import jax
import jax.numpy as jnp
from jax import lax
from jax.experimental import pallas as pl
from jax.experimental.pallas import tpu as pltpu

F32 = jnp.float32
BF16 = jnp.bfloat16

D_MODEL = 2048
BATCH = 2
SEQ = 4096
DEPTH = 2
DEC_BATCH = 128
DEC_SEQ = 4
PAST_LEN = 8192

POOL_DIM = 1024
POOL_WINDOWS = (2, 4, 8, 16)
POOL_GROUP_DIM = POOL_DIM // len(POOL_WINDOWS)
POOL_HIST = max(POOL_WINDOWS) - 1
HEAD_DIM = 64
N_HEADS = 16
N_KV_HEADS = 4
GROUP = N_HEADS // N_KV_HEADS
Q_DIM = N_HEADS * HEAD_DIM
KV_DIM = N_KV_HEADS * HEAD_DIM
IN_DIM = POOL_DIM + Q_DIM + 2 * KV_DIM
WINDOW = 128
FFN_DIM = 5632
RMS_EPS = 1e-5
ATTN_SCALE = HEAD_DIM ** -0.5
N_BUF = min(WINDOW, PAST_LEN)
N_HIST_SAMPLE = min(POOL_HIST, PAST_LEN)

PROMPT_ROWS = BATCH * SEQ
SAMPLE_ROWS = DEC_BATCH * DEC_SEQ
M_ROWS = PROMPT_ROWS + SAMPLE_ROWS

VMEM_LIMIT_BYTES = 56 * 1024 * 1024
LANES = 128

FFN_TM = 1088
FFN_TF = 512
FFN_TILES = M_ROWS // FFN_TM
FFN_CHUNKS = FFN_DIM // FFN_TF
FFN_FULL_TILES = PROMPT_ROWS // FFN_TM
FFN_SPLIT = PROMPT_ROWS - FFN_FULL_TILES * FFN_TM
FFN_TF_HEAD = 256
WIN_ROWS = 32
WIN_BLOCKS = D_MODEL // WIN_ROWS
WOUT_BLOCKS = D_MODEL // HEAD_DIM
PROJ_TM = 512
PROJ_PROMPT_TILES = PROMPT_ROWS // PROJ_TM
MIX_TQ = 512
HALO_U = 16
ATT_CHUNK = 64
POOL_SB = 32
ATTN_SB = 32
ATTN_ROWS = ATTN_SB * DEC_SEQ

assert FFN_TILES * FFN_TM == M_ROWS and FFN_CHUNKS * FFN_TF == FFN_DIM
assert FFN_FULL_TILES + 1 == FFN_TILES and FFN_SPLIT + SAMPLE_ROWS == FFN_TM
assert FFN_SPLIT % 8 == 0 and FFN_FULL_TILES >= 1
assert FFN_CHUNKS >= 2
assert FFN_DIM % FFN_TF_HEAD == 0
assert SAMPLE_ROWS == PROJ_TM and PROJ_PROMPT_TILES * PROJ_TM == PROMPT_ROWS


def _rms(x, g):
    return x * lax.rsqrt(jnp.mean(x * x, axis=-1, keepdims=True) + RMS_EPS) * g


def _params(*sem):
    return pltpu.CompilerParams(dimension_semantics=sem,
                                vmem_limit_bytes=VMEM_LIMIT_BYTES)


def _ffn_body(*refs, final_norm, cast_next, prep_mix, t0, n_tiles, head_in):
    xp_hbm, xs_hbm, g_ref, wg_ref, wu_ref, wd_ref, gf_ref = refs[:7]
    refs = refs[7:]
    if cast_next:
        ng_ref, nu_ref, nd_ref = refs[:3]
        refs = refs[3:]
    if prep_mix:
        win_ref, wout_ref = refs[:2]
        refs = refs[2:]
    if head_in:
        head_hbm = refs[0]
        refs = refs[1:]
    op_hbm, os_hbm = refs[:2]
    refs = refs[2:]
    if cast_next:
        cg_ref, cu_ref, cd_ref = refs[:3]
        refs = refs[3:]
    if prep_mix:
        cwin_ref, cwout_ref = refs[:2]
        refs = refs[2:]
    acc_ref, xn_ref, in_sem, out_sem = refs[:4]
    head_sem = refs[4] if head_in else None

    i = pl.program_id(0)
    f = pl.program_id(1)
    slot = i % 2
    tile = i + t0

    def head_copy():
        return pltpu.make_async_copy(head_hbm, op_hbm.at[pl.ds(0, t0 * FFN_TM)], head_sem)

    def copies(p_hbm, s_hbm, tile, s, sem, mixed, to_vmem):
        if mixed:
            pairs = [(p_hbm.at[pl.ds(FFN_FULL_TILES * FFN_TM, FFN_SPLIT)],
                      acc_ref.at[s, pl.ds(0, FFN_SPLIT)]),
                     (s_hbm, acc_ref.at[s, pl.ds(FFN_SPLIT, SAMPLE_ROWS)])]
        else:
            pairs = [(p_hbm.at[pl.ds(tile * FFN_TM, FFN_TM)], acc_ref.at[s])]
        return [pltpu.make_async_copy(h, v, sem.at[s]) if to_vmem
                else pltpu.make_async_copy(v, h, sem.at[s]) for h, v in pairs]

    def on_tile(tile, s, op, to_vmem):
        p_hbm, s_hbm, sem = ((xp_hbm, xs_hbm, in_sem) if to_vmem
                             else (op_hbm, os_hbm, out_sem))

        @pl.when(tile < FFN_FULL_TILES)
        def _():
            for c in copies(p_hbm, s_hbm, tile, s, sem, False, to_vmem):
                op(c)

        @pl.when(tile == FFN_FULL_TILES)
        def _():
            for c in copies(p_hbm, s_hbm, tile, s, sem, True, to_vmem):
                op(c)

    start = lambda c: c.start()
    wait = lambda c: c.wait()

    @pl.when((i == 0) & (f == 0))
    def _():
        on_tile(tile, slot, start, True)
        if head_in:
            head_copy().start()

    @pl.when(f == 0)
    def _():
        on_tile(tile, slot, wait, True)
        xn_ref[...] = _rms(acc_ref[slot], g_ref[...]).astype(BF16)

    a = xn_ref[...]
    gate = jnp.dot(a, wg_ref[...], preferred_element_type=F32)
    if cast_next:
        cg_ref[...] = ng_ref[...].astype(BF16)
        cu_ref[...] = nu_ref[...].astype(BF16)
    up = jnp.dot(a, wu_ref[...], preferred_element_type=F32)
    if cast_next:
        cd_ref[...] = nd_ref[...].astype(BF16)
    if prep_mix:
        _prep_w_in(win_ref, cwin_ref)
        cwout_ref[...] = wout_ref[...].astype(BF16)
    h = (gate * jax.nn.sigmoid(gate) * up).astype(BF16)
    acc_ref[slot] += 0.5 * jnp.dot(h, wd_ref[...], preferred_element_type=F32)

    @pl.when(f == 1)
    def _():
        @pl.when(i >= 1)
        def _():
            on_tile(tile - 1, 1 - slot, wait, False)

        @pl.when(i + 1 < n_tiles)
        def _():
            on_tile(tile + 1, 1 - slot, start, True)

    @pl.when(f == FFN_CHUNKS - 1)
    def _():
        if final_norm:
            acc_ref[slot] = _rms(acc_ref[slot], gf_ref[...])
        on_tile(tile, slot, start, False)

        @pl.when(i == n_tiles - 1)
        def _():
            on_tile(tile, slot, wait, False)
            if head_in:
                head_copy().wait()


def _prep_w_in(w_ref, o_ref):
    k0 = POOL_DIM + Q_DIM
    o_ref[:, :POOL_DIM] = w_ref[:, :POOL_DIM].astype(BF16)
    o_ref[:, k0:] = w_ref[:, k0:].astype(BF16)
    lane = lax.broadcasted_iota(jnp.int32, (WIN_ROWS, 2 * HEAD_DIM), 1)
    for g in range(GROUP):
        for kp in range(N_KV_HEADS // 2):
            src = [POOL_DIM + (2 * kp + j) * KV_DIM + (g // 2) * 2 * HEAD_DIM
                   for j in (0, 1)]
            lo, hi = [w_ref[:, c:c + 2 * HEAD_DIM] for c in src]
            if g % 2 == 0:
                hi = pltpu.roll(hi, HEAD_DIM, axis=1)
            else:
                lo = pltpu.roll(lo, HEAD_DIM, axis=1)
            dst = POOL_DIM + g * KV_DIM + kp * 2 * HEAD_DIM
            o_ref[:, dst:dst + 2 * HEAD_DIM] = (
                jnp.where(lane < HEAD_DIM, lo, hi) * ATTN_SCALE).astype(BF16)


def _w_out_src_block(out_blk):
    pool_blocks = POOL_DIM // HEAD_DIM
    h = out_blk - pool_blocks
    src = pool_blocks + (h % N_KV_HEADS) * GROUP + h // N_KV_HEADS
    return jnp.where(out_blk < pool_blocks, out_blk, src)


def _ffn(xp, xs, g, wg, wu, wd, gf, next_w=None, mix_w=None, *, final_norm=False,
         head=None):
    cast_next = next_w is not None
    prep_mix = mix_w is not None
    head_in = head is not None
    t0 = head.shape[0] // FFN_TM if head_in else 0
    n_tiles = FFN_TILES - t0
    cast_blocks = 1 << (n_tiles.bit_length() - 1)
    cast_rows = D_MODEL // cast_blocks
    win_steps = pl.cdiv(WIN_BLOCKS, n_tiles)
    wout_steps = pl.cdiv(WOUT_BLOCKS, n_tiles)
    assert max(win_steps, wout_steps) <= FFN_CHUNKS
    cast_blk = lambda i: jnp.minimum(i, cast_blocks - 1)
    cast_chunk = lambda i, f: jnp.where(i < cast_blocks, f, FFN_CHUNKS - 1)
    vec = pl.BlockSpec((1, D_MODEL), lambda i, f: (0, 0))
    in_specs = [
        pl.BlockSpec(memory_space=pl.ANY),
        pl.BlockSpec(memory_space=pl.ANY),
        vec,
        pl.BlockSpec((D_MODEL, FFN_TF), lambda i, f: (0, f)),
        pl.BlockSpec((D_MODEL, FFN_TF), lambda i, f: (0, f)),
        pl.BlockSpec((FFN_TF, D_MODEL), lambda i, f: (f, 0)),
        vec,
    ]
    out_shape = [jax.ShapeDtypeStruct((PROMPT_ROWS, D_MODEL), F32),
                 jax.ShapeDtypeStruct((SAMPLE_ROWS, D_MODEL), F32)]
    out_specs = [pl.BlockSpec(memory_space=pl.ANY), pl.BlockSpec(memory_space=pl.ANY)]
    args = [xp, xs, g, wg, wu, wd, gf]
    if cast_next:
        nl = next_w[0]
        in_specs += [
            pl.BlockSpec((None, cast_rows, FFN_TF),
                         lambda i, f: (nl, cast_blk(i), cast_chunk(i, f))),
            pl.BlockSpec((None, cast_rows, FFN_TF),
                         lambda i, f: (nl, cast_blk(i), cast_chunk(i, f))),
            pl.BlockSpec((None, FFN_TF, cast_rows),
                         lambda i, f: (nl, cast_chunk(i, f), cast_blk(i))),
        ]
        out_shape += [jax.ShapeDtypeStruct((D_MODEL, FFN_DIM), BF16),
                      jax.ShapeDtypeStruct((D_MODEL, FFN_DIM), BF16),
                      jax.ShapeDtypeStruct((FFN_DIM, D_MODEL), BF16)]
        out_specs += [
            pl.BlockSpec((cast_rows, FFN_TF), lambda i, f: (cast_blk(i), cast_chunk(i, f))),
            pl.BlockSpec((cast_rows, FFN_TF), lambda i, f: (cast_blk(i), cast_chunk(i, f))),
            pl.BlockSpec((FFN_TF, cast_rows), lambda i, f: (cast_chunk(i, f), cast_blk(i))),
        ]
        args += list(next_w[1:])
    if prep_mix:
        ml = mix_w[0]
        win_blk = lambda i, f: jnp.minimum(
            i * win_steps + jnp.minimum(f, win_steps - 1), WIN_BLOCKS - 1)
        wout_blk = lambda i, f: jnp.minimum(
            i * wout_steps + jnp.minimum(f, wout_steps - 1), WOUT_BLOCKS - 1)
        in_specs += [
            pl.BlockSpec((None, WIN_ROWS, IN_DIM), lambda i, f: (ml, win_blk(i, f), 0)),
            pl.BlockSpec((None, HEAD_DIM, D_MODEL),
                         lambda i, f: (ml, _w_out_src_block(wout_blk(i, f)), 0)),
        ]
        out_shape += [jax.ShapeDtypeStruct((D_MODEL, IN_DIM), BF16),
                      jax.ShapeDtypeStruct((D_MODEL, D_MODEL), BF16)]
        out_specs += [
            pl.BlockSpec((WIN_ROWS, IN_DIM), lambda i, f: (win_blk(i, f), 0)),
            pl.BlockSpec((HEAD_DIM, D_MODEL), lambda i, f: (wout_blk(i, f), 0)),
        ]
        args += list(mix_w[1:])
    scratch = [pltpu.VMEM((2, FFN_TM, D_MODEL), F32),
               pltpu.VMEM((FFN_TM, D_MODEL), BF16),
               pltpu.SemaphoreType.DMA((2,)),
               pltpu.SemaphoreType.DMA((2,))]
    if head_in:
        in_specs.append(pl.BlockSpec(memory_space=pl.ANY))
        args.append(head)
        scratch.append(pltpu.SemaphoreType.DMA(()))
    outs = pl.pallas_call(
        functools.partial(_ffn_body, final_norm=final_norm, cast_next=cast_next,
                          prep_mix=prep_mix, t0=t0, n_tiles=n_tiles, head_in=head_in),
        out_shape=out_shape,
        grid=(n_tiles, FFN_CHUNKS),
        in_specs=in_specs,
        out_specs=out_specs,
        scratch_shapes=scratch,
        compiler_params=_params("arbitrary", "arbitrary"),
        name="ffn_final" if final_norm else "ffn",
    )(*args)
    n_cast = 3 if cast_next else 0
    return outs[0], outs[1], tuple(outs[2:2 + n_cast]), tuple(outs[2 + n_cast:])


def _ffn_head_body(x_ref, g_ref, wg_ref, wu_ref, wd_ref, o_ref, og_ref, ou_ref, od_ref,
                   xn_ref, wgs_ref, wus_ref, wds_ref):
    step = pl.program_id(0)
    slot = step % 2

    @pl.when(step == 0)
    def _():
        x = x_ref[...]
        xn_ref[...] = _rms(x, g_ref[...]).astype(BF16)
        o_ref[...] = x
        wgs_ref[1] = jnp.zeros(wgs_ref.shape[1:], BF16)
        wus_ref[1] = jnp.zeros(wus_ref.shape[1:], BF16)
        wds_ref[1] = jnp.zeros(wds_ref.shape[1:], BF16)

    a = xn_ref[...]
    wg = wg_ref[...].astype(BF16)
    wgs_ref[slot] = wg
    og_ref[...] = wg
    gate = jnp.dot(a, wgs_ref[1 - slot], preferred_element_type=F32)
    wu = wu_ref[...].astype(BF16)
    wus_ref[slot] = wu
    ou_ref[...] = wu
    up = jnp.dot(a, wus_ref[1 - slot], preferred_element_type=F32)
    wd = wd_ref[...].astype(BF16)
    wds_ref[slot] = wd
    od_ref[...] = wd
    h = (gate * jax.nn.sigmoid(gate) * up).astype(BF16)
    o_ref[...] += 0.5 * jnp.dot(h, wds_ref[1 - slot], preferred_element_type=F32)


def _ffn_head(layer, xp, g, gate, up, down):
    tf = FFN_TF_HEAD
    n_chunks = FFN_DIM // tf
    fixed = lambda s: (0, 0)
    chunk = lambda s: jnp.minimum(s, n_chunks - 1)
    return pl.pallas_call(
        _ffn_head_body,
        out_shape=(jax.ShapeDtypeStruct((FFN_TM, D_MODEL), F32),
                   jax.ShapeDtypeStruct((D_MODEL, FFN_DIM), BF16),
                   jax.ShapeDtypeStruct((D_MODEL, FFN_DIM), BF16),
                   jax.ShapeDtypeStruct((FFN_DIM, D_MODEL), BF16)),
        grid=(n_chunks + 1,),
        in_specs=[
            pl.BlockSpec((FFN_TM, D_MODEL), fixed, pipeline_mode=pl.Buffered(1)),
            pl.BlockSpec((1, D_MODEL), fixed),
            pl.BlockSpec((None, D_MODEL, tf), lambda s: (layer, 0, chunk(s))),
            pl.BlockSpec((None, D_MODEL, tf), lambda s: (layer, 0, chunk(s))),
            pl.BlockSpec((None, tf, D_MODEL), lambda s: (layer, chunk(s), 0)),
        ],
        out_specs=(pl.BlockSpec((FFN_TM, D_MODEL), fixed),
                   pl.BlockSpec((D_MODEL, tf), lambda s: (0, chunk(s))),
                   pl.BlockSpec((D_MODEL, tf), lambda s: (0, chunk(s))),
                   pl.BlockSpec((tf, D_MODEL), lambda s: (chunk(s), 0))),
        scratch_shapes=[pltpu.VMEM((FFN_TM, D_MODEL), BF16),
                        pltpu.VMEM((2, D_MODEL, tf), BF16),
                        pltpu.VMEM((2, D_MODEL, tf), BF16),
                        pltpu.VMEM((2, tf, D_MODEL), BF16)],
        compiler_params=_params("arbitrary"),
        name="ffn_head",
    )(xp, g, gate, up, down)


def _inproj_body(xp_ref, xs_ref, g_ref, w_ref, hist_ref, pp_ref, ps_ref, kvt_ref, d_ref,
                 ut_ref, u_scr, d_scr):
    i = pl.program_id(0)

    @pl.when(i < PROJ_PROMPT_TILES)
    def _():
        xn = _rms(xp_ref[...], g_ref[...]).astype(BF16)
        pp_ref[...] = jnp.dot(xn, w_ref[...], preferred_element_type=F32)

    @pl.when(i == PROJ_PROMPT_TILES)
    def _():
        xn = _rms(xs_ref[...], g_ref[...]).astype(BF16)
        proj = jnp.dot(xn, w_ref[...], preferred_element_type=F32)
        ps_ref[...] = proj
        kvt_ref[...] = proj[:, POOL_DIM + Q_DIM:].T
        token_rows = [pl.ds(t, DEC_BATCH, stride=DEC_SEQ) for t in range(DEC_SEQ)]
        for c in range(POOL_DIM // LANES):
            lanes = slice(c * LANES, (c + 1) * LANES)
            w = POOL_WINDOWS[c * LANES // POOL_GROUP_DIM]
            u_scr[c] = proj[:, lanes]
            seq = [hist_ref[j, :, lanes] for j in range(POOL_HIST)]
            seq += [u_scr[c, rows, :] for rows in token_rows]
            for t in range(DEC_SEQ):
                last = POOL_HIST + t
                s = seq[last - w + 1]
                for j in range(last - w + 2, last + 1):
                    s = s + seq[j]
                cnt = float(min(w, N_HIST_SAMPLE + t + 1))
                d_scr[c, token_rows[t], :] = s / cnt - seq[last]
                ut_ref[t, :, lanes] = seq[last]
            d_ref[:, lanes] = d_scr[c]


def _inproj(layer, xp, xs, g, w, state_t):
    prompt_tile = lambda i: (jnp.minimum(i, PROJ_PROMPT_TILES - 1), 0)
    fixed = lambda i: (0, 0)
    return pl.pallas_call(
        _inproj_body,
        out_shape=(jax.ShapeDtypeStruct((PROMPT_ROWS, IN_DIM), F32),
                   jax.ShapeDtypeStruct((SAMPLE_ROWS, IN_DIM), F32),
                   jax.ShapeDtypeStruct((2 * KV_DIM, SAMPLE_ROWS), F32),
                   jax.ShapeDtypeStruct((SAMPLE_ROWS, POOL_DIM), F32),
                   jax.ShapeDtypeStruct((DEC_SEQ, DEC_BATCH, POOL_DIM), F32)),
        grid=(PROJ_PROMPT_TILES + 1,),
        in_specs=[
            pl.BlockSpec((PROJ_TM, D_MODEL), prompt_tile),
            pl.BlockSpec((PROJ_TM, D_MODEL), fixed),
            pl.BlockSpec((1, D_MODEL), fixed),
            pl.BlockSpec((D_MODEL, IN_DIM), fixed),
            pl.BlockSpec((None, POOL_HIST, DEC_BATCH, POOL_DIM),
                         lambda i: (layer, 0, 0, 0)),
        ],
        out_specs=(pl.BlockSpec((PROJ_TM, IN_DIM), prompt_tile),
                   pl.BlockSpec((PROJ_TM, IN_DIM), fixed),
                   pl.BlockSpec((2 * KV_DIM, SAMPLE_ROWS), fixed),
                   pl.BlockSpec((SAMPLE_ROWS, POOL_DIM), fixed),
                   pl.BlockSpec((DEC_SEQ, DEC_BATCH, POOL_DIM), lambda i: (0, 0, 0))),
        scratch_shapes=[pltpu.VMEM((POOL_DIM // LANES, SAMPLE_ROWS, LANES), F32),
                        pltpu.VMEM((POOL_DIM // LANES, SAMPLE_ROWS, LANES), F32)],
        compiler_params=_params("arbitrary"),
        name="inproj",
    )(xp, xs, g, w, state_t)


def _kv_head_of_lane(shape):
    return lax.broadcasted_iota(jnp.int32, shape, 1) // HEAD_DIM


def _pool_linear(d, gi, pw_ref, ps_ref):
    lanes = slice(gi * POOL_GROUP_DIM, (gi + 1) * POOL_GROUP_DIM)
    y = jnp.dot(d.astype(BF16), pw_ref[gi], preferred_element_type=F32)
    return y * ps_ref[:, lanes]


def _prompt_mix_body(sink_ref, u_ref, q_ref, kv_ref, kvh_ref, uh_ref, x_ref,
                     pw_ref, ps_ref, wo_ref, o_ref, mix_ref, s_ref, p_ref):
    tiles_per_seq = SEQ // MIX_TQ
    tile = pl.program_id(0) % tiles_per_seq
    first = tile == 0

    lo = jnp.where(first, WINDOW, 0)
    keep_u = jnp.full((HALO_U, POOL_DIM), lo, jnp.int32) == 0
    keep_kv = jnp.full((WINDOW, 2 * KV_DIM), lo, jnp.int32) == 0

    def pool_mixer():
        uh = jnp.where(keep_u, uh_ref[...], 0.0)
        pos = tile * MIX_TQ + lax.broadcasted_iota(jnp.int32, (MIX_TQ, 1), 0)
        for gi, w in enumerate(POOL_WINDOWS):
            lanes = slice(gi * POOL_GROUP_DIM, (gi + 1) * POOL_GROUP_DIM)
            xg = jnp.concatenate([uh[:, lanes], u_ref[:, lanes]], axis=0)
            s = xg
            sh = 1
            while sh < w:
                s = s + pltpu.roll(s, sh, axis=0)
                sh *= 2
            cnt = jnp.minimum(w, pos + 1).astype(F32)
            d = s[HALO_U:] / cnt - xg[HALO_U:]
            mix_ref[:, lanes] = _pool_linear(d, gi, pw_ref, ps_ref).astype(BF16)

    pool_mixer()

    kvh = jnp.where(keep_kv, kvh_ref[...], 0.0)
    k_all = jnp.concatenate([kvh[:, :KV_DIM], kv_ref[:, :KV_DIM]], axis=0)
    v_all = jnp.concatenate([kvh[:, KV_DIM:], kv_ref[:, KV_DIM:]], axis=0)
    head = _kv_head_of_lane((2 * WINDOW, KV_DIM))
    n_blocks = MIX_TQ // WINDOW
    rows = GROUP * WINDOW
    for n in range(n_blocks):
        kw = k_all[n * WINDOW:(n + 2) * WINDOW]
        qn = q_ref[n * WINDOW:(n + 1) * WINDOW, :].astype(BF16)
        qs = jnp.concatenate(
            [qn[:, g * KV_DIM:(g + 1) * KV_DIM] for g in range(GROUP)], axis=0)
        for k in range(N_KV_HEADS):
            kk = jnp.where(head == k, kw, 0.0).astype(BF16)
            s_ref[n * rows:(n + 1) * rows, k * 2 * WINDOW:(k + 1) * 2 * WINDOW] = (
                lax.dot_general(qs, kk, (((1,), (1,)), ((), ())),
                                preferred_element_type=F32))

    o_ref[...] = x_ref[...] + jnp.dot(mix_ref[:, :POOL_DIM], wo_ref[:POOL_DIM, :],
                                      preferred_element_type=F32)

    fi = lax.broadcasted_iota(jnp.int32, (ATT_CHUNK, WINDOW), 0)
    fj = lax.broadcasted_iota(jnp.int32, (ATT_CHUNK, WINDOW), 1)
    from_prev = [fj > fi + c * ATT_CHUNK for c in range(WINDOW // ATT_CHUNK)]
    no_prev = [jnp.where(fp & (fj < lo), -jnp.inf, 0.0) for fp in from_prev]
    for n in range(n_blocks):
        for k in range(N_KV_HEADS):
            c0 = k * 2 * WINDOW
            for g in range(GROUP):
                sink = sink_ref[k * GROUP + g]
                for c in range(WINDOW // ATT_CHUNK):
                    r0 = n * rows + g * WINDOW + c * ATT_CHUNK
                    sc = jnp.where(from_prev[c],
                                   s_ref[r0:r0 + ATT_CHUNK, c0:c0 + WINDOW],
                                   s_ref[r0:r0 + ATT_CHUNK, c0 + WINDOW:c0 + 2 * WINDOW])
                    if n == 0:
                        sc = sc + no_prev[c]
                    m = jnp.maximum(jnp.max(sc, axis=-1, keepdims=True), sink)
                    p = jnp.exp(sc - m)
                    den = jnp.sum(p, axis=-1, keepdims=True) + jnp.exp(sink - m)
                    p = p * (1.0 / den)
                    p_ref[r0:r0 + ATT_CHUNK, c0:c0 + WINDOW] = (
                        jnp.where(from_prev[c], p, 0.0).astype(BF16))
                    p_ref[r0:r0 + ATT_CHUNK, c0 + WINDOW:c0 + 2 * WINDOW] = (
                        jnp.where(from_prev[c], 0.0, p).astype(BF16))

    for n in range(n_blocks):
        vw = v_all[n * WINDOW:(n + 2) * WINDOW]
        vcat = jnp.concatenate(
            [jnp.where(head == k, vw, 0.0).astype(BF16)
             for k in range(N_KV_HEADS)], axis=0)
        o = jnp.dot(p_ref[n * rows:(n + 1) * rows, :], vcat, preferred_element_type=F32)
        blk = slice(n * WINDOW, (n + 1) * WINDOW)
        for g in range(GROUP):
            mix_ref[blk, POOL_DIM + g * KV_DIM:POOL_DIM + (g + 1) * KV_DIM] = (
                o[g * WINDOW:(g + 1) * WINDOW].astype(BF16))

    o_ref[...] += jnp.dot(mix_ref[:, POOL_DIM:], wo_ref[POOL_DIM:, :],
                          preferred_element_type=F32)


def _prompt_mix(sinks, proj, x, pool_w, pool_scale, w_out):
    blocks_kv = MIX_TQ // WINDOW
    blocks_u = MIX_TQ // HALO_U
    return pl.pallas_call(
        _prompt_mix_body,
        out_shape=jax.ShapeDtypeStruct((PROMPT_ROWS, D_MODEL), F32),
        grid=(PROMPT_ROWS // MIX_TQ,),
        in_specs=[
            pl.BlockSpec(memory_space=pltpu.SMEM),
            pl.BlockSpec((MIX_TQ, POOL_DIM), lambda i: (i, 0)),
            pl.BlockSpec((MIX_TQ, Q_DIM), lambda i: (i, 1)),
            pl.BlockSpec((MIX_TQ, 2 * KV_DIM), lambda i: (i, 4)),
            pl.BlockSpec((WINDOW, 2 * KV_DIM),
                         lambda i: (jnp.maximum(i * blocks_kv - 1, 0), 4)),
            pl.BlockSpec((HALO_U, POOL_DIM),
                         lambda i: (jnp.maximum(i * blocks_u - 1, 0), 0)),
            pl.BlockSpec((MIX_TQ, D_MODEL), lambda i: (i, 0)),
            pl.BlockSpec((len(POOL_WINDOWS), POOL_GROUP_DIM, POOL_GROUP_DIM),
                         lambda i: (0, 0, 0)),
            pl.BlockSpec((1, POOL_DIM), lambda i: (0, 0)),
            pl.BlockSpec((D_MODEL, D_MODEL), lambda i: (0, 0)),
        ],
        out_specs=pl.BlockSpec((MIX_TQ, D_MODEL), lambda i: (i, 0)),
        scratch_shapes=[pltpu.VMEM((MIX_TQ, D_MODEL), BF16),
                        pltpu.VMEM((GROUP * MIX_TQ, N_KV_HEADS * 2 * WINDOW), F32),
                        pltpu.VMEM((GROUP * MIX_TQ, N_KV_HEADS * 2 * WINDOW), BF16)],
        compiler_params=_params("parallel"),
        name="prompt_mix",
    )(sinks, proj, proj, proj, proj, proj, x, pool_w, pool_scale, w_out)


def _sample_pool_body(hist_ref, ut_ref, newpool_ref):
    for j in range(POOL_HIST - DEC_SEQ):
        newpool_ref[j] = hist_ref[j + DEC_SEQ]
    for t in range(DEC_SEQ):
        newpool_ref[POOL_HIST - DEC_SEQ + t] = ut_ref[t]


def _sample_pool(state_t, u_tokens):
    block = lambda rows: pl.BlockSpec((None, rows, POOL_SB, POOL_DIM),
                                      lambda l, i: (l, 0, i, 0))
    return pl.pallas_call(
        _sample_pool_body,
        out_shape=jax.ShapeDtypeStruct((DEPTH, POOL_HIST, DEC_BATCH, POOL_DIM), F32),
        grid=(DEPTH, DEC_BATCH // POOL_SB),
        in_specs=[block(POOL_HIST), block(DEC_SEQ)],
        out_specs=block(POOL_HIST),
        compiler_params=_params("parallel", "parallel"),
        name="sample_pool",
    )(state_t, u_tokens)


def _sample_attn_body(*refs, update_cache):
    if update_cache:
        (sink_ref, d_ref, q_ref, kvt_ref, ck_ref, cv_ref, x_ref, pw_ref, ps_ref, wo_ref,
         o_ref, nk_ref, nv_ref, mix_ref, s_ref, p_ref) = refs
    else:
        (sink_ref, d_ref, q_ref, kvt_ref, ck_ref, cv_ref, x_ref, pw_ref, ps_ref, wo_ref,
         o_ref, mix_ref, s_ref, p_ref) = refs

    lane = lax.broadcasted_iota(jnp.int32, (KV_DIM, N_BUF), 1)
    is_old = lane < N_BUF - DEC_SEQ

    def shift_in(b, kt, vt, kt_new, vt_new):
        shift_new = (N_BUF - DEC_SEQ - b * DEC_SEQ) % N_BUF
        nk_ref[b] = jnp.where(is_old, pltpu.roll(kt, N_BUF - DEC_SEQ, axis=1),
                              pltpu.roll(kt_new, shift_new, axis=1))
        nv_ref[b] = jnp.where(is_old, pltpu.roll(vt, N_BUF - DEC_SEQ, axis=1),
                              pltpu.roll(vt_new, shift_new, axis=1))

    def attend():
        _sample_attend(sink_ref, d_ref, q_ref, kvt_ref, ck_ref, cv_ref, x_ref, pw_ref,
                       ps_ref, wo_ref, o_ref, mix_ref, s_ref, p_ref,
                       shift_in if update_cache else None)

    if not update_cache:
        attend()
        return

    phase = pl.program_id(0)

    @pl.when(phase < DEPTH - 1)
    def _():
        kt_new = kvt_ref[:KV_DIM, :]
        vt_new = kvt_ref[KV_DIM:, :]
        for b in range(ATTN_SB):
            shift_in(b, ck_ref[b], cv_ref[b], kt_new, vt_new)

    @pl.when(phase == DEPTH - 1)
    def _():
        attend()


def _sample_attend(sink_ref, d_ref, q_ref, kvt_ref, ck_ref, cv_ref, x_ref, pw_ref, ps_ref,
                   wo_ref, o_ref, mix_ref, s_ref, p_ref, shift_in):
    for gi in range(len(POOL_WINDOWS)):
        lanes = slice(gi * POOL_GROUP_DIM, (gi + 1) * POOL_GROUP_DIM)
        mix_ref[:, lanes] = _pool_linear(d_ref[:, lanes], gi, pw_ref, ps_ref).astype(BF16)

    pair = 2 * DEC_SEQ
    grp = GROUP * pair
    rows = N_KV_HEADS * grp
    new_cols = ATTN_SB * DEC_SEQ
    keys = N_BUF + new_cols
    assert keys == KV_DIM and new_cols == N_BUF
    head = _kv_head_of_lane((rows, KV_DIM))
    row = lax.broadcasted_iota(jnp.int32, (rows, keys), 0)
    row_head = row // grp
    in_pair = row % pair
    low = in_pair < DEC_SEQ
    tok = in_pair % DEC_SEQ
    col = lax.broadcasted_iota(jnp.int32, (rows, keys), 1)
    new_col = col - N_BUF
    valid_cache = (col < N_BUF) & (col > tok)
    causal_new = (new_col >= 0) & ((new_col % DEC_SEQ) <= tok)
    seq_of_col = new_col // DEC_SEQ - in_pair // DEC_SEQ
    sink = jnp.concatenate(
        [jnp.full((pair, 1), sink_ref[k * GROUP + g], F32)
         for k in range(N_KV_HEADS) for g in range(GROUP)], axis=0)
    kt_new = kvt_ref[:KV_DIM, :]
    vt_new = kvt_ref[KV_DIM:, :]
    kt_new_bf = kt_new.astype(BF16)
    vt_new_bf = vt_new.astype(BF16)
    nt = (((1,), (1,)), ((), ()))

    n_pairs = ATTN_SB // 2
    for j in range(n_pairs):
        r0 = j * pair
        q8 = q_ref[r0:r0 + pair, :]
        qs = jnp.concatenate(
            [q8[:, g * KV_DIM:(g + 1) * KV_DIM] for g in range(GROUP)], axis=0)
        qrep = jnp.concatenate([qs] * N_KV_HEADS, axis=0)
        lhs = jnp.where(head == row_head, qrep, 0.0).astype(BF16)
        scores = [jnp.dot(lhs, ck_ref[b].astype(BF16), preferred_element_type=F32)
                  for b in (2 * j, 2 * j + 1)]
        s_new = jnp.dot(lhs, kt_new_bf, preferred_element_type=F32)
        s = jnp.concatenate([jnp.where(low[:, :N_BUF], scores[0], scores[1]), s_new],
                            axis=1)
        valid = valid_cache | (causal_new & (seq_of_col == 2 * j))
        s_ref[j * rows:(j + 1) * rows, :] = jnp.where(valid, s, -jnp.inf)

    for j in range(n_pairs):
        s = s_ref[j * rows:(j + 1) * rows, :]
        m = jnp.maximum(jnp.max(s, axis=-1, keepdims=True), sink)
        p = jnp.exp(s - m)
        den = jnp.sum(p, axis=-1, keepdims=True) + jnp.exp(sink - m)
        p_ref[j * rows:(j + 1) * rows, :] = (p * (1.0 / den)).astype(BF16)

    for j in range(n_pairs):
        r0 = j * pair
        p = p_ref[j * rows:(j + 1) * rows, :]
        p_old = p[:, :N_BUF]
        outs = []
        for b in (2 * j, 2 * j + 1):
            kt = ck_ref[b]
            vt = cv_ref[b]
            outs.append(lax.dot_general(p_old, vt.astype(BF16), nt,
                                        preferred_element_type=F32))
            if shift_in is not None:
                shift_in(b, kt, vt, kt_new, vt_new)
        o = jnp.where(low, outs[0], outs[1])
        o = o + lax.dot_general(p[:, N_BUF:], vt_new_bf, nt, preferred_element_type=F32)
        o = jnp.where(head == row_head, o, 0.0)
        og = o[0:grp]
        for k in range(1, N_KV_HEADS):
            og = og + o[k * grp:(k + 1) * grp]
        for g in range(GROUP):
            mix_ref[r0:r0 + pair,
                    POOL_DIM + g * KV_DIM:POOL_DIM + (g + 1) * KV_DIM] = (
                        og[g * pair:(g + 1) * pair].astype(BF16))

    o_ref[...] = x_ref[...] + jnp.dot(mix_ref[...], wo_ref[...],
                                      preferred_element_type=F32)


def _sample_attn(layer, sinks, d_rows, proj, kvts, cache_kt, cache_vt, x, pool_w,
                 pool_scale, w_out):
    phases = kvts.shape[0]
    update_cache = phases > 1
    score_shape = (ATTN_SB // 2 * N_HEADS * 2 * DEC_SEQ, N_BUF + ATTN_ROWS)
    assert phases == 1 or (phases == DEPTH and layer == DEPTH - 1)
    layer_of = (lambda p: p) if update_cache else (lambda p: layer)
    row = lambda p, i: jnp.where(p == phases - 1, i, 0)
    cache_spec = pl.BlockSpec((None, ATTN_SB, KV_DIM, N_BUF),
                              lambda p, i: (layer_of(p), i, 0, 0))
    out_shape = [jax.ShapeDtypeStruct((SAMPLE_ROWS, D_MODEL), F32)]
    out_specs = [pl.BlockSpec((ATTN_ROWS, D_MODEL), lambda p, i: (row(p, i), 0))]
    if update_cache:
        out_shape += [jax.ShapeDtypeStruct((DEPTH, DEC_BATCH, KV_DIM, N_BUF), F32)] * 2
        out_specs += [pl.BlockSpec((None, ATTN_SB, KV_DIM, N_BUF),
                                   lambda p, i: (p, i, 0, 0))] * 2
    return pl.pallas_call(
        functools.partial(_sample_attn_body, update_cache=update_cache),
        out_shape=out_shape,
        grid=(phases, DEC_BATCH // ATTN_SB),
        in_specs=[
            pl.BlockSpec(memory_space=pltpu.SMEM),
            pl.BlockSpec((ATTN_ROWS, POOL_DIM), lambda p, i: (row(p, i), 0)),
            pl.BlockSpec((ATTN_ROWS, Q_DIM), lambda p, i: (row(p, i), 1)),
            pl.BlockSpec((None, 2 * KV_DIM, ATTN_ROWS), lambda p, i: (p, 0, i)),
            cache_spec,
            cache_spec,
            pl.BlockSpec((ATTN_ROWS, D_MODEL), lambda p, i: (row(p, i), 0)),
            pl.BlockSpec((len(POOL_WINDOWS), POOL_GROUP_DIM, POOL_GROUP_DIM),
                         lambda p, i: (0, 0, 0)),
            pl.BlockSpec((1, POOL_DIM), lambda p, i: (0, 0)),
            pl.BlockSpec((D_MODEL, D_MODEL), lambda p, i: (0, 0),
                         pipeline_mode=pl.Buffered(1)),
        ],
        out_specs=out_specs,
        scratch_shapes=[pltpu.VMEM((ATTN_ROWS, D_MODEL), BF16),
                        pltpu.VMEM(score_shape, F32),
                        pltpu.VMEM(score_shape, BF16)],
        compiler_params=_params("arbitrary", "arbitrary"),
        name="sample_attn",
    )(sinks, d_rows, proj, kvts, cache_kt, cache_vt, x, pool_w, pool_scale, w_out)


def kernel(x_prompt, x_sample, cache_k, cache_v, state_pool, norm_ffn1, ffn1_gate,
           ffn1_up, ffn1_down, norm_mix, w_in, pool_w, pool_scale, attn_sinks,
           w_out, norm_ffn2, ffn2_gate, ffn2_up, ffn2_down, final_norm):
    xp = x_prompt.reshape(PROMPT_ROWS, D_MODEL)
    xs = x_sample.reshape(SAMPLE_ROWS, D_MODEL)
    gf = final_norm.reshape(1, D_MODEL)
    state_t = jnp.transpose(state_pool, (0, 2, 1, 3))

    def to_cache_t(c):
        return jnp.transpose(c, (0, 1, 3, 4, 2)).reshape(DEPTH, DEC_BATCH, KV_DIM, N_BUF)

    def from_cache_t(c):
        c = c.reshape(DEPTH, DEC_BATCH, N_KV_HEADS, HEAD_DIM, N_BUF)
        return jnp.transpose(c, (0, 1, 4, 2, 3))

    cache_kt = to_cache_t(cache_k)
    cache_vt = to_cache_t(cache_v)
    keep = min(WINDOW, SEQ)
    assert keep >= POOL_HIST
    k0 = POOL_DIM + Q_DIM

    ffn1_w = (ffn1_gate, ffn1_up, ffn1_down)
    ffn2_w = (ffn2_gate, ffn2_up, ffn2_down)
    head, *w_bf = _ffn_head(0, xp, norm_ffn1[0].reshape(1, D_MODEL), *ffn1_w)
    kp_l, vp_l, pp_l, ps_l, kvt_l = [], [], [], [], []
    for l in range(DEPTH):
        last = l == DEPTH - 1
        pool_w_l = pool_w[l].astype(BF16)
        pool_scale_l = pool_scale[l].reshape(1, POOL_DIM)
        sinks_l = attn_sinks[l].astype(F32)

        xp, xs, w_bf, (w_in_l, w_out_l) = _ffn(
            xp, xs, norm_ffn1[l].reshape(1, D_MODEL), *w_bf, gf,
            next_w=(l,) + ffn2_w, mix_w=(l, w_in, w_out),
            head=head if l == 0 else None)
        proj_p, proj_s, kvt, d_rows, u_tokens = _inproj(
            l, xp, xs, norm_mix[l].reshape(1, D_MODEL), w_in_l, state_t)

        tails = [proj_p[(b + 1) * SEQ - keep:(b + 1) * SEQ] for b in range(BATCH)]
        kp_l.append(jnp.stack([t[:, k0:k0 + KV_DIM] for t in tails])
                    .reshape(BATCH, keep, N_KV_HEADS, HEAD_DIM))
        vp_l.append(jnp.stack([t[:, k0 + KV_DIM:] for t in tails])
                    .reshape(BATCH, keep, N_KV_HEADS, HEAD_DIM))
        pp_l.append(jnp.stack([t[keep - POOL_HIST:, :POOL_DIM] for t in tails]))

        ps_l.append(u_tokens)

        xp = _prompt_mix(sinks_l, proj_p, xp, pool_w_l, pool_scale_l, w_out_l)
        kvt_l.append(kvt)
        outs = _sample_attn(
            l, sinks_l, d_rows, proj_s,
            jnp.stack(kvt_l) if last else kvt[None],
            cache_kt, cache_vt, xs, pool_w_l, pool_scale_l, w_out_l)
        xs = outs[0]
        if last:
            new_kt, new_vt = outs[1:]

        xp, xs, w_bf, _ = _ffn(xp, xs, norm_ffn2[l].reshape(1, D_MODEL), *w_bf, gf,
                               next_w=None if last else (l + 1,) + ffn1_w,
                               final_norm=last)

    y_prompt = xp.reshape(BATCH, SEQ, D_MODEL)
    y_sample = xs.reshape(DEC_BATCH, DEC_SEQ, D_MODEL)
    new_pool_sample = jnp.transpose(_sample_pool(state_t, jnp.stack(ps_l)), (0, 2, 1, 3))
    return (y_prompt, y_sample, jnp.stack(kp_l), jnp.stack(vp_l), jnp.stack(pp_l),
            from_cache_t(new_kt), from_cache_t(new_vt), new_pool_sample)
```
